```python
import jax
import jax.numpy as jnp
from jax import lax
import numpy as np

D_MODEL = 1024
BATCH = 16
SEQ = 256
DEPTH = 4
DEC_BATCH = 8
DEC_SEQ = 1024
PAST_LEN = 256

GRID_W = 64
N_MIXERS = 3
N_ATTN_LAYERS = (DEPTH + 2) // 3
N_CONV_LAYERS = (DEPTH + 1) // 3
N_HGRN_LAYERS = DEPTH // 3
N_MOD = 9
N_NORMS = 6
D_FF = 2816
FFN_RES = 0.5
ATTN_HEADS = 16
ATTN_KV_HEADS = 4
ATTN_GROUPS = ATTN_HEADS // ATTN_KV_HEADS
HEAD_DIM = 64
ATTN_DIM = ATTN_HEADS * HEAD_DIM
WINDOW = 128
BLOCK = 128
ATTN_SCALE = HEAD_DIM ** -0.5
ROPE_BASE = 10000.0
ROPE_PAIRS_PER_AXIS = HEAD_DIM // 4
CONV_WIDTH = 3
HGRN_HEADS = 8
HGRN_DK = 128
HGRN_DV = D_MODEL // HGRN_HEADS
HGRN_FD = HGRN_HEADS * HGRN_DK
CHUNK = 16
EPS = 1e-6

kernel_name = 'hybrid_diffusion_prefix_trunk_step'


def rmsnorm(x, g):
    xf = x.astype(jnp.float32)
    y = xf * lax.rsqrt(jnp.mean(xf * xf, axis=-1, keepdims=True) + EPS)
    return y.astype(x.dtype) * g


def modulate(x, shift, scale):
    return x * (1 + scale) + shift


def adaln(cond, w, b):
    m = jax.nn.silu(cond) @ w + b
    return [t[:, None, :] for t in jnp.split(m, N_MOD, axis=-1)]


def ffn_sublayer(x, shift, scale, gate, g_pre, g_post, w_in, w_out):
    h = modulate(rmsnorm(x, g_pre), shift, scale)
    a, b = jnp.split(h @ w_in, 2, axis=-1)
    h = (jax.nn.silu(a) * b) @ w_out
    return x + FFN_RES * gate * rmsnorm(h, g_post)


def axial_rope(L):
    rows = L // GRID_W
    row = jnp.repeat(jnp.arange(rows), GRID_W).astype(jnp.float32)
    col = jnp.tile(jnp.arange(GRID_W), rows).astype(jnp.float32)
    inv = ROPE_BASE ** (-jnp.arange(ROPE_PAIRS_PER_AXIS, dtype=jnp.float32) / ROPE_PAIRS_PER_AXIS)
    ang = jnp.concatenate([row[:, None] * inv, col[:, None] * inv], axis=-1)
    return jnp.cos(ang), jnp.sin(ang)


def apply_rope(x, cos, sin):
    x1, x2 = jnp.split(x, 2, axis=-1)
    c = cos[None, :, None, :]
    s = sin[None, :, None, :]
    return jnp.concatenate([x1 * c - x2 * s, x1 * s + x2 * c], axis=-1).astype(x.dtype)


def attn_project(h, w_qkv):
    B, L, _ = h.shape
    q, k, v = jnp.split(h @ w_qkv, [ATTN_DIM, ATTN_DIM + ATTN_KV_HEADS * HEAD_DIM], axis=-1)
    return (q.reshape(B, L, ATTN_HEADS, HEAD_DIM),
            k.reshape(B, L, ATTN_KV_HEADS, HEAD_DIM),
            v.reshape(B, L, ATTN_KV_HEADS, HEAD_DIM))


def sink_logits(sink, shape):
    s = sink.astype(jnp.float32).reshape(ATTN_KV_HEADS, ATTN_GROUPS)
    s = s.reshape((1,) * (len(shape) - 4) + (ATTN_KV_HEADS, ATTN_GROUPS, 1, 1))
    return jnp.broadcast_to(s, shape[:-1] + (1,))


def attn_context(h, w_qkv, w_o, sink):
    B, L, _ = h.shape
    q, k, v = attn_project(h, w_qkv)
    nqb = L // BLOCK
    qb = jnp.moveaxis(q.reshape(B, nqb, BLOCK, ATTN_KV_HEADS, ATTN_GROUPS, HEAD_DIM), 1, 0)

    def block(qi):
        s = jnp.einsum('bqkgd,bskd->bkgqs', qi, k).astype(jnp.float32) * ATTN_SCALE
        p = jax.nn.softmax(jnp.concatenate([s, sink_logits(sink, s.shape)], axis=-1), axis=-1)
        return jnp.einsum('bkgqs,bskd->bqkgd', p[..., :-1].astype(v.dtype), v)

    o = jnp.moveaxis(lax.map(block, qb), 0, 1).reshape(B, L, ATTN_DIM)
    return o @ w_o, k, v


def attn_latent(h, w_qkv, w_o, sink, k_ctx, v_ctx):
    B, L, _ = h.shape
    q, k, v = attn_project(h, w_qkv)
    cos, sin = axial_rope(L)
    q = apply_rope(q, cos, sin)
    k = apply_rope(k, cos, sin)
    nb = L // BLOCK
    qb = q.reshape(B, nb, BLOCK, ATTN_KV_HEADS, ATTN_GROUPS, HEAD_DIM)

    def band(t):
        tp = jnp.pad(t, ((0, 0), (BLOCK, BLOCK), (0, 0), (0, 0))).reshape(B, nb + 2, BLOCK, ATTN_KV_HEADS, HEAD_DIM)
        return jnp.concatenate([tp[:, :-2], tp[:, 1:-1], tp[:, 2:]], axis=2)

    kb, vb = band(k), band(v)
    n_band = 3 * BLOCK
    qi = jnp.arange(BLOCK)[:, None]
    u = jnp.arange(n_band)[None, :]
    in_window = jnp.abs(qi + BLOCK - u) <= WINDOW
    kpos = (jnp.arange(nb)[:, None] - 1) * BLOCK + jnp.arange(n_band)[None, :]
    in_range = (kpos >= 0) & (kpos < L)
    mask = in_window[None] & in_range[:, None, :]
    s_lat = jnp.einsum('bnqkgd,bnskd->bnkgqs', qb, kb).astype(jnp.float32) * ATTN_SCALE
    s_lat = jnp.where(mask[None, :, None, None], s_lat, -jnp.inf)
    s_ctx = jnp.einsum('bnqkgd,bskd->bnkgqs', qb, k_ctx).astype(jnp.float32) * ATTN_SCALE
    logits = jnp.concatenate([s_lat, s_ctx, sink_logits(sink, s_lat.shape)], axis=-1)
    p = jax.nn.softmax(logits, axis=-1).astype(v.dtype)
    o = (jnp.einsum('bnkgqs,bnskd->bnqkgd', p[..., :n_band], vb)
         + jnp.einsum('bnkgqs,bskd->bnqkgd', p[..., n_band:-1], v_ctx))
    return o.reshape(B, L, ATTN_DIM) @ w_o


def short_conv(h, w_in, cw, cb, w_out):
    L = h.shape[1]
    bg, cg, u = jnp.split(h @ w_in, 3, axis=-1)
    z = cg * u
    pad = CONV_WIDTH // 2
    zp = jnp.pad(z, ((0, 0), (pad, pad), (0, 0)))
    conv = cb + sum(zp[:, j:j + L] * cw[j] for j in range(CONV_WIDTH))
    return (bg * conv) @ w_out


def chunk_gla(q, k, v, logf, s0):
    B, L, H, _ = q.shape
    DV = v.shape[-1]
    n = L // CHUNK
    q, k, v, logf = (t.astype(jnp.float32).reshape(B, n, CHUNK, H, t.shape[-1]) for t in (q, k, v, logf))
    b = jnp.cumsum(logf, axis=2)
    tri = jnp.tril(jnp.ones((CHUNK, CHUNK), bool))
    diff = b[:, :, :, None] - b[:, :, None, :]
    decay = jnp.exp(jnp.where(tri[None, None, :, :, None, None], diff, -jnp.inf))
    scores = jnp.einsum('bnthd,bnshd,bntshd->bnhts', q, k, decay)
    o_intra = jnp.einsum('bnhts,bnshv->bnthv', scores, v)
    b_last = b[:, :, -1]
    kv = jnp.einsum('bnshd,bnshv->bnhdv', k * jnp.exp(b_last[:, :, None] - b), v)

    def step(S, xs):
        dec, kv_n = xs
        return dec[..., None] * S + kv_n, S

    S_fin, S_prev = lax.scan(step, s0.astype(jnp.float32),
                             (jnp.moveaxis(jnp.exp(b_last), 1, 0), jnp.moveaxis(kv, 1, 0)))
    S_prev = jnp.moveaxis(S_prev, 0, 1)
    o_inter = jnp.einsum('bnthd,bnhdv->bnthv', q * jnp.exp(b), S_prev)
    return (o_intra + o_inter).reshape(B, L, H, DV), S_fin


def hgrn_lower_bound(lb_param, li):
    p = jax.nn.softmax(lb_param.astype(jnp.float32), axis=0)
    return jnp.cumsum(p, axis=0)[li] - p[0]


def hgrn_mixer(h, w_in, lb_f, lb_b, norm_g, w_out, s0_f, s0_b):
    B, L, _ = h.shape
    zq, zi, zf, zb, zg = jnp.split(
        h @ w_in, [HGRN_FD, HGRN_FD + D_MODEL, 2 * HGRN_FD + D_MODEL, 3 * HGRN_FD + D_MODEL], axis=-1)
    heads = lambda t: t.reshape(B, L, HGRN_HEADS, -1)
    q = heads(jax.nn.silu(zq)) * HGRN_DK ** -0.5
    v = heads(zi)

    def gates(z, lb):
        lb = lb.reshape(HGRN_HEADS, HGRN_DK)
        f = lb + (1 - lb) * jax.nn.sigmoid(heads(z).astype(jnp.float32))
        return 1 - f, jnp.log(f)

    k_f, lf_f = gates(zf, lb_f)
    k_b, lf_b = gates(zb, lb_b)
    rev = lambda t: jnp.flip(t, axis=1)
    o_f, S_f = chunk_gla(q, k_f, v, lf_f, s0_f)
    o_b, S_b = chunk_gla(rev(q), rev(k_b), rev(v), rev(lf_b), s0_b)
    o = o_f + rev(o_b)
    o = rmsnorm(o, norm_g) * jax.nn.silu(heads(zg))
    return o.reshape(B, L, D_MODEL).astype(h.dtype) @ w_out, S_f.astype(h.dtype), S_b.astype(h.dtype)


def setup_inputs(seed: int = 0) -> dict:
    key = jax.random.key(seed)
    ks = jax.random.split(key, 24)
    nrm = lambda k, shape, s: jax.random.normal(k, shape, jnp.float32) * s
    D = D_MODEL
    qkv_dim = ATTN_DIM + 2 * ATTN_KV_HEADS * HEAD_DIM
    return {
        'x_prompt': nrm(ks[0], (BATCH, SEQ, D), 1.0),
        'x_sample': nrm(ks[1], (DEC_BATCH, DEC_SEQ, D), 1.0),
        'cache_k': nrm(ks[2], (DEC_BATCH, N_ATTN_LAYERS, PAST_LEN, ATTN_KV_HEADS, HEAD_DIM), 1.0),
        'cache_v': nrm(ks[3], (DEC_BATCH, N_ATTN_LAYERS, PAST_LEN, ATTN_KV_HEADS, HEAD_DIM), 1.0),
        'state_hgrn': nrm(ks[4], (DEC_BATCH, N_HGRN_LAYERS, 2, HGRN_HEADS, HGRN_DK, HGRN_DV), 0.5),
        'c': nrm(ks[5], (DEC_BATCH, D), 1.0),
        'c_ctx': nrm(ks[6], (D,), 1.0),
        'w_ada': nrm(ks[7], (DEPTH, D, N_MOD * D), 0.5 * D ** -0.5),
        'b_ada': nrm(ks[8], (DEPTH, N_MOD * D), 0.01),
        'norm_g': 1.0 + nrm(ks[9], (DEPTH, N_NORMS, D), 0.05),
        'w_ffn_in': nrm(ks[10], (DEPTH, 2, D, 2 * D_FF), D ** -0.5),
        'w_ffn_out': nrm(ks[11], (DEPTH, 2, D_FF, D), D_FF ** -0.5),
        'w_attn_qkv': nrm(ks[12], (N_ATTN_LAYERS, D, qkv_dim), D ** -0.5),
        'w_attn_o': nrm(ks[13], (N_ATTN_LAYERS, ATTN_DIM, D), ATTN_DIM ** -0.5),
        'attn_sink': nrm(ks[14], (N_ATTN_LAYERS, ATTN_HEADS), 1.0),
        'w_conv_in': nrm(ks[15], (N_CONV_LAYERS, D, 3 * D), D ** -0.5),
        'conv_w': nrm(ks[16], (N_CONV_LAYERS, CONV_WIDTH, D), CONV_WIDTH ** -0.5),
        'conv_b': nrm(ks[17], (N_CONV_LAYERS, D), 0.01),
        'w_conv_out': nrm(ks[18], (N_CONV_LAYERS, D, D), D ** -0.5),
        'w_hgrn_in': nrm(ks[19], (N_HGRN_LAYERS, D, 3 * HGRN_FD + 2 * D), D ** -0.5),
        'hgrn_lb': nrm(ks[20], (2, DEPTH, HGRN_FD), 1.0),
        'hgrn_norm_g': 1.0 + nrm(ks[21], (N_HGRN_LAYERS, HGRN_DV), 0.05),
        'w_hgrn_out': nrm(ks[22], (N_HGRN_LAYERS, D, D), D ** -0.5),
    }


def reference(x_prompt, x_sample, cache_k, cache_v, state_hgrn, c, c_ctx, w_ada, b_ada, norm_g,
              w_ffn_in, w_ffn_out, w_attn_qkv, w_attn_o, attn_sink, w_conv_in, conv_w, conv_b,
              w_conv_out, w_hgrn_in, hgrn_lb, hgrn_norm_g, w_hgrn_out):
    yp, ys = x_prompt, x_sample
    Bp = x_prompt.shape[0]
    Bs = x_sample.shape[0]
    new_k, new_v, new_s = [], [], []
    for li in range(DEPTH):
        kind, j = li % N_MIXERS, li // N_MIXERS
        mp = adaln(c_ctx[None, :], w_ada[li], b_ada[li])
        ms = adaln(c, w_ada[li], b_ada[li])
        g = norm_g[li]
        yp = ffn_sublayer(yp, mp[0], mp[1], mp[2], g[0], g[1], w_ffn_in[li, 0], w_ffn_out[li, 0])
        ys = ffn_sublayer(ys, ms[0], ms[1], ms[2], g[0], g[1], w_ffn_in[li, 0], w_ffn_out[li, 0])
        hp = modulate(rmsnorm(yp, g[2]), mp[3], mp[4])
        hs = modulate(rmsnorm(ys, g[2]), ms[3], ms[4])
        if kind == 0:
            op, kc, vc = attn_context(hp, w_attn_qkv[j], w_attn_o[j], attn_sink[j])
            os_ = attn_latent(hs, w_attn_qkv[j], w_attn_o[j], attn_sink[j], cache_k[:, j], cache_v[:, j])
            new_k.append(kc)
            new_v.append(vc)
        elif kind == 1:
            op = short_conv(hp, w_conv_in[j], conv_w[j], conv_b[j], w_conv_out[j])
            os_ = short_conv(hs, w_conv_in[j], conv_w[j], conv_b[j], w_conv_out[j])
        else:
            lb_f = hgrn_lower_bound(hgrn_lb[0], li)
            lb_b = hgrn_lower_bound(hgrn_lb[1], li)
            z0 = jnp.zeros((Bp, HGRN_HEADS, HGRN_DK, HGRN_DV), jnp.float32)
            op, sf, sb = hgrn_mixer(hp, w_hgrn_in[j], lb_f, lb_b, hgrn_norm_g[j], w_hgrn_out[j], z0, z0)
            os_, _, _ = hgrn_mixer(hs, w_hgrn_in[j], lb_f, lb_b, hgrn_norm_g[j], w_hgrn_out[j],
                                   state_hgrn[:, j, 0], state_hgrn[:, j, 1])
            new_s.append(jnp.stack([sf, sb], axis=1))
        yp = yp + mp[5] * rmsnorm(op, g[3])
        ys = ys + ms[5] * rmsnorm(os_, g[3])
        yp = ffn_sublayer(yp, mp[6], mp[7], mp[8], g[4], g[5], w_ffn_in[li, 1], w_ffn_out[li, 1])
        ys = ffn_sublayer(ys, ms[6], ms[7], ms[8], g[4], g[5], w_ffn_in[li, 1], w_ffn_out[li, 1])
    new_cache_k = jnp.stack(new_k, axis=1)
    new_cache_v = jnp.stack(new_v, axis=1)
    new_state_hgrn = jnp.stack(new_s, axis=1)
    return (yp, ys, new_cache_k, new_cache_v, new_state_hgrn)
```

```python
import functools

import jax
import jax.numpy as jnp
import numpy as np
from jax import lax
from jax.experimental import pallas as pl
from jax.experimental.pallas import tpu as pltpu

D_MODEL = 1024
BATCH = 16
SEQ = 256
DEPTH = 4
DEC_BATCH = 8
DEC_SEQ = 1024
PAST_LEN = 256
GRID_W = 64
N_MIXERS = 3
N_MOD = 9
N_NORMS = 6
D_FF = 2816
FFN_RES = 0.5
ATTN_HEADS = 16
ATTN_KV_HEADS = 4
HEAD_DIM = 64
ATTN_DIM = ATTN_HEADS * HEAD_DIM
KV_DIM = ATTN_KV_HEADS * HEAD_DIM
BLOCK = 128
WINDOW = 128
ATTN_SCALE = HEAD_DIM ** -0.5
ROPE_BASE = 10000.0
ROPE_PAIRS_PER_AXIS = HEAD_DIM // 4
HGRN_HEADS = 8
HGRN_DK = 128
HGRN_DV = D_MODEL // HGRN_HEADS
HGRN_FD = HGRN_HEADS * HGRN_DK
EPS = 1e-6

N_SAMPLE_TOK = DEC_BATCH * DEC_SEQ
N_PROMPT_TOK = BATCH * SEQ
N_TOK = N_SAMPLE_TOK + N_PROMPT_TOK
N_COND = 16
CTX_COND = DEC_BATCH

LANES = 128
TM = 512
CONV_TM = 1024
FFN_CHUNKS = 2
GLA_CHUNK = 128
NEG_BIG = -1e30
VMEM_LIMIT = 56 * 1024 * 1024

F32 = jnp.float32
BF16 = jnp.bfloat16


def _rms(x):
    return x * lax.rsqrt(jnp.mean(x * x, axis=-1, keepdims=True) + EPS)


def _silu(x):
    return x * jax.nn.sigmoid(x)


def _dot(a, b):
    return jnp.dot(a, b, preferred_element_type=F32)


def _dot_nt(a, b):
    return lax.dot_general(a, b, (((1,), (1,)), ((), ())), preferred_element_type=F32)


def _cond_of_tile(tm):
    per = DEC_SEQ // tm
    return lambda i: jnp.minimum(i // per, CTX_COND)


def _params(sem):
    return pltpu.CompilerParams(dimension_semantics=sem, vmem_limit_bytes=VMEM_LIMIT)


def _resident(shape):
    zeros = (0,) * len(shape)
    return pl.BlockSpec(shape, lambda *_: zeros, pipeline_mode=pl.Buffered(1))


def _ada_kernel(c_ref, w_ref, b_ref, o_ref):
    s = _silu(c_ref[...]).astype(BF16)
    o_ref[...] = _dot(s, w_ref[...].astype(BF16)) + b_ref[...]


def _ada_table(cond, w_ada, b_ada):
    tn = 1024
    n = N_MOD * D_MODEL
    out = pl.pallas_call(
        _ada_kernel,
        out_shape=jax.ShapeDtypeStruct((DEPTH, N_COND, n), F32),
        grid=(DEPTH, n // tn),
        in_specs=[
            pl.BlockSpec((N_COND, D_MODEL), lambda l, j: (0, 0)),
            pl.BlockSpec((None, D_MODEL, tn), lambda l, j: (l, 0, j)),
            pl.BlockSpec((None, 1, tn), lambda l, j: (l, 0, j)),
        ],
        out_specs=pl.BlockSpec((None, N_COND, tn), lambda l, j: (l, 0, j)),
        compiler_params=_params(("arbitrary", "arbitrary")),
        name="ada_table",
    )(cond, w_ada, b_ada.reshape(DEPTH, 1, n))
    return out.reshape(DEPTH, N_COND, N_MOD, D_MODEL)


def _ffn_kernel(x_ref, m_ref, g_ref, win_ref, wout_ref, o_ref, *, k0, g0):
    x = x_ref[...]
    m = m_ref[...]
    g = g_ref[...]
    h = _rms(x) * g[g0:g0 + 1]
    h = (h * (1.0 + m[k0 + 1:k0 + 2]) + m[k0:k0 + 1]).astype(BF16)
    tf = wout_ref.shape[1]
    acc = None
    for c in range(wout_ref.shape[0]):
        ab = _dot(h, win_ref[c])
        u = (_silu(ab[:, :tf]) * ab[:, tf:]).astype(BF16)
        part = _dot(u, wout_ref[c])
        acc = part if acc is None else acc + part
    y = _rms(acc) * g[g0 + 1:g0 + 2]
    o_ref[...] = x + FFN_RES * m[k0 + 2:k0 + 3] * y


def _ffn(x, mods, g, win, wout, which):
    k0, g0 = (0, 0) if which == 0 else (6, 4)
    return pl.pallas_call(
        functools.partial(_ffn_kernel, k0=k0, g0=g0),
        out_shape=jax.ShapeDtypeStruct((N_TOK, D_MODEL), F32),
        grid=(N_TOK // TM,),
        in_specs=[
            pl.BlockSpec((TM, D_MODEL), lambda i: (i, 0)),
            pl.BlockSpec((None, N_MOD, D_MODEL), lambda i: (_cond_of_tile(TM)(i), 0, 0)),
            pl.BlockSpec((N_NORMS, D_MODEL), lambda i: (0, 0)),
            _resident(win.shape),
            _resident(wout.shape),
        ],
        out_specs=pl.BlockSpec((TM, D_MODEL), lambda i: (i, 0)),
        compiler_params=_params(("arbitrary",)),
        name=f"ffn{which}",
    )(x, mods, g, win, wout)


def _prep_ffn(w_in, w_out):
    tf = D_FF // FFN_CHUNKS
    a = w_in[:, :D_FF].reshape(D_MODEL, FFN_CHUNKS, tf)
    b = w_in[:, D_FF:].reshape(D_MODEL, FFN_CHUNKS, tf)
    win = jnp.concatenate([a, b], axis=-1).transpose(1, 0, 2).astype(BF16)
    wout = w_out.reshape(FFN_CHUNKS, tf, D_MODEL).astype(BF16)
    return win, wout


def _mixer_in(x_ref, m_ref, g_ref):
    m = m_ref[...]
    h = _rms(x_ref[...]) * g_ref[2:3, :]
    return (h * (1.0 + m[4:5]) + m[3:4]).astype(BF16)


def _proj_kernel(x_ref, m_ref, g_ref, w_ref, o_ref):
    o_ref[...] = _dot(_mixer_in(x_ref, m_ref, g_ref), w_ref[...]).astype(o_ref.dtype)


def _proj(x, mods, g, w, out_dtype):
    n = w.shape[1]
    return pl.pallas_call(
        _proj_kernel,
        out_shape=jax.ShapeDtypeStruct((N_TOK, n), out_dtype),
        grid=(N_TOK // TM,),
        in_specs=[
            pl.BlockSpec((TM, D_MODEL), lambda i: (i, 0)),
            pl.BlockSpec((None, N_MOD, D_MODEL), lambda i: (_cond_of_tile(TM)(i), 0, 0)),
            pl.BlockSpec((N_NORMS, D_MODEL), lambda i: (0, 0)),
            _resident(w.shape),
        ],
        out_specs=pl.BlockSpec((TM, n), lambda i: (i, 0)),
        compiler_params=_params(("arbitrary",)),
        name="mixer_proj",
    )(x, mods, g, w)


def _rope_cols(t, cos, sin):
    lane = lax.broadcasted_iota(jnp.int32, (t.shape[0], LANES), 1)
    low = (lane % HEAD_DIM) < (HEAD_DIM // 2)
    outs = []
    for c in range(t.shape[1] // LANES):
        tc = t[:, c * LANES:(c + 1) * LANES]
        partner = jnp.where(low, pltpu.roll(tc, LANES - HEAD_DIM // 2, 1),
                            pltpu.roll(tc, HEAD_DIM // 2, 1))
        outs.append(tc * cos + partner * sin)
    return jnp.concatenate(outs, axis=1)


def _qkv_kernel(x_ref, m_ref, g_ref, w_ref, cos_ref, sin_ref,
                q_ref, k_ref, v_ref, kf_ref, vf_ref):
    qkv = _dot(_mixer_in(x_ref, m_ref, g_ref), w_ref[...])
    q = qkv[:, :ATTN_DIM]
    k = qkv[:, ATTN_DIM:ATTN_DIM + KV_DIM]
    v = qkv[:, ATTN_DIM + KV_DIM:]
    kf_ref[...] = k
    vf_ref[...] = v
    cos = cos_ref[...]
    sin = sin_ref[...]
    q_ref[...] = (_rope_cols(q, cos, sin) * ATTN_SCALE).astype(BF16)
    k_ref[...] = _rope_cols(k, cos, sin).astype(BF16)
    v_ref[...] = v.astype(BF16)


def _rope_tables():
    pos = np.arange(DEC_SEQ)
    row = (pos // GRID_W).astype(np.float32)
    col = (pos % GRID_W).astype(np.float32)
    inv = (ROPE_BASE ** (-np.arange(ROPE_PAIRS_PER_AXIS, dtype=np.float32) / ROPE_PAIRS_PER_AXIS)).astype(np.float32)
    ang = np.concatenate([row[:, None] * inv, col[:, None] * inv], axis=-1).astype(np.float32)
    ang = jnp.asarray(ang)
    cos, sin = jnp.cos(ang), jnp.sin(ang)
    cos = jnp.concatenate([cos, cos, cos, cos], axis=-1)
    sin = jnp.concatenate([-sin, sin, -sin, sin], axis=-1)
    cos = jnp.concatenate([cos, jnp.ones((TM, LANES), F32)], axis=0)
    sin = jnp.concatenate([sin, jnp.zeros((TM, LANES), F32)], axis=0)
    return cos, sin


def _qkv(x, mods, g, w, cos, sin):
    per = DEC_SEQ // TM
    n_sample_tiles = N_SAMPLE_TOK // TM
    tab = lambda i: (jnp.where(i < n_sample_tiles, i % per, per), 0)
    tile = lambda n: pl.BlockSpec((TM, n), lambda i: (i, 0))
    return pl.pallas_call(
        _qkv_kernel,
        out_shape=(
            jax.ShapeDtypeStruct((N_TOK, ATTN_DIM), BF16),
            jax.ShapeDtypeStruct((N_TOK, KV_DIM), BF16),
            jax.ShapeDtypeStruct((N_TOK, KV_DIM), BF16),
            jax.ShapeDtypeStruct((N_TOK, KV_DIM), F32),
            jax.ShapeDtypeStruct((N_TOK, KV_DIM), F32),
        ),
        grid=(N_TOK // TM,),
        in_specs=[
            tile(D_MODEL),
            pl.BlockSpec((None, N_MOD, D_MODEL), lambda i: (_cond_of_tile(TM)(i), 0, 0)),
            pl.BlockSpec((N_NORMS, D_MODEL), lambda i: (0, 0)),
            _resident(w.shape),
            pl.BlockSpec((TM, LANES), tab),
            pl.BlockSpec((TM, LANES), tab),
        ],
        out_specs=(tile(ATTN_DIM), tile(KV_DIM), tile(KV_DIM), tile(KV_DIM), tile(KV_DIM)),
        compiler_params=_params(("arbitrary",)),
        name="attn_qkv",
    )(x, mods, g, w, cos, sin)


def _attend(q, kall, vall, mask, sink_ref, o_ref):
    rows = q.shape[0]
    lane = lax.broadcasted_iota(jnp.int32, (kall.shape[0], LANES), 1)
    low = lane < HEAD_DIM
    rid = lax.broadcasted_iota(jnp.int32, (2 * rows, 1), 0)
    if mask is not None:
        mask = jnp.concatenate([mask, mask], axis=0)
    for g in range(ATTN_KV_HEADS):
        cg, pg = g // 2, g % 2
        kb = kall[:, cg * LANES:(cg + 1) * LANES]
        vb = vall[:, cg * LANES:(cg + 1) * LANES]
        kr = pltpu.roll(kb, HEAD_DIM, 1)
        vr = pltpu.roll(vb, HEAD_DIM, 1)
        k_lo, k_hi = (kb, kr) if pg == 0 else (kr, kb)
        v_lo, v_hi = (vb, vr) if pg == 0 else (vr, vb)
        sides = (
            (jnp.where(low, k_lo, 0.0).astype(BF16), jnp.where(low, v_lo, 0.0).astype(BF16)),
            (jnp.where(low, 0.0, k_hi).astype(BF16), jnp.where(low, 0.0, v_hi).astype(BF16)),
        )
        c0 = 2 * g * LANES
        qs = jnp.concatenate([q[:, c0:c0 + LANES], q[:, c0 + LANES:c0 + 2 * LANES]], axis=0)
        out = None
        for side, (kx, vx) in enumerate(sides):
            s = _dot_nt(qs, kx)
            if mask is not None:
                s = jnp.where(mask, s, NEG_BIG)
            sk = jnp.where(rid < rows, sink_ref[4 * g + side], sink_ref[4 * g + 2 + side])
            mx = jnp.maximum(jnp.max(s, axis=-1, keepdims=True), sk)
            p = jnp.exp(s - mx)
            den = jnp.sum(p, axis=-1, keepdims=True) + jnp.exp(sk - mx)
            part = _dot(p.astype(BF16), vx) / den
            out = part if out is None else out + part
        o_ref[:, c0:c0 + LANES] = out[:rows].astype(o_ref.dtype)
        o_ref[:, c0 + LANES:c0 + 2 * LANES] = out[rows:].astype(o_ref.dtype)


def _attn_lat_kernel(sink_ref, q_ref, kp_ref, kc_ref, kn_ref, vp_ref, vc_ref, vn_ref,
                     ck_ref, cv_ref, o_ref):
    n = pl.program_id(1)
    nb = pl.num_programs(1)
    f = lambda r: r[...].astype(F32)
    kall = jnp.concatenate([f(kp_ref), f(kc_ref), f(kn_ref), ck_ref[...]], axis=0)
    vall = jnp.concatenate([f(vp_ref), f(vc_ref), f(vn_ref), cv_ref[...]], axis=0)
    n_keys = 3 * BLOCK + PAST_LEN
    qi = lax.broadcasted_iota(jnp.int32, (BLOCK, n_keys), 0)
    u = lax.broadcasted_iota(jnp.int32, (BLOCK, n_keys), 1)
    in_window = jnp.abs(qi + BLOCK - u) <= WINDOW
    kpos = (n - 1) * BLOCK + u
    in_range = (kpos >= 0) & (kpos < nb * BLOCK)
    mask = (u >= 3 * BLOCK) | (in_window & in_range)
    _attend(q_ref[...], kall, vall, mask, sink_ref, o_ref)


def _attn_latent(sink, q, k, v, ck, cv):
    nb = DEC_SEQ // BLOCK
    qrow = lambda b, n: (b * nb + n, 0)
    prev = lambda b, n: (b * nb + jnp.maximum(n - 1, 0), 0)
    nxt = lambda b, n: (b * nb + jnp.minimum(n + 1, nb - 1), 0)
    kv = lambda im: pl.BlockSpec((BLOCK, KV_DIM), im)
    ctx = pl.BlockSpec((None, PAST_LEN, KV_DIM), lambda b, n: (b, 0, 0))
    return pl.pallas_call(
        _attn_lat_kernel,
        out_shape=jax.ShapeDtypeStruct((N_SAMPLE_TOK, ATTN_DIM), BF16),
        grid=(DEC_BATCH, nb),
        in_specs=[
            pl.BlockSpec(memory_space=pltpu.SMEM),
            pl.BlockSpec((BLOCK, ATTN_DIM), qrow),
            kv(prev), kv(qrow), kv(nxt), kv(prev), kv(qrow), kv(nxt),
            ctx, ctx,
        ],
        out_specs=pl.BlockSpec((BLOCK, ATTN_DIM), qrow),
        compiler_params=_params(("arbitrary", "arbitrary")),
        name="attn_latent",
    )(sink, q, k, k, k, v, v, v, ck, cv)


def _attn_ctx_kernel(sink_ref, q_ref, k_ref, v_ref, o_ref):
    _attend(q_ref[...], k_ref[...].astype(F32), v_ref[...].astype(F32), None, sink_ref, o_ref)


def _attn_context(sink, q, k, v):
    off = N_SAMPLE_TOK // SEQ
    row = lambda b: (off + b, 0)
    return pl.pallas_call(
        _attn_ctx_kernel,
        out_shape=jax.ShapeDtypeStruct((N_PROMPT_TOK, ATTN_DIM), BF16),
        grid=(BATCH,),
        in_specs=[
            pl.BlockSpec(memory_space=pltpu.SMEM),
            pl.BlockSpec((SEQ, ATTN_DIM), row),
            pl.BlockSpec((SEQ, KV_DIM), row),
            pl.BlockSpec((SEQ, KV_DIM), row),
        ],
        out_specs=pl.BlockSpec((SEQ, ATTN_DIM), lambda b: (b, 0)),
        compiler_params=_params(("arbitrary",)),
        name="attn_context",
    )(sink, q, k, v)


def _outproj_kernel(a_ref, x_ref, m_ref, g_ref, w_ref, o_ref):
    y = _rms(_dot(a_ref[...], w_ref[...])) * g_ref[3:4, :]
    o_ref[...] = x_ref[...] + m_ref[5:6, :] * y


def _outproj(a, x, mods, g, w):
    return pl.pallas_call(
        _outproj_kernel,
        out_shape=jax.ShapeDtypeStruct((N_TOK, D_MODEL), F32),
        grid=(N_TOK // TM,),
        in_specs=[
            pl.BlockSpec((TM, D_MODEL), lambda i: (i, 0)),
            pl.BlockSpec((TM, D_MODEL), lambda i: (i, 0)),
            pl.BlockSpec((None, N_MOD, D_MODEL), lambda i: (_cond_of_tile(TM)(i), 0, 0)),
            pl.BlockSpec((N_NORMS, D_MODEL), lambda i: (0, 0)),
            _resident(w.shape),
        ],
        out_specs=pl.BlockSpec((TM, D_MODEL), lambda i: (i, 0)),
        compiler_params=_params(("arbitrary",)),
        name="mixer_out",
    )(a, x, mods, g, w)


CONV_COLS = 256


def _conv_kernel(x_ref, m_ref, g_ref, win_ref, cw_ref, cb_ref, wout_ref, o_ref, a_ref):
    i = pl.program_id(0)
    h = _mixer_in(x_ref, m_ref, g_ref)
    tm = h.shape[0]
    seq = jnp.where(i < N_SAMPLE_TOK // tm, DEC_SEQ, SEQ)
    pos = lax.broadcasted_iota(jnp.int32, (tm, CONV_COLS), 0) & (seq - 1)
    first = pos == 0
    last = pos == seq - 1
    cw = cw_ref[...]
    cb = cb_ref[...]
    for c in range(D_MODEL // CONV_COLS):
        cols = slice(c * CONV_COLS, (c + 1) * CONV_COLS)
        z3 = _dot(h, win_ref[c])
        bg = z3[:, :CONV_COLS]
        z = z3[:, CONV_COLS:2 * CONV_COLS] * z3[:, 2 * CONV_COLS:]
        zl = jnp.where(first, 0.0, pltpu.roll(z, 1, 0))
        zr = jnp.where(last, 0.0, pltpu.roll(z, tm - 1, 0))
        conv = cb[:, cols] + zl * cw[0:1, cols] + z * cw[1:2, cols] + zr * cw[2:3, cols]
        a_ref[:, cols] = (bg * conv).astype(BF16)
    y = _rms(_dot(a_ref[...], wout_ref[...])) * g_ref[3:4, :]
    o_ref[...] = x_ref[...] + m_ref[5:6, :] * y


def _conv_layer(x, mods, g, w_in, cw, cb, w_out):
    nc = D_MODEL // CONV_COLS
    parts = [w_in[:, j * D_MODEL:(j + 1) * D_MODEL].reshape(D_MODEL, nc, CONV_COLS) for j in range(3)]
    win = jnp.concatenate(parts, axis=-1).transpose(1, 0, 2).astype(BF16)
    wout = w_out.astype(BF16)
    tm = CONV_TM
    return pl.pallas_call(
        _conv_kernel,
        out_shape=jax.ShapeDtypeStruct((N_TOK, D_MODEL), F32),
        grid=(N_TOK // tm,),
        in_specs=[
            pl.BlockSpec((tm, D_MODEL), lambda i: (i, 0)),
            pl.BlockSpec((None, N_MOD, D_MODEL), lambda i: (_cond_of_tile(tm)(i), 0, 0)),
            pl.BlockSpec((N_NORMS, D_MODEL), lambda i: (0, 0)),
            _resident(win.shape),
            pl.BlockSpec((3, D_MODEL), lambda i: (0, 0)),
            pl.BlockSpec((1, D_MODEL), lambda i: (0, 0)),
            _resident(wout.shape),
        ],
        out_specs=pl.BlockSpec((tm, D_MODEL), lambda i: (i, 0)),
        scratch_shapes=[pltpu.VMEM((tm, D_MODEL), BF16)],
        compiler_params=_params(("arbitrary",)),
        name="conv_layer",
    )(x, mods, g, win, cw, cb.reshape(1, D_MODEL), wout)


def _gla_chunk(q, k, lf, v_bf, vt_bf, st_ref, xor_ts, reverse):
    c = q.shape[0]
    rowi = lax.broadcasted_iota(jnp.int32, (c, LANES), 0)
    p = lf
    tot = lf
    attn = jnp.zeros((c, c), F32)
    w = 1
    while w < c:
        second = (rowi & w) != 0
        if not reverse:
            e = jnp.exp(jnp.where(second, p, tot - p))
            qh = jnp.where(second, q * e, 0.0)
            kh = jnp.where(second, 0.0, k * e)
        else:
            e = jnp.exp(jnp.where(second, p - lf, tot - p + lf))
            qh = jnp.where(second, 0.0, q * e)
            kh = jnp.where(second, k * e, 0.0)
        a = _dot_nt(qh.astype(BF16), kh.astype(BF16))
        attn = attn + jnp.where(xor_ts < 2 * w, a, 0.0)
        up = pltpu.roll(tot, w, 0)
        dn = pltpu.roll(tot, c - w, 0)
        p = p + jnp.where(second, up, 0.0)
        tot = tot + jnp.where(second, up, dn)
        w *= 2
    if not reverse:
        eq, ek = p, tot - p
    else:
        eq, ek = tot - p + lf, p - lf
    qt = (q * jnp.exp(eq)).astype(BF16)
    kt = (k * jnp.exp(ek)).astype(BF16)
    st = st_ref[...]
    diag = jnp.sum(q * k, axis=-1, keepdims=True)
    o = _dot(attn.astype(BF16), v_bf) + _dot_nt(qt, st.astype(BF16)) + diag * v_bf.astype(F32)
    st_ref[...] = st * jnp.exp(tot[0:1, :]) + _dot(vt_bf, kt)
    return o


def _hgrn_kernel(*refs, seq, li, has_init, emit_state):
    lb_ref, zq_ref, zi_ref, zf_ref, zb_ref, zg_ref, ng_ref = refs[:7]
    refs = refs[7:]
    s0_ref = None
    if has_init:
        s0_ref, refs = refs[0], refs[1:]
    o_ref, refs = refs[0], refs[1:]
    sout_ref = None
    if emit_state:
        sout_ref, refs = refs[0], refs[1:]
    acc_ref, stf_ref, stb_ref = refs
    c = GLA_CHUNK
    nc = seq // c

    def lower_bound(x):
        e = jnp.exp(x - jnp.max(x, axis=0, keepdims=True))
        pr = e / jnp.sum(e, axis=0, keepdims=True)
        return jnp.sum(pr[1:li + 1], axis=0, keepdims=True)

    lbs = (lower_bound(lb_ref[0]), lower_bound(lb_ref[1]))
    if has_init:
        stf_ref[...] = s0_ref[0].T
        stb_ref[...] = s0_ref[1].T
    else:
        stf_ref[...] = jnp.zeros_like(stf_ref)
        stb_ref[...] = jnp.zeros_like(stb_ref)
    acc_ref[...] = jnp.zeros_like(acc_ref)
    xor_ts = (lax.broadcasted_iota(jnp.int32, (c, c), 0) ^ lax.broadcasted_iota(jnp.int32, (c, c), 1))

    def one(n, z_ref, st_ref, lb, reverse):
        rows = pl.ds(pl.multiple_of(n * c, c), c)
        q = _silu(zq_ref[rows, :]) * (HGRN_DK ** -0.5)
        v = zi_ref[rows, :]
        f = lb + (1.0 - lb) * jax.nn.sigmoid(z_ref[rows, :])
        o = _gla_chunk(q, 1.0 - f, jnp.log(f), v.astype(BF16), v.T.astype(BF16), st_ref, xor_ts, reverse)
        acc_ref[rows, :] += o

    def body(n, carry):
        one(n, zf_ref, stf_ref, lbs[0], False)
        one(nc - 1 - n, zb_ref, stb_ref, lbs[1], True)
        return carry

    lax.fori_loop(0, nc, body, 0)
    o_ref[...] = (_rms(acc_ref[...]) * ng_ref[...] * _silu(zg_ref[...])).astype(o_ref.dtype)
    if emit_state:
        sout_ref[0] = stf_ref[...].T
        sout_ref[1] = stb_ref[...].T


def _hgrn_core(z, lb, ng, s0, *, batch, seq, row_off, li, emit_state):
    has_init = s0 is not None
    blk0 = row_off // seq
    fd = HGRN_FD // LANES

    def col(c0):
        return pl.BlockSpec((seq, LANES), lambda b, h: (blk0 + b, c0 + h))

    in_specs = [
        pl.BlockSpec((2, DEPTH, LANES), lambda b, h: (0, 0, h)),
        col(0), col(fd), col(fd + D_MODEL // LANES), col(2 * fd + D_MODEL // LANES),
        col(3 * fd + D_MODEL // LANES),
        pl.BlockSpec((1, HGRN_DV), lambda b, h: (0, 0)),
    ]
    args = [lb, z, z, z, z, z, ng.reshape(1, HGRN_DV)]
    state_spec = pl.BlockSpec((None, 2, None, HGRN_DK, HGRN_DV), lambda b, h: (b, 0, h, 0, 0))
    if has_init:
        in_specs.append(state_spec)
        args.append(s0)
    out_shape = [jax.ShapeDtypeStruct((batch * seq, D_MODEL), BF16)]
    out_specs = [pl.BlockSpec((seq, LANES), lambda b, h: (b, h))]
    if emit_state:
        out_shape.append(jax.ShapeDtypeStruct((batch, 2, HGRN_HEADS, HGRN_DK, HGRN_DV), F32))
        out_specs.append(state_spec)
    return pl.pallas_call(
        functools.partial(_hgrn_kernel, seq=seq, li=li, has_init=has_init, emit_state=emit_state),
        out_shape=tuple(out_shape),
        grid=(batch, HGRN_HEADS),
        in_specs=in_specs,
        out_specs=tuple(out_specs),
        scratch_shapes=[
            pltpu.VMEM((seq, HGRN_DV), F32),
            pltpu.VMEM((HGRN_DV, HGRN_DK), F32),
            pltpu.VMEM((HGRN_DV, HGRN_DK), F32),
        ],
        compiler_params=_params(("arbitrary", "arbitrary")),
        name=f"hgrn_core_{seq}",
    )(*args)


def kernel(x_prompt, x_sample, cache_k, cache_v, state_hgrn, c, c_ctx, w_ada, b_ada, norm_g, w_ffn_in, w_ffn_out, w_attn_qkv, w_attn_o, attn_sink, w_conv_in, conv_w, conv_b, w_conv_out, w_hgrn_in, hgrn_lb, hgrn_norm_g, w_hgrn_out):
    x = jnp.concatenate([x_sample.reshape(N_SAMPLE_TOK, D_MODEL),
                         x_prompt.reshape(N_PROMPT_TOK, D_MODEL)], axis=0)
    cond = jnp.concatenate([c, c_ctx[None, :],
                            jnp.zeros((N_COND - DEC_BATCH - 1, D_MODEL), F32)], axis=0)
    mods_all = _ada_table(cond, w_ada, b_ada)
    cos, sin = _rope_tables()
    ck = cache_k.reshape(DEC_BATCH, -1, PAST_LEN, KV_DIM)
    cv = cache_v.reshape(DEC_BATCH, -1, PAST_LEN, KV_DIM)
    new_k, new_v, new_s = [], [], []
    for li in range(DEPTH):
        kind, j = li % N_MIXERS, li // N_MIXERS
        mods = mods_all[li]
        g = norm_g[li]
        x = _ffn(x, mods, g, *_prep_ffn(w_ffn_in[li, 0], w_ffn_out[li, 0]), 0)
        if kind == 0:
            q, k, v, kf, vf = _qkv(x, mods, g, w_attn_qkv[j].astype(BF16), cos, sin)
            o_s = _attn_latent(attn_sink[j], q, k, v, ck[:, j], cv[:, j])
            o_p = _attn_context(attn_sink[j], q, k, v)
            a = jnp.concatenate([o_s, o_p], axis=0)
            x = _outproj(a, x, mods, g, w_attn_o[j].astype(BF16))
            new_k.append(kf[N_SAMPLE_TOK:].reshape(BATCH, SEQ, ATTN_KV_HEADS, HEAD_DIM))
            new_v.append(vf[N_SAMPLE_TOK:].reshape(BATCH, SEQ, ATTN_KV_HEADS, HEAD_DIM))
        elif kind == 1:
            x = _conv_layer(x, mods, g, w_conv_in[j], conv_w[j], conv_b[j], w_conv_out[j])
        else:
            z = _proj(x, mods, g, w_hgrn_in[j].astype(BF16), F32)
            o_s = _hgrn_core(z, hgrn_lb, hgrn_norm_g[j], state_hgrn[:, j], batch=DEC_BATCH,
                             seq=DEC_SEQ, row_off=0, li=li, emit_state=False)[0]
            o_p, st = _hgrn_core(z, hgrn_lb, hgrn_norm_g[j], None, batch=BATCH, seq=SEQ,
                                 row_off=N_SAMPLE_TOK, li=li, emit_state=True)
            a = jnp.concatenate([o_s, o_p], axis=0)
            x = _outproj(a, x, mods, g, w_hgrn_out[j].astype(BF16))
            new_s.append(st)
        x = _ffn(x, mods, g, *_prep_ffn(w_ffn_in[li, 1], w_ffn_out[li, 1]), 1)
    y_sample = x[:N_SAMPLE_TOK].reshape(DEC_BATCH, DEC_SEQ, D_MODEL)
    y_prompt = x[N_SAMPLE_TOK:].reshape(BATCH, SEQ, D_MODEL)
    return (y_prompt, y_sample, jnp.stack(new_k, axis=1), jnp.stack(new_v, axis=1),
            jnp.stack(new_s, axis=1))
```

```python
import functools
import math

import jax
import jax.numpy as jnp
import numpy as np
from jax import lax
from jax.experimental import pallas as pl
from jax.experimental.pallas import tpu as pltpu

D_MODEL = 1024
BATCH = 16
SEQ = 256
DEPTH = 4
DEC_BATCH = 8
DEC_SEQ = 1024
PAST_LEN = 256
GRID_W = 64
N_MIXERS = 3
N_MOD = 9
N_NORMS = 6
D_FF = 2816
FFN_RES = 0.5
ATTN_HEADS = 16
ATTN_KV_HEADS = 4
HEAD_DIM = 64
ATTN_DIM = ATTN_HEADS * HEAD_DIM
KV_DIM = ATTN_KV_HEADS * HEAD_DIM
BLOCK = 128
WINDOW = 128
ATTN_SCALE = HEAD_DIM ** -0.5
LOG2E = math.log2(math.e)
ROPE_BASE = 10000.0
ROPE_PAIRS_PER_AXIS = HEAD_DIM // 4
HGRN_HEADS = 8
HGRN_DK = 128
HGRN_DV = D_MODEL // HGRN_HEADS
HGRN_FD = HGRN_HEADS * HGRN_DK
EPS = 1e-6

N_SAMPLE_TOK = DEC_BATCH * DEC_SEQ
N_PROMPT_TOK = BATCH * SEQ
N_TOK = N_SAMPLE_TOK + N_PROMPT_TOK
N_COND = 16
CTX_COND = DEC_BATCH

LANES = 128
SUBLANES = 8
TM = 512
CONV_TM = 1024
FFN_SPLITS = (0, 1024, 2048, D_FF)
GLA_CHUNK = 128
NEG_BIG = -1e30
VMEM_LIMIT = 56 * 1024 * 1024

F32 = jnp.float32
BF16 = jnp.bfloat16


def _rms(x):
    return x * lax.rsqrt(jnp.mean(x * x, axis=-1, keepdims=True) + EPS)


def _silu(x):
    return x * jax.nn.sigmoid(x)


def _dot(a, b):
    return jnp.dot(a, b, preferred_element_type=F32)


def _dot_nt(a, b):
    return lax.dot_general(a, b, (((1,), (1,)), ((), ())), preferred_element_type=F32)


def _params(sem):
    return pltpu.CompilerParams(dimension_semantics=sem, vmem_limit_bytes=VMEM_LIMIT)


def _resident(arr, *lead):
    shape = (None,) * len(lead) + arr.shape[len(lead):]
    idx = tuple(lead) + (0,) * (arr.ndim - len(lead))
    return pl.BlockSpec(shape, lambda *_: idx, pipeline_mode=pl.Buffered(1))


def _mods_spec(li, tm):
    per = DEC_SEQ // tm
    return pl.BlockSpec((None, None, N_MOD, D_MODEL),
                        lambda i: (li, jnp.minimum(i // per, CTX_COND), 0, 0))


def _norm_spec(li):
    return pl.BlockSpec((None, N_NORMS, D_MODEL), lambda i: (li, 0, 0))


def _tok_spec(tm, n):
    return pl.BlockSpec((tm, n), lambda i: (i, 0))


def _split_specs(tm, n):
    ns = N_SAMPLE_TOK // tm
    return [pl.BlockSpec((tm, n), lambda i: (jnp.minimum(i, ns - 1), 0)),
            pl.BlockSpec((tm, n), lambda i: (jnp.maximum(i - ns, 0), 0))]


def _load_split(s_ref, p_ref, tm):
    return jnp.where(pl.program_id(0) < N_SAMPLE_TOK // tm, s_ref[...], p_ref[...])


def _store_split(s_ref, p_ref, val, tm):
    ns = N_SAMPLE_TOK // tm
    i = pl.program_id(0)

    @pl.when(i < ns)
    def _():
        s_ref[...] = val

    @pl.when(i >= ns)
    def _():
        p_ref[...] = val


def _ada_kernel(c_ref, w_ref, b_ref, o_ref):
    s = _silu(c_ref[...]).astype(BF16)
    o_ref[...] = _dot(s, w_ref[...].astype(BF16)) + b_ref[...]


def _ada_table(cond, w_ada, b_ada):
    tn = 1024
    n = N_MOD * D_MODEL
    out = pl.pallas_call(
        _ada_kernel,
        out_shape=jax.ShapeDtypeStruct((DEPTH, N_COND, n), F32),
        grid=(DEPTH, n // tn),
        in_specs=[
            pl.BlockSpec((N_COND, D_MODEL), lambda l, j: (0, 0)),
            pl.BlockSpec((None, D_MODEL, tn), lambda l, j: (l, 0, j)),
            pl.BlockSpec((None, 1, tn), lambda l, j: (l, 0, j)),
        ],
        out_specs=pl.BlockSpec((None, N_COND, tn), lambda l, j: (l, 0, j)),
        compiler_params=_params(("arbitrary", "arbitrary")),
        name="ada_table",
    )(cond, w_ada, b_ada.reshape(DEPTH, 1, n))
    return out.reshape(DEPTH, N_COND, N_MOD, D_MODEL)


def _ffn_kernel(*refs, k0, g0, split_in, split_out):
    n_in = 2 if split_in else 1
    x_refs, refs = refs[:n_in], refs[n_in:]
    m_ref, g_ref, win_ref, wout_ref = refs[:4]
    o_refs = refs[4:]
    x = _load_split(*x_refs, TM) if split_in else x_refs[0][...]
    m = m_ref[...]
    g = g_ref[...]
    h = _rms(x) * g[g0:g0 + 1]
    h = (h * (1.0 + m[k0 + 1:k0 + 2]) + m[k0:k0 + 1]).astype(BF16)
    acc = None
    for lo, hi in zip(FFN_SPLITS[:-1], FFN_SPLITS[1:]):
        a = _dot(h, win_ref[:, lo:hi])
        b = _dot(h, win_ref[:, D_FF + lo:D_FF + hi])
        u = (_silu(a) * b).astype(BF16)
        part = _dot(u, wout_ref[lo:hi, :])
        acc = part if acc is None else acc + part
    y = _rms(acc) * g[g0 + 1:g0 + 2]
    res = x + FFN_RES * m[k0 + 2:k0 + 3] * y
    if split_out:
        _store_split(*o_refs, res, TM)
    else:
        o_refs[0][...] = res


def _ffn(xs, mods, norm_g, w_in, w_out, li, which, split_out=False):
    k0, g0 = (0, 0) if which == 0 else (6, 4)
    split_in = len(xs) == 2
    x_specs = _split_specs(TM, D_MODEL) if split_in else [_tok_spec(TM, D_MODEL)]
    if split_out:
        out_shape = (jax.ShapeDtypeStruct((N_SAMPLE_TOK, D_MODEL), F32),
                     jax.ShapeDtypeStruct((N_PROMPT_TOK, D_MODEL), F32))
        out_specs = tuple(_split_specs(TM, D_MODEL))
    else:
        out_shape = jax.ShapeDtypeStruct((N_TOK, D_MODEL), F32)
        out_specs = _tok_spec(TM, D_MODEL)
    return pl.pallas_call(
        functools.partial(_ffn_kernel, k0=k0, g0=g0, split_in=split_in, split_out=split_out),
        out_shape=out_shape,
        grid=(N_TOK // TM,),
        in_specs=x_specs + [
            _mods_spec(li, TM), _norm_spec(li),
            _resident(w_in, li, which), _resident(w_out, li, which),
        ],
        out_specs=out_specs,
        compiler_params=_params(("arbitrary",)),
        name=f"ffn{which}",
    )(*xs, mods, norm_g, w_in, w_out)


def _mixer_in(x_ref, m_ref, g_ref):
    m = m_ref[...]
    h = _rms(x_ref[...]) * g_ref[2:3, :]
    return (h * (1.0 + m[4:5]) + m[3:4]).astype(BF16)


def _proj_kernel(x_ref, m_ref, g_ref, w_ref, o_ref):
    o_ref[...] = _dot(_mixer_in(x_ref, m_ref, g_ref), w_ref[...]).astype(o_ref.dtype)


def _proj(x, mods, norm_g, w, li, j, out_dtype):
    n = w.shape[-1]
    return pl.pallas_call(
        _proj_kernel,
        out_shape=jax.ShapeDtypeStruct((N_TOK, n), out_dtype),
        grid=(N_TOK // TM,),
        in_specs=[_tok_spec(TM, D_MODEL), _mods_spec(li, TM), _norm_spec(li), _resident(w, j)],
        out_specs=_tok_spec(TM, n),
        compiler_params=_params(("arbitrary",)),
        name="mixer_proj",
    )(x, mods, norm_g, w)


def _rope_cols(t, cos, sin):
    lane = lax.broadcasted_iota(jnp.int32, (t.shape[0], LANES), 1)
    low = (lane % HEAD_DIM) < (HEAD_DIM // 2)
    outs = []
    for c in range(t.shape[1] // LANES):
        tc = t[:, c * LANES:(c + 1) * LANES]
        partner = jnp.where(low, pltpu.roll(tc, LANES - HEAD_DIM // 2, 1),
                            pltpu.roll(tc, HEAD_DIM // 2, 1))
        outs.append(tc * cos + partner * sin)
    return jnp.concatenate(outs, axis=1)


def _qkv_kernel(x_ref, m_ref, g_ref, w_ref, cos_ref, sin_ref,
                q_ref, k_ref, v_ref, kf_ref, vf_ref):
    qkv = _dot(_mixer_in(x_ref, m_ref, g_ref), w_ref[...])
    q = qkv[:, :ATTN_DIM]
    k = qkv[:, ATTN_DIM:ATTN_DIM + KV_DIM]
    v = qkv[:, ATTN_DIM + KV_DIM:]

    @pl.when(pl.program_id(0) >= N_SAMPLE_TOK // TM)
    def _():
        kf_ref[...] = k
        vf_ref[...] = v

    cos = cos_ref[...]
    sin = sin_ref[...]
    q_ref[...] = (_rope_cols(q, cos, sin) * (ATTN_SCALE * LOG2E)).astype(BF16)
    k_ref[...] = _rope_cols(k, cos, sin).astype(BF16)
    v_ref[...] = v.astype(BF16)


def _rope_tables():
    pos = np.arange(DEC_SEQ)
    row = (pos // GRID_W).astype(np.float32)
    col = (pos % GRID_W).astype(np.float32)
    inv = (ROPE_BASE ** (-np.arange(ROPE_PAIRS_PER_AXIS, dtype=np.float32) / ROPE_PAIRS_PER_AXIS)).astype(np.float32)
    ang = jnp.asarray(np.concatenate([row[:, None] * inv, col[:, None] * inv], axis=-1).astype(np.float32))
    cos, sin = jnp.cos(ang), jnp.sin(ang)
    cos = jnp.concatenate([cos, cos, cos, cos], axis=-1)
    sin = jnp.concatenate([-sin, sin, -sin, sin], axis=-1)
    cos = jnp.concatenate([cos, jnp.ones((TM, LANES), F32)], axis=0)
    sin = jnp.concatenate([sin, jnp.zeros((TM, LANES), F32)], axis=0)
    return cos, sin


def _qkv(x, mods, norm_g, w, li, j, cos, sin):
    per = DEC_SEQ // TM
    ns = N_SAMPLE_TOK // TM
    tab = pl.BlockSpec((TM, LANES), lambda i: (jnp.where(i < ns, i % per, per), 0))
    cache = pl.BlockSpec((TM, KV_DIM), lambda i: (jnp.maximum(i - ns, 0), 0))
    return pl.pallas_call(
        _qkv_kernel,
        out_shape=(
            jax.ShapeDtypeStruct((N_TOK, ATTN_DIM), BF16),
            jax.ShapeDtypeStruct((N_TOK, KV_DIM), BF16),
            jax.ShapeDtypeStruct((N_TOK, KV_DIM), BF16),
            jax.ShapeDtypeStruct((N_PROMPT_TOK, KV_DIM), F32),
            jax.ShapeDtypeStruct((N_PROMPT_TOK, KV_DIM), F32),
        ),
        grid=(N_TOK // TM,),
        in_specs=[_tok_spec(TM, D_MODEL), _mods_spec(li, TM), _norm_spec(li), _resident(w, j), tab, tab],
        out_specs=(_tok_spec(TM, ATTN_DIM), _tok_spec(TM, KV_DIM), _tok_spec(TM, KV_DIM), cache, cache),
        compiler_params=_params(("arbitrary",)),
        name="attn_qkv",
    )(x, mods, norm_g, w, cos, sin)


N_HEAD_SLOTS = 2 * ATTN_KV_HEADS


def _fill_head_slots(x, k_dst, v_dst, row0, is_value):
    dst = v_dst if is_value else k_dst
    pad = 1.0 if is_value else 0.0
    rows = x.shape[0]
    low = lax.broadcasted_iota(jnp.int32, (rows, LANES), 1) < HEAD_DIM
    for g in range(ATTN_KV_HEADS):
        cg, pg = g // 2, g % 2
        xb = x[:, cg * LANES:(cg + 1) * LANES]
        xr = pltpu.roll(xb, HEAD_DIM, 1)
        x_lo, x_hi = (xb, xr) if pg == 0 else (xr, xb)
        dst[2 * g, pl.ds(row0, rows), :] = jnp.where(low, x_lo, pad).astype(BF16)
        dst[2 * g + 1, pl.ds(row0, rows), :] = jnp.where(low, pad, x_hi).astype(BF16)


def _attend_rows(q, key_blocks, sink_ref, o_ref, orow0):
    rows = q.shape[0]
    rid = lax.broadcasted_iota(jnp.int32, (2 * rows, 1), 0)
    low = lax.broadcasted_iota(jnp.int32, (2 * rows, LANES), 1) < HEAD_DIM
    for g in range(ATTN_KV_HEADS):
        c0 = 2 * g * LANES
        qs = jnp.concatenate([q[:, c0:c0 + LANES], q[:, c0 + LANES:c0 + 2 * LANES]], axis=0)
        res = []
        for side in range(2):
            slot = 2 * g + side
            sk = jnp.where(rid < rows, sink_ref[4 * g + side], sink_ref[4 * g + 2 + side]) * LOG2E
            scores = []
            mx = sk
            for k_of, _, bias in key_blocks:
                s = _dot_nt(qs, k_of(slot))
                if bias is not None:
                    s = s + bias
                scores.append(s)
                mx = jnp.maximum(mx, jnp.max(s, axis=-1, keepdims=True))
            part = None
            for s, (_, v_of, _) in zip(scores, key_blocks):
                pv = _dot(jnp.exp2(s - mx).astype(BF16), v_of(slot))
                part = pv if part is None else part + pv
            den = pltpu.roll(part, HEAD_DIM, 1) + jnp.exp2(sk - mx)
            res.append(part / den)
        out = jnp.where(low, res[0], res[1]).astype(o_ref.dtype)
        o_ref[pl.ds(orow0, rows), c0:c0 + LANES] = out[:rows]
        o_ref[pl.ds(orow0, rows), c0 + LANES:c0 + 2 * LANES] = out[rows:]


def _attn_lat_kernel(sink_ref, q_ref, k_ref, v_ref, ck_ref, cv_ref, o_ref,
                     kl_ref, vl_ref, kc_ref, vc_ref):
    nb = DEC_SEQ // BLOCK
    zeros = jnp.zeros((BLOCK, LANES), BF16)
    for slot in range(N_HEAD_SLOTS):
        for ref in (kl_ref, vl_ref):
            ref[slot, pl.ds(0, BLOCK), :] = zeros
            ref[slot, pl.ds(DEC_SEQ + BLOCK, BLOCK), :] = zeros
    _fill_head_slots(k_ref[...].astype(F32), kl_ref, vl_ref, BLOCK, False)
    _fill_head_slots(v_ref[...].astype(F32), kl_ref, vl_ref, BLOCK, True)
    _fill_head_slots(ck_ref[...], kc_ref, vc_ref, 0, False)
    _fill_head_slots(cv_ref[...], kc_ref, vc_ref, 0, True)
    n_band = 3 * BLOCK
    qi = lax.broadcasted_iota(jnp.int32, (2 * BLOCK, n_band), 0) % BLOCK
    u = lax.broadcasted_iota(jnp.int32, (2 * BLOCK, n_band), 1)
    in_window = jnp.abs(qi + BLOCK - u) <= WINDOW

    def body(n, carry):
        r0 = pl.multiple_of(n * BLOCK, BLOCK)
        kpos = (n - 1) * BLOCK + u
        ok = in_window & (kpos >= 0) & (kpos < nb * BLOCK)
        bias = jnp.where(ok, 0.0, NEG_BIG)
        band = (lambda s: kl_ref[s, pl.ds(r0, n_band), :], lambda s: vl_ref[s, pl.ds(r0, n_band), :], bias)
        ctx = (lambda s: kc_ref[s], lambda s: vc_ref[s], None)
        _attend_rows(q_ref[pl.ds(r0, BLOCK), :], [band, ctx], sink_ref, o_ref, r0)
        return carry

    lax.fori_loop(0, nb, body, 0)


def _attn_latent(sink, q, k, v, ck, cv, j):
    seq = lambda n: pl.BlockSpec((DEC_SEQ, n), lambda b: (b, 0))
    ctx = pl.BlockSpec((None, None, PAST_LEN, KV_DIM), lambda b: (b, j, 0, 0))
    return pl.pallas_call(
        _attn_lat_kernel,
        out_shape=jax.ShapeDtypeStruct((N_SAMPLE_TOK, ATTN_DIM), BF16),
        grid=(DEC_BATCH,),
        in_specs=[pl.BlockSpec(memory_space=pltpu.SMEM), seq(ATTN_DIM), seq(KV_DIM), seq(KV_DIM), ctx, ctx],
        out_specs=seq(ATTN_DIM),
        scratch_shapes=[
            pltpu.VMEM((N_HEAD_SLOTS, DEC_SEQ + 2 * BLOCK, LANES), BF16),
            pltpu.VMEM((N_HEAD_SLOTS, DEC_SEQ + 2 * BLOCK, LANES), BF16),
            pltpu.VMEM((N_HEAD_SLOTS, PAST_LEN, LANES), BF16),
            pltpu.VMEM((N_HEAD_SLOTS, PAST_LEN, LANES), BF16),
        ],
        compiler_params=_params(("arbitrary",)),
        name="attn_latent",
    )(sink, q, k, v, ck, cv)


def _attn_ctx_kernel(sink_ref, q_ref, k_ref, v_ref, o_ref, ks_ref, vs_ref):
    _fill_head_slots(k_ref[...].astype(F32), ks_ref, vs_ref, 0, False)
    _fill_head_slots(v_ref[...].astype(F32), ks_ref, vs_ref, 0, True)
    blk = (lambda s: ks_ref[s], lambda s: vs_ref[s], None)
    _attend_rows(q_ref[...], [blk], sink_ref, o_ref, 0)


def _attn_context(sink, q, k, v):
    off = N_SAMPLE_TOK // SEQ
    row = lambda n: pl.BlockSpec((SEQ, n), lambda b: (off + b, 0))
    return pl.pallas_call(
        _attn_ctx_kernel,
        out_shape=jax.ShapeDtypeStruct((N_PROMPT_TOK, ATTN_DIM), BF16),
        grid=(BATCH,),
        in_specs=[pl.BlockSpec(memory_space=pltpu.SMEM), row(ATTN_DIM), row(KV_DIM), row(KV_DIM)],
        out_specs=pl.BlockSpec((SEQ, ATTN_DIM), lambda b: (b, 0)),
        scratch_shapes=[pltpu.VMEM((N_HEAD_SLOTS, SEQ, LANES), BF16),
                        pltpu.VMEM((N_HEAD_SLOTS, SEQ, LANES), BF16)],
        compiler_params=_params(("arbitrary",)),
        name="attn_context",
    )(sink, q, k, v)


def _outproj_kernel(as_ref, ap_ref, x_ref, m_ref, g_ref, w_ref, o_ref):
    a = _load_split(as_ref, ap_ref, TM)
    y = _rms(_dot(a, w_ref[...])) * g_ref[3:4, :]
    o_ref[...] = x_ref[...] + m_ref[5:6, :] * y


def _outproj(a_s, a_p, x, mods, norm_g, w, li, j):
    return pl.pallas_call(
        _outproj_kernel,
        out_shape=jax.ShapeDtypeStruct((N_TOK, D_MODEL), F32),
        grid=(N_TOK // TM,),
        in_specs=_split_specs(TM, D_MODEL) + [
            _tok_spec(TM, D_MODEL), _mods_spec(li, TM), _norm_spec(li), _resident(w, j)],
        out_specs=_tok_spec(TM, D_MODEL),
        compiler_params=_params(("arbitrary",)),
        name="mixer_out",
    )(a_s, a_p, x, mods, norm_g, w)


CONV_COLS = 256


def _conv_kernel(x_ref, m_ref, g_ref, win_ref, cw_ref, cb_ref, wout_ref, o_ref, a_ref):
    i = pl.program_id(0)
    h = _mixer_in(x_ref, m_ref, g_ref)
    tm = h.shape[0]
    seq = jnp.where(i < N_SAMPLE_TOK // tm, DEC_SEQ, SEQ)
    pos = lax.broadcasted_iota(jnp.int32, (tm, CONV_COLS), 0) & (seq - 1)
    first = pos == 0
    last = pos == seq - 1
    cw = cw_ref[...]
    cb = cb_ref[...]
    for c in range(D_MODEL // CONV_COLS):
        cols = slice(c * CONV_COLS, (c + 1) * CONV_COLS)
        part = lambda j: _dot(h, win_ref[:, j * D_MODEL + c * CONV_COLS:j * D_MODEL + (c + 1) * CONV_COLS])
        z = part(1) * part(2)
        zl = jnp.where(first, 0.0, pltpu.roll(z, 1, 0))
        zr = jnp.where(last, 0.0, pltpu.roll(z, tm - 1, 0))
        conv = cb[:, cols] + zl * cw[0:1, cols] + z * cw[1:2, cols] + zr * cw[2:3, cols]
        a_ref[:, cols] = (part(0) * conv).astype(BF16)
    y = _rms(_dot(a_ref[...], wout_ref[...])) * g_ref[3:4, :]
    o_ref[...] = x_ref[...] + m_ref[5:6, :] * y


def _conv_layer(x, mods, norm_g, w_in, cw, cb, w_out, li, j):
    tm = CONV_TM
    return pl.pallas_call(
        _conv_kernel,
        out_shape=jax.ShapeDtypeStruct((N_TOK, D_MODEL), F32),
        grid=(N_TOK // tm,),
        in_specs=[
            _tok_spec(tm, D_MODEL), _mods_spec(li, tm), _norm_spec(li), _resident(w_in, j),
            pl.BlockSpec((None, 3, D_MODEL), lambda i: (j, 0, 0)),
            pl.BlockSpec((None, 1, D_MODEL), lambda i: (j, 0, 0)),
            _resident(w_out, j),
        ],
        out_specs=_tok_spec(tm, D_MODEL),
        scratch_shapes=[pltpu.VMEM((tm, D_MODEL), BF16)],
        compiler_params=_params(("arbitrary",)),
        name="conv_layer",
    )(x, mods, norm_g, w_in, cw, cb.reshape(-1, 1, D_MODEL), w_out)


def _block_row(x, w, off):
    c = x.shape[0]
    blk = 2 * w
    row = lambda r: jnp.broadcast_to(x[r:r + 1, :], (max(blk, SUBLANES), LANES))
    if blk >= SUBLANES:
        return jnp.concatenate([row(j * blk + off) for j in range(c // blk)], axis=0)
    if blk == 4:
        upper = lax.broadcasted_iota(jnp.int32, (SUBLANES, LANES), 0) < 4
        return jnp.concatenate(
            [jnp.where(upper, row(SUBLANES * v + off), row(SUBLANES * v + 4 + off))
             for v in range(c // SUBLANES)], axis=0)
    odd = (lax.broadcasted_iota(jnp.int32, (c, LANES), 0) & 1) != 0
    if off == 0:
        return jnp.where(odd, pltpu.roll(x, 1, 0), x)
    return jnp.where(odd, x, pltpu.roll(x, c - 1, 0))


def _split3(x):
    hi = x.astype(BF16)
    r = x - hi.astype(F32)
    mid = r.astype(BF16)
    lo = (r - mid.astype(F32)).astype(BF16)
    return jnp.concatenate([hi, mid, lo], axis=0)


def _hgrn_kernel(*refs, seq, li, has_init, emit_state):
    lb_ref, lv_ref, zq_ref, zi_ref, zf_ref, zb_ref, zg_ref, ng_ref = refs[:8]
    refs = refs[8:]
    s0_ref = None
    if has_init:
        s0_ref, refs = refs[0], refs[1:]
    o_ref, refs = refs[0], refs[1:]
    sout_ref = None
    if emit_state:
        sout_ref, refs = refs[0], refs[1:]
    acc_ref, qf_ref, qb_ref, kv_ref, dec_ref, stf_ref, stb_ref = refs
    c = GLA_CHUNK
    nc = seq // c

    def lower_bound(x):
        e = jnp.exp(x - jnp.max(x, axis=0, keepdims=True))
        pr = e / jnp.sum(e, axis=0, keepdims=True)
        return jnp.sum(pr[1:li + 1], axis=0, keepdims=True)

    lbf, lbb = lower_bound(lb_ref[0]), lower_bound(lb_ref[1])
    ri = lax.broadcasted_iota(jnp.int32, (c, c), 0)
    ci = lax.broadcasted_iota(jnp.int32, (c, c), 1)
    tril = jnp.where(ci <= ri, 1.0, 0.0).astype(BF16)
    triu = jnp.where(ci >= ri, 1.0, 0.0).astype(BF16)
    tril3 = jnp.concatenate([tril, tril, tril], axis=1)
    triu3 = jnp.concatenate([triu, triu, triu], axis=1)
    lv = lv_ref[...]
    rowi = lax.broadcasted_iota(jnp.int32, (c, LANES), 0)

    def chunk_local(n, carry):
        rows = pl.ds(pl.multiple_of(n * c, c), c)
        q = _silu(zq_ref[rows, :]) * (HGRN_DK ** -0.5)
        v = zi_ref[rows, :].astype(BF16)
        ff = lbf + (1.0 - lbf) * jax.nn.sigmoid(zf_ref[rows, :])
        fb = lbb + (1.0 - lbb) * jax.nn.sigmoid(zb_ref[rows, :])
        kf = 1.0 - ff
        kb = 1.0 - fb
        pre = _dot(tril3, _split3(jnp.log2(ff)))
        suf = _dot(triu3, _split3(jnp.log2(fb)))
        attn = jnp.where(lv == 0, _dot_nt(q.astype(BF16), (kf + kb).astype(BF16)), 0.0)
        w, level = 1, 1
        while w < c:
            d_f = pre - _block_row(pre, w, w - 1)
            d_b = suf - _block_row(suf, w, w)
            xq = jnp.minimum(d_f, d_b)
            xk = xq - (d_f + d_b)
            second = (rowi & w) != 0
            qh = (q * jnp.exp2(xq)).astype(BF16)
            kh = (jnp.where(second, kb, kf) * jnp.exp2(xk)).astype(BF16)
            attn = jnp.where(lv == level, _dot_nt(qh, kh), attn)
            w *= 2
            level += 1
        acc_ref[rows, :] = _dot(attn.astype(BF16), v)
        tot_f = pre[c - 1:c, :]
        tot_b = suf[0:1, :]
        qf_ref[rows, :] = (q * jnp.exp2(pre)).astype(BF16)
        qb_ref[rows, :] = (q * jnp.exp2(suf)).astype(BF16)
        kt = jnp.concatenate([(kf * jnp.exp2(tot_f - pre)).astype(BF16),
                              (kb * jnp.exp2(tot_b - suf)).astype(BF16)], axis=1)
        kv_ref[n] = _dot(zi_ref[rows, :].T.astype(BF16), kt)
        dec_ref[n] = jnp.exp2(jnp.concatenate([tot_f, tot_b], axis=1))
        return carry

    lax.fori_loop(0, nc, chunk_local, 0, unroll=2)

    if has_init:
        stf_ref[...] = s0_ref[0].T
        stb_ref[...] = s0_ref[1].T
    else:
        stf_ref[...] = jnp.zeros_like(stf_ref)
        stb_ref[...] = jnp.zeros_like(stb_ref)

    def carry_state(i, carry):
        n, m = i, nc - 1 - i
        rn = pl.ds(pl.multiple_of(n * c, c), c)
        rm = pl.ds(pl.multiple_of(m * c, c), c)
        stf = stf_ref[...]
        stb = stb_ref[...]
        acc_ref[rn, :] += _dot_nt(qf_ref[rn, :], stf.astype(BF16))
        acc_ref[rm, :] += _dot_nt(qb_ref[rm, :], stb.astype(BF16))
        stf_ref[...] = stf * dec_ref[n][:, :HGRN_DK] + kv_ref[n][:, :HGRN_DK]
        stb_ref[...] = stb * dec_ref[m][:, HGRN_DK:] + kv_ref[m][:, HGRN_DK:]
        return carry

    lax.fori_loop(0, nc, carry_state, 0, unroll=True)
    o_ref[...] = (_rms(acc_ref[...]) * ng_ref[...] * _silu(zg_ref[...])).astype(o_ref.dtype)
    if emit_state:
        sout_ref[0] = stf_ref[...].T
        sout_ref[1] = stb_ref[...].T


def _pair_levels():
    t = np.arange(GLA_CHUNK)
    x = t[:, None] ^ t[None, :]
    return jnp.asarray(np.where(x > 0, np.floor(np.log2(np.maximum(x, 1))) + 1, 0).astype(np.int32))


def _hgrn_core(z, lb, ng, s0, *, j, batch, seq, row_off, li, emit_state):
    has_init = s0 is not None
    blk0 = row_off // seq
    fd = HGRN_FD // LANES
    dm = D_MODEL // LANES
    c = GLA_CHUNK

    def col(c0):
        return pl.BlockSpec((seq, LANES), lambda b, h: (blk0 + b, c0 + h))

    in_specs = [
        pl.BlockSpec((2, DEPTH, LANES), lambda b, h: (0, 0, h)),
        pl.BlockSpec((c, c), lambda b, h: (0, 0)),
        col(0), col(fd), col(fd + dm), col(2 * fd + dm), col(3 * fd + dm),
        pl.BlockSpec((None, 1, HGRN_DV), lambda b, h: (j, 0, 0)),
    ]
    args = [lb, _pair_levels(), z, z, z, z, z, ng.reshape(-1, 1, HGRN_DV)]
    if has_init:
        in_specs.append(pl.BlockSpec((None, None, 2, None, HGRN_DK, HGRN_DV),
                                     lambda b, h: (b, j, 0, h, 0, 0)))
        args.append(s0)
    out_shape = [jax.ShapeDtypeStruct((batch * seq, D_MODEL), BF16)]
    out_specs = [pl.BlockSpec((seq, LANES), lambda b, h: (b, h))]
    if emit_state:
        out_shape.append(jax.ShapeDtypeStruct((batch, 2, HGRN_HEADS, HGRN_DK, HGRN_DV), F32))
        out_specs.append(pl.BlockSpec((None, 2, None, HGRN_DK, HGRN_DV), lambda b, h: (b, 0, h, 0, 0)))
    return pl.pallas_call(
        functools.partial(_hgrn_kernel, seq=seq, li=li, has_init=has_init, emit_state=emit_state),
        out_shape=tuple(out_shape),
        grid=(batch, HGRN_HEADS),
        in_specs=in_specs,
        out_specs=tuple(out_specs),
        scratch_shapes=[
            pltpu.VMEM((seq, HGRN_DV), F32),
            pltpu.VMEM((seq, HGRN_DK), BF16),
            pltpu.VMEM((seq, HGRN_DK), BF16),
            pltpu.VMEM((seq // c, HGRN_DV, 2 * HGRN_DK), F32),
            pltpu.VMEM((seq // c, 1, 2 * HGRN_DK), F32),
            pltpu.VMEM((HGRN_DV, HGRN_DK), F32),
            pltpu.VMEM((HGRN_DV, HGRN_DK), F32),
        ],
        compiler_params=_params(("arbitrary", "arbitrary")),
        name=f"hgrn_core_{seq}",
    )(*args)


def kernel(x_prompt, x_sample, cache_k, cache_v, state_hgrn, c, c_ctx, w_ada, b_ada, norm_g, w_ffn_in, w_ffn_out, w_attn_qkv, w_attn_o, attn_sink, w_conv_in, conv_w, conv_b, w_conv_out, w_hgrn_in, hgrn_lb, hgrn_norm_g, w_hgrn_out):
    cond = jnp.concatenate([c, c_ctx[None, :],
                            jnp.zeros((N_COND - DEC_BATCH - 1, D_MODEL), F32)], axis=0)
    mods = _ada_table(cond, w_ada, b_ada)
    cos, sin = _rope_tables()
    ck = cache_k.reshape(DEC_BATCH, -1, PAST_LEN, KV_DIM)
    cv = cache_v.reshape(DEC_BATCH, -1, PAST_LEN, KV_DIM)
    bf = lambda w: w.astype(BF16)
    w_ffn_in, w_ffn_out = bf(w_ffn_in), bf(w_ffn_out)
    w_attn_qkv, w_attn_o = bf(w_attn_qkv), bf(w_attn_o)
    w_conv_in, w_conv_out = bf(w_conv_in), bf(w_conv_out)
    w_hgrn_in, w_hgrn_out = bf(w_hgrn_in), bf(w_hgrn_out)
    xs = (x_sample.reshape(N_SAMPLE_TOK, D_MODEL), x_prompt.reshape(N_PROMPT_TOK, D_MODEL))
    new_k, new_v, new_s = [], [], []
    for li in range(DEPTH):
        kind, j = li % N_MIXERS, li // N_MIXERS
        x = _ffn(xs, mods, norm_g, w_ffn_in, w_ffn_out, li, 0)
        if kind == 0:
            q, k, v, kf, vf = _qkv(x, mods, norm_g, w_attn_qkv, li, j, cos, sin)
            o_s = _attn_latent(attn_sink[j], q, k, v, ck, cv, j)
            o_p = _attn_context(attn_sink[j], q, k, v)
            x = _outproj(o_s, o_p, x, mods, norm_g, w_attn_o, li, j)
            new_k.append(kf.reshape(BATCH, SEQ, ATTN_KV_HEADS, HEAD_DIM))
            new_v.append(vf.reshape(BATCH, SEQ, ATTN_KV_HEADS, HEAD_DIM))
        elif kind == 1:
            x = _conv_layer(x, mods, norm_g, w_conv_in, conv_w, conv_b, w_conv_out, li, j)
        else:
            z = _proj(x, mods, norm_g, w_hgrn_in, li, j, F32)
            o_s = _hgrn_core(z, hgrn_lb, hgrn_norm_g, state_hgrn, j=j, batch=DEC_BATCH,
                             seq=DEC_SEQ, row_off=0, li=li, emit_state=False)[0]
            o_p, st = _hgrn_core(z, hgrn_lb, hgrn_norm_g, None, j=j, batch=BATCH, seq=SEQ,
                                 row_off=N_SAMPLE_TOK, li=li, emit_state=True)
            x = _outproj(o_s, o_p, x, mods, norm_g, w_hgrn_out, li, j)
            new_s.append(st)
        out = _ffn((x,), mods, norm_g, w_ffn_in, w_ffn_out, li, 1, split_out=(li == DEPTH - 1))
        xs = tuple(out) if isinstance(out, (tuple, list)) else (out,)
    y_sample = xs[0].reshape(DEC_BATCH, DEC_SEQ, D_MODEL)
    y_prompt = xs[1].reshape(BATCH, SEQ, D_MODEL)
    return (y_prompt, y_sample, jnp.stack(new_k, axis=1), jnp.stack(new_v, axis=1),
            jnp.stack(new_s, axis=1))
```

```python
import functools
import math

import jax
import jax.numpy as jnp
import numpy as np
from jax import lax
from jax.experimental import pallas as pl
from jax.experimental.pallas import tpu as pltpu

D_MODEL = 1024
BATCH = 16
SEQ = 256
DEPTH = 4
DEC_BATCH = 8
DEC_SEQ = 1024
PAST_LEN = 256
GRID_W = 64
N_MIXERS = 3
N_MOD = 9
N_NORMS = 6
D_FF = 2816
FFN_RES = 0.5
ATTN_HEADS = 16
ATTN_KV_HEADS = 4
HEAD_DIM = 64
ATTN_DIM = ATTN_HEADS * HEAD_DIM
KV_DIM = ATTN_KV_HEADS * HEAD_DIM
BLOCK = 128
WINDOW = 128
ATTN_SCALE = HEAD_DIM ** -0.5
LOG2E = math.log2(math.e)
ROPE_BASE = 10000.0
ROPE_PAIRS_PER_AXIS = HEAD_DIM // 4
HGRN_HEADS = 8
HGRN_DK = 128
HGRN_DV = D_MODEL // HGRN_HEADS
HGRN_FD = HGRN_HEADS * HGRN_DK
EPS = 1e-6

N_SAMPLE_TOK = DEC_BATCH * DEC_SEQ
N_PROMPT_TOK = BATCH * SEQ
N_TOK = N_SAMPLE_TOK + N_PROMPT_TOK
N_COND = 16
CTX_COND = DEC_BATCH

LANES = 128
SUBLANES = 8
TM = 512
CONV_TM = 1024
FFN_SPLITS = (0, 1024, 2048, D_FF)
GLA_CHUNK = 128
NEG_BIG = -1e30
VMEM_LIMIT = 56 * 1024 * 1024

F32 = jnp.float32
BF16 = jnp.bfloat16


def _rms(x):
    return x * lax.rsqrt(jnp.mean(x * x, axis=-1, keepdims=True) + EPS)


def _silu(x):
    return x * jax.nn.sigmoid(x)


def _dot(a, b):
    return jnp.dot(a, b, preferred_element_type=F32)


def _dot_nt(a, b):
    return lax.dot_general(a, b, (((1,), (1,)), ((), ())), preferred_element_type=F32)


def _params(sem):
    return pltpu.CompilerParams(dimension_semantics=sem, vmem_limit_bytes=VMEM_LIMIT)


def _resident(arr, *lead):
    shape = (None,) * len(lead) + arr.shape[len(lead):]
    idx = tuple(lead) + (0,) * (arr.ndim - len(lead))
    return pl.BlockSpec(shape, lambda *_: idx, pipeline_mode=pl.Buffered(1))


def _mods_spec(li, tm):
    per = DEC_SEQ // tm
    return pl.BlockSpec((None, None, N_MOD, D_MODEL),
                        lambda i: (li, jnp.minimum(i // per, CTX_COND), 0, 0))


def _norm_spec(li):
    return pl.BlockSpec((None, N_NORMS, D_MODEL), lambda i: (li, 0, 0))


def _tok_spec(tm, n):
    return pl.BlockSpec((tm, n), lambda i: (i, 0))


def _split_specs(tm, n):
    ns = N_SAMPLE_TOK // tm
    return [pl.BlockSpec((tm, n), lambda i: (jnp.minimum(i, ns - 1), 0)),
            pl.BlockSpec((tm, n), lambda i: (jnp.maximum(i - ns, 0), 0))]


def _load_split(s_ref, p_ref, tm):
    return jnp.where(pl.program_id(0) < N_SAMPLE_TOK // tm, s_ref[...], p_ref[...])


def _store_split(s_ref, p_ref, val, tm):
    ns = N_SAMPLE_TOK // tm
    i = pl.program_id(0)

    @pl.when(i < ns)
    def _():
        s_ref[...] = val

    @pl.when(i >= ns)
    def _():
        p_ref[...] = val


def _ada_kernel(c_ref, w_ref, b_ref, o_ref):
    s = _silu(c_ref[...]).astype(BF16)
    o_ref[...] = _dot(s, w_ref[...].astype(BF16)) + b_ref[...]


def _ada_table(cond, w_ada, b_ada):
    tn = 1024
    n = N_MOD * D_MODEL
    out = pl.pallas_call(
        _ada_kernel,
        out_shape=jax.ShapeDtypeStruct((DEPTH, N_COND, n), F32),
        grid=(DEPTH, n // tn),
        in_specs=[
            pl.BlockSpec((N_COND, D_MODEL), lambda l, j: (0, 0)),
            pl.BlockSpec((None, D_MODEL, tn), lambda l, j: (l, 0, j)),
            pl.BlockSpec((None, 1, tn), lambda l, j: (l, 0, j)),
        ],
        out_specs=pl.BlockSpec((None, N_COND, tn), lambda l, j: (l, 0, j)),
        compiler_params=_params(("arbitrary", "arbitrary")),
        name="ada_table",
    )(cond, w_ada, b_ada.reshape(DEPTH, 1, n))
    return out.reshape(DEPTH, N_COND, N_MOD, D_MODEL)


def _ffn_kernel(*refs, k0, g0, split_in, split_out, has_mixer, tm):
    n_in = 2 if split_in else 1
    x_refs, refs = refs[:n_in], refs[n_in:]
    mix_refs = ()
    if has_mixer:
        mix_refs, refs = refs[:3], refs[3:]
    m_ref, g_ref, win_ref, wout_ref = refs[:4]
    o_refs = refs[4:]
    x = _load_split(*x_refs, tm) if split_in else x_refs[0][...]
    m = m_ref[...]
    g = g_ref[...]
    if has_mixer:
        as_ref, ap_ref, wo_ref = mix_refs
        mixed = _dot(_load_split(as_ref, ap_ref, tm), wo_ref[...])
        x = x + m[5:6] * (_rms(mixed) * g[3:4])
    h = _rms(x) * g[g0:g0 + 1]
    h = (h * (1.0 + m[k0 + 1:k0 + 2]) + m[k0:k0 + 1]).astype(BF16)
    acc = None
    for lo, hi in zip(FFN_SPLITS[:-1], FFN_SPLITS[1:]):
        a = _dot(h, win_ref[:, lo:hi])
        b = _dot(h, win_ref[:, D_FF + lo:D_FF + hi])
        u = (_silu(a) * b).astype(BF16)
        part = _dot(u, wout_ref[lo:hi, :])
        acc = part if acc is None else acc + part
    y = _rms(acc) * g[g0 + 1:g0 + 2]
    res = x + FFN_RES * m[k0 + 2:k0 + 3] * y
    if split_out:
        _store_split(*o_refs, res, tm)
    else:
        o_refs[0][...] = res


def _ffn(xs, mods, norm_g, w_in, w_out, li, which, split_out=False, mixer=None):
    k0, g0 = (0, 0) if which == 0 else (6, 4)
    tm = TM
    split_in = len(xs) == 2
    x_specs = _split_specs(tm, D_MODEL) if split_in else [_tok_spec(tm, D_MODEL)]
    mix_specs, mix_args = [], []
    if mixer is not None:
        a_s, a_p, w_o, j = mixer
        mix_specs = _split_specs(tm, D_MODEL) + [_resident(w_o, j)]
        mix_args = [a_s, a_p, w_o]
    if split_out:
        out_shape = (jax.ShapeDtypeStruct((N_SAMPLE_TOK, D_MODEL), F32),
                     jax.ShapeDtypeStruct((N_PROMPT_TOK, D_MODEL), F32))
        out_specs = tuple(_split_specs(tm, D_MODEL))
    else:
        out_shape = jax.ShapeDtypeStruct((N_TOK, D_MODEL), F32)
        out_specs = _tok_spec(tm, D_MODEL)
    return pl.pallas_call(
        functools.partial(_ffn_kernel, k0=k0, g0=g0, split_in=split_in, split_out=split_out,
                          has_mixer=mixer is not None, tm=tm),
        out_shape=out_shape,
        grid=(N_TOK // tm,),
        in_specs=x_specs + mix_specs + [
            _mods_spec(li, tm), _norm_spec(li),
            _resident(w_in, li, which), _resident(w_out, li, which),
        ],
        out_specs=out_specs,
        compiler_params=_params(("arbitrary",)),
        name=f"ffn{which}",
    )(*xs, *mix_args, mods, norm_g, w_in, w_out)


def _mixer_in(x_ref, m_ref, g_ref):
    m = m_ref[...]
    h = _rms(x_ref[...]) * g_ref[2:3, :]
    return (h * (1.0 + m[4:5]) + m[3:4]).astype(BF16)


def _proj_kernel(x_ref, m_ref, g_ref, w_ref, o_ref):
    o_ref[...] = _dot(_mixer_in(x_ref, m_ref, g_ref), w_ref[...]).astype(o_ref.dtype)


def _proj(x, mods, norm_g, w, li, j, out_dtype):
    n = w.shape[-1]
    return pl.pallas_call(
        _proj_kernel,
        out_shape=jax.ShapeDtypeStruct((N_TOK, n), out_dtype),
        grid=(N_TOK // TM,),
        in_specs=[_tok_spec(TM, D_MODEL), _mods_spec(li, TM), _norm_spec(li), _resident(w, j)],
        out_specs=_tok_spec(TM, n),
        compiler_params=_params(("arbitrary",)),
        name="mixer_proj",
    )(x, mods, norm_g, w)


def _rope_cols(t, cos, sin):
    lane = lax.broadcasted_iota(jnp.int32, (t.shape[0], LANES), 1)
    low = (lane % HEAD_DIM) < (HEAD_DIM // 2)
    outs = []
    for c in range(t.shape[1] // LANES):
        tc = t[:, c * LANES:(c + 1) * LANES]
        partner = jnp.where(low, pltpu.roll(tc, LANES - HEAD_DIM // 2, 1),
                            pltpu.roll(tc, HEAD_DIM // 2, 1))
        outs.append(tc * cos + partner * sin)
    return jnp.concatenate(outs, axis=1)


def _qkv_kernel(x_ref, m_ref, g_ref, w_ref, cos_ref, sin_ref,
                q_ref, k_ref, v_ref, kf_ref, vf_ref):
    h = _mixer_in(x_ref, m_ref, g_ref)
    cos = cos_ref[...]
    sin = sin_ref[...]
    for c0 in range(0, ATTN_DIM, KV_DIM):
        q = _dot(h, w_ref[:, c0:c0 + KV_DIM])
        q_ref[:, c0:c0 + KV_DIM] = (_rope_cols(q, cos, sin) * (ATTN_SCALE * LOG2E)).astype(BF16)
    k = _dot(h, w_ref[:, ATTN_DIM:ATTN_DIM + KV_DIM])
    v = _dot(h, w_ref[:, ATTN_DIM + KV_DIM:])

    @pl.when(pl.program_id(0) >= N_SAMPLE_TOK // TM)
    def _():
        kf_ref[...] = k
        vf_ref[...] = v

    k_ref[...] = _rope_cols(k, cos, sin).astype(BF16)
    v_ref[...] = v.astype(BF16)


def _rope_tables():
    pos = np.arange(DEC_SEQ)
    row = (pos // GRID_W).astype(np.float32)
    col = (pos % GRID_W).astype(np.float32)
    inv = (ROPE_BASE ** (-np.arange(ROPE_PAIRS_PER_AXIS, dtype=np.float32) / ROPE_PAIRS_PER_AXIS)).astype(np.float32)
    ang = jnp.asarray(np.concatenate([row[:, None] * inv, col[:, None] * inv], axis=-1).astype(np.float32))
    cos, sin = jnp.cos(ang), jnp.sin(ang)
    cos = jnp.concatenate([cos, cos, cos, cos], axis=-1)
    sin = jnp.concatenate([-sin, sin, -sin, sin], axis=-1)
    cos = jnp.concatenate([cos, jnp.ones((TM, LANES), F32)], axis=0)
    sin = jnp.concatenate([sin, jnp.zeros((TM, LANES), F32)], axis=0)
    return cos, sin


def _qkv(x, mods, norm_g, w, li, j, cos, sin):
    per = DEC_SEQ // TM
    ns = N_SAMPLE_TOK // TM
    tab = pl.BlockSpec((TM, LANES), lambda i: (jnp.where(i < ns, i % per, per), 0))
    cache = pl.BlockSpec((TM, KV_DIM), lambda i: (jnp.maximum(i - ns, 0), 0))
    return pl.pallas_call(
        _qkv_kernel,
        out_shape=(
            jax.ShapeDtypeStruct((N_TOK, ATTN_DIM), BF16),
            jax.ShapeDtypeStruct((N_TOK, KV_DIM), BF16),
            jax.ShapeDtypeStruct((N_TOK, KV_DIM), BF16),
            jax.ShapeDtypeStruct((N_PROMPT_TOK, KV_DIM), F32),
            jax.ShapeDtypeStruct((N_PROMPT_TOK, KV_DIM), F32),
        ),
        grid=(N_TOK // TM,),
        in_specs=[_tok_spec(TM, D_MODEL), _mods_spec(li, TM), _norm_spec(li), _resident(w, j), tab, tab],
        out_specs=(_tok_spec(TM, ATTN_DIM), _tok_spec(TM, KV_DIM), _tok_spec(TM, KV_DIM), cache, cache),
        compiler_params=_params(("arbitrary",)),
        name="attn_qkv",
    )(x, mods, norm_g, w, cos, sin)


N_HEAD_SLOTS = 2 * ATTN_KV_HEADS


def _fill_head_slots(x, k_dst, v_dst, row0, is_value):
    dst = v_dst if is_value else k_dst
    pad = 1.0 if is_value else 0.0
    rows = x.shape[0]
    low = lax.broadcasted_iota(jnp.int32, (rows, LANES), 1) < HEAD_DIM
    for g in range(ATTN_KV_HEADS):
        cg, pg = g // 2, g % 2
        xb = x[:, cg * LANES:(cg + 1) * LANES]
        xr = pltpu.roll(xb, HEAD_DIM, 1)
        x_lo, x_hi = (xb, xr) if pg == 0 else (xr, xb)
        dst[2 * g, pl.ds(row0, rows), :] = jnp.where(low, x_lo, pad).astype(BF16)
        dst[2 * g + 1, pl.ds(row0, rows), :] = jnp.where(low, pad, x_hi).astype(BF16)


def _attend_rows(q, key_blocks, sink_ref, o_ref, orow0, s_scr, p_scr):
    rows = q.shape[0]
    rid = lax.broadcasted_iota(jnp.int32, (2 * rows, 1), 0)
    low = lax.broadcasted_iota(jnp.int32, (2 * rows, LANES), 1) < HEAD_DIM
    sinks, maxes = [], []
    for slot in range(N_HEAD_SLOTS):
        g, side = slot // 2, slot % 2
        c0 = 2 * g * LANES
        qs = jnp.concatenate([q[:, c0:c0 + LANES], q[:, c0 + LANES:c0 + 2 * LANES]], axis=0)
        sk = jnp.where(rid < rows, sink_ref[4 * g + side], sink_ref[4 * g + 2 + side]) * LOG2E
        mx = sk
        col = 0
        for k_of, _, bias, n_keys in key_blocks:
            s = _dot_nt(qs, k_of(slot))
            if bias is not None:
                s = s + bias
            s_scr[slot, :, col:col + n_keys] = s
            mx = jnp.maximum(mx, jnp.max(s, axis=-1, keepdims=True))
            col += n_keys
        sinks.append(sk)
        maxes.append(mx)
    for slot in range(N_HEAD_SLOTS):
        p_scr[slot] = jnp.exp2(s_scr[slot] - maxes[slot]).astype(BF16)
    for g in range(ATTN_KV_HEADS):
        res = []
        for side in range(2):
            slot = 2 * g + side
            part = None
            col = 0
            for _, v_of, _, n_keys in key_blocks:
                pv = _dot(p_scr[slot, :, col:col + n_keys], v_of(slot))
                part = pv if part is None else part + pv
                col += n_keys
            den = pltpu.roll(part, HEAD_DIM, 1) + jnp.exp2(sinks[slot] - maxes[slot])
            res.append(part / den)
        out = jnp.where(low, res[0], res[1]).astype(o_ref.dtype)
        c0 = 2 * g * LANES
        o_ref[pl.ds(orow0, rows), c0:c0 + LANES] = out[:rows]
        o_ref[pl.ds(orow0, rows), c0 + LANES:c0 + 2 * LANES] = out[rows:]


def _attn_lat_kernel(sink_ref, q_ref, k_ref, v_ref, ck_ref, cv_ref, o_ref,
                     kl_ref, vl_ref, kc_ref, vc_ref, s_scr, p_scr):
    nb = DEC_SEQ // BLOCK
    zeros = jnp.zeros((BLOCK, LANES), BF16)
    for slot in range(N_HEAD_SLOTS):
        for ref in (kl_ref, vl_ref):
            ref[slot, pl.ds(0, BLOCK), :] = zeros
            ref[slot, pl.ds(DEC_SEQ + BLOCK, BLOCK), :] = zeros
    _fill_head_slots(k_ref[...].astype(F32), kl_ref, vl_ref, BLOCK, False)
    _fill_head_slots(v_ref[...].astype(F32), kl_ref, vl_ref, BLOCK, True)
    _fill_head_slots(ck_ref[...], kc_ref, vc_ref, 0, False)
    _fill_head_slots(cv_ref[...], kc_ref, vc_ref, 0, True)
    n_band = 3 * BLOCK
    qi = lax.broadcasted_iota(jnp.int32, (2 * BLOCK, n_band), 0) % BLOCK
    u = lax.broadcasted_iota(jnp.int32, (2 * BLOCK, n_band), 1)
    in_window = jnp.abs(qi + BLOCK - u) <= WINDOW

    def body(n, carry):
        r0 = pl.multiple_of(n * BLOCK, BLOCK)
        kpos = (n - 1) * BLOCK + u
        ok = in_window & (kpos >= 0) & (kpos < nb * BLOCK)
        bias = jnp.where(ok, 0.0, NEG_BIG)
        band = (lambda s: kl_ref[s, pl.ds(r0, n_band), :], lambda s: vl_ref[s, pl.ds(r0, n_band), :],
                bias, n_band)
        ctx = (lambda s: kc_ref[s], lambda s: vc_ref[s], None, PAST_LEN)
        _attend_rows(q_ref[pl.ds(r0, BLOCK), :], [band, ctx], sink_ref, o_ref, r0, s_scr, p_scr)
        return carry

    lax.fori_loop(0, nb, body, 0)


def _attn_latent(sink, q, k, v, ck, cv, j):
    seq = lambda n: pl.BlockSpec((DEC_SEQ, n), lambda b: (b, 0))
    ctx = pl.BlockSpec((None, None, PAST_LEN, KV_DIM), lambda b: (b, j, 0, 0))
    return pl.pallas_call(
        _attn_lat_kernel,
        out_shape=jax.ShapeDtypeStruct((N_SAMPLE_TOK, ATTN_DIM), BF16),
        grid=(DEC_BATCH,),
        in_specs=[pl.BlockSpec(memory_space=pltpu.SMEM), seq(ATTN_DIM), seq(KV_DIM), seq(KV_DIM), ctx, ctx],
        out_specs=seq(ATTN_DIM),
        scratch_shapes=[
            pltpu.VMEM((N_HEAD_SLOTS, DEC_SEQ + 2 * BLOCK, LANES), BF16),
            pltpu.VMEM((N_HEAD_SLOTS, DEC_SEQ + 2 * BLOCK, LANES), BF16),
            pltpu.VMEM((N_HEAD_SLOTS, PAST_LEN, LANES), BF16),
            pltpu.VMEM((N_HEAD_SLOTS, PAST_LEN, LANES), BF16),
            pltpu.VMEM((N_HEAD_SLOTS, 2 * BLOCK, 3 * BLOCK + PAST_LEN), F32),
            pltpu.VMEM((N_HEAD_SLOTS, 2 * BLOCK, 3 * BLOCK + PAST_LEN), BF16),
        ],
        compiler_params=_params(("arbitrary",)),
        name="attn_latent",
    )(sink, q, k, v, ck, cv)


def _attn_ctx_kernel(sink_ref, q_ref, k_ref, v_ref, o_ref, ks_ref, vs_ref, s_scr, p_scr):
    _fill_head_slots(k_ref[...].astype(F32), ks_ref, vs_ref, 0, False)
    _fill_head_slots(v_ref[...].astype(F32), ks_ref, vs_ref, 0, True)
    blk = (lambda s: ks_ref[s], lambda s: vs_ref[s], None, SEQ)
    _attend_rows(q_ref[...], [blk], sink_ref, o_ref, 0, s_scr, p_scr)


def _attn_context(sink, q, k, v):
    off = N_SAMPLE_TOK // SEQ
    row = lambda n: pl.BlockSpec((SEQ, n), lambda b: (off + b, 0))
    return pl.pallas_call(
        _attn_ctx_kernel,
        out_shape=jax.ShapeDtypeStruct((N_PROMPT_TOK, ATTN_DIM), BF16),
        grid=(BATCH,),
        in_specs=[pl.BlockSpec(memory_space=pltpu.SMEM), row(ATTN_DIM), row(KV_DIM), row(KV_DIM)],
        out_specs=pl.BlockSpec((SEQ, ATTN_DIM), lambda b: (b, 0)),
        scratch_shapes=[pltpu.VMEM((N_HEAD_SLOTS, SEQ, LANES), BF16),
                        pltpu.VMEM((N_HEAD_SLOTS, SEQ, LANES), BF16),
                        pltpu.VMEM((N_HEAD_SLOTS, 2 * SEQ, SEQ), F32),
                        pltpu.VMEM((N_HEAD_SLOTS, 2 * SEQ, SEQ), BF16)],
        compiler_params=_params(("arbitrary",)),
        name="attn_context",
    )(sink, q, k, v)


CONV_COLS = 256


def _conv_kernel(x_ref, m_ref, g_ref, win_ref, cw_ref, cb_ref, wout_ref, o_ref, a_ref):
    i = pl.program_id(0)
    h = _mixer_in(x_ref, m_ref, g_ref)
    tm = h.shape[0]
    seq = jnp.where(i < N_SAMPLE_TOK // tm, DEC_SEQ, SEQ)
    pos = lax.broadcasted_iota(jnp.int32, (tm, CONV_COLS), 0) & (seq - 1)
    first = pos == 0
    last = pos == seq - 1
    cw = cw_ref[...]
    cb = cb_ref[...]
    for c in range(D_MODEL // CONV_COLS):
        cols = slice(c * CONV_COLS, (c + 1) * CONV_COLS)
        part = lambda j: _dot(h, win_ref[:, j * D_MODEL + c * CONV_COLS:j * D_MODEL + (c + 1) * CONV_COLS])
        z = part(1) * part(2)
        zl = jnp.where(first, 0.0, pltpu.roll(z, 1, 0))
        zr = jnp.where(last, 0.0, pltpu.roll(z, tm - 1, 0))
        conv = cb[:, cols] + zl * cw[0:1, cols] + z * cw[1:2, cols] + zr * cw[2:3, cols]
        a_ref[:, cols] = (part(0) * conv).astype(BF16)
    y = _rms(_dot(a_ref[...], wout_ref[...])) * g_ref[3:4, :]
    o_ref[...] = x_ref[...] + m_ref[5:6, :] * y


def _conv_layer(x, mods, norm_g, w_in, cw, cb, w_out, li, j):
    tm = CONV_TM
    return pl.pallas_call(
        _conv_kernel,
        out_shape=jax.ShapeDtypeStruct((N_TOK, D_MODEL), F32),
        grid=(N_TOK // tm,),
        in_specs=[
            _tok_spec(tm, D_MODEL), _mods_spec(li, tm), _norm_spec(li), _resident(w_in, j),
            pl.BlockSpec((None, 3, D_MODEL), lambda i: (j, 0, 0)),
            pl.BlockSpec((None, 1, D_MODEL), lambda i: (j, 0, 0)),
            _resident(w_out, j),
        ],
        out_specs=_tok_spec(tm, D_MODEL),
        scratch_shapes=[pltpu.VMEM((tm, D_MODEL), BF16)],
        compiler_params=_params(("arbitrary",)),
        name="conv_layer",
    )(x, mods, norm_g, w_in, cw, cb.reshape(-1, 1, D_MODEL), w_out)


def _block_row(x, w, off):
    c = x.shape[0]
    blk = 2 * w
    row = lambda r: jnp.broadcast_to(x[r:r + 1, :], (max(blk, SUBLANES), LANES))
    if blk >= SUBLANES:
        return jnp.concatenate([row(j * blk + off) for j in range(c // blk)], axis=0)
    if blk == 4:
        upper = lax.broadcasted_iota(jnp.int32, (SUBLANES, LANES), 0) < 4
        return jnp.concatenate(
            [jnp.where(upper, row(SUBLANES * v + off), row(SUBLANES * v + 4 + off))
             for v in range(c // SUBLANES)], axis=0)
    odd = (lax.broadcasted_iota(jnp.int32, (c, LANES), 0) & 1) != 0
    if off == 0:
        return jnp.where(odd, pltpu.roll(x, 1, 0), x)
    return jnp.where(odd, x, pltpu.roll(x, c - 1, 0))


def _split3(x):
    hi = x.astype(BF16)
    r = x - hi.astype(F32)
    mid = r.astype(BF16)
    lo = (r - mid.astype(F32)).astype(BF16)
    return jnp.concatenate([hi, mid, lo], axis=0)


def _hgrn_kernel(*refs, seq, li, has_init, emit_state):
    lb_ref, lv_ref, zq_ref, zi_ref, zf_ref, zb_ref, zg_ref, ng_ref = refs[:8]
    refs = refs[8:]
    s0_ref = None
    if has_init:
        s0_ref, refs = refs[0], refs[1:]
    o_ref, refs = refs[0], refs[1:]
    sout_ref = None
    if emit_state:
        sout_ref, refs = refs[0], refs[1:]
    acc_ref, qf_ref, qb_ref, kv_ref, dec_ref, stf_ref, stb_ref = refs
    c = GLA_CHUNK
    nc = seq // c

    def lower_bound(x):
        e = jnp.exp(x - jnp.max(x, axis=0, keepdims=True))
        pr = e / jnp.sum(e, axis=0, keepdims=True)
        return jnp.sum(pr[1:li + 1], axis=0, keepdims=True)

    lbf, lbb = lower_bound(lb_ref[0]), lower_bound(lb_ref[1])
    ri = lax.broadcasted_iota(jnp.int32, (c, c), 0)
    ci = lax.broadcasted_iota(jnp.int32, (c, c), 1)
    tril = jnp.where(ci <= ri, 1.0, 0.0).astype(BF16)
    triu = jnp.where(ci >= ri, 1.0, 0.0).astype(BF16)
    tril3 = jnp.concatenate([tril, tril, tril], axis=1)
    triu3 = jnp.concatenate([triu, triu, triu], axis=1)
    lv = lv_ref[...]
    rowi = lax.broadcasted_iota(jnp.int32, (c, LANES), 0)

    def chunk_local(n, carry):
        rows = pl.ds(pl.multiple_of(n * c, c), c)
        q = _silu(zq_ref[rows, :]) * (HGRN_DK ** -0.5)
        v = zi_ref[rows, :].astype(BF16)
        ff = lbf + (1.0 - lbf) * jax.nn.sigmoid(zf_ref[rows, :])
        fb = lbb + (1.0 - lbb) * jax.nn.sigmoid(zb_ref[rows, :])
        kf = 1.0 - ff
        kb = 1.0 - fb
        pre = _dot(tril3, _split3(jnp.log2(ff)))
        suf = _dot(triu3, _split3(jnp.log2(fb)))
        attn = jnp.where(lv == 0, _dot_nt(q.astype(BF16), (kf + kb).astype(BF16)), 0.0)
        w, level = 1, 1
        while w < c:
            d_f = pre - _block_row(pre, w, w - 1)
            d_b = suf - _block_row(suf, w, w)
            xq = jnp.minimum(d_f, d_b)
            xk = xq - (d_f + d_b)
            second = (rowi & w) != 0
            qh = (q * jnp.exp2(xq)).astype(BF16)
            kh = (jnp.where(second, kb, kf) * jnp.exp2(xk)).astype(BF16)
            attn = jnp.where(lv == level, _dot_nt(qh, kh), attn)
            w *= 2
            level += 1
        acc_ref[rows, :] = _dot(attn.astype(BF16), v)
        tot_f = pre[c - 1:c, :]
        tot_b = suf[0:1, :]
        qf_ref[rows, :] = (q * jnp.exp2(pre)).astype(BF16)
        qb_ref[rows, :] = (q * jnp.exp2(suf)).astype(BF16)
        kt = jnp.concatenate([(kf * jnp.exp2(tot_f - pre)).astype(BF16),
                              (kb * jnp.exp2(tot_b - suf)).astype(BF16)], axis=1)
        kv_ref[n] = _dot(zi_ref[rows, :].T.astype(BF16), kt)
        dec_ref[n] = jnp.exp2(jnp.concatenate([tot_f, tot_b], axis=1))
        return carry

    lax.fori_loop(0, nc, chunk_local, 0, unroll=2)

    if has_init:
        stf_ref[...] = s0_ref[0].T
        stb_ref[...] = s0_ref[1].T
    else:
        stf_ref[...] = jnp.zeros_like(stf_ref)
        stb_ref[...] = jnp.zeros_like(stb_ref)

    def carry_state(i, carry):
        n, m = i, nc - 1 - i
        rn = pl.ds(pl.multiple_of(n * c, c), c)
        rm = pl.ds(pl.multiple_of(m * c, c), c)
        stf = stf_ref[...]
        stb = stb_ref[...]
        acc_ref[rn, :] += _dot_nt(qf_ref[rn, :], stf.astype(BF16))
        acc_ref[rm, :] += _dot_nt(qb_ref[rm, :], stb.astype(BF16))
        stf_ref[...] = stf * dec_ref[n][:, :HGRN_DK] + kv_ref[n][:, :HGRN_DK]
        stb_ref[...] = stb * dec_ref[m][:, HGRN_DK:] + kv_ref[m][:, HGRN_DK:]
        return carry

    lax.fori_loop(0, nc, carry_state, 0, unroll=True)
    o_ref[...] = (_rms(acc_ref[...]) * ng_ref[...] * _silu(zg_ref[...])).astype(o_ref.dtype)
    if emit_state:
        sout_ref[0] = stf_ref[...].T
        sout_ref[1] = stb_ref[...].T


def _pair_levels():
    t = np.arange(GLA_CHUNK)
    x = t[:, None] ^ t[None, :]
    return jnp.asarray(np.where(x > 0, np.floor(np.log2(np.maximum(x, 1))) + 1, 0).astype(np.int32))


def _hgrn_core(z, lb, ng, s0, *, j, batch, seq, row_off, li, emit_state):
    has_init = s0 is not None
    blk0 = row_off // seq
    fd = HGRN_FD // LANES
    dm = D_MODEL // LANES
    c = GLA_CHUNK

    def col(c0):
        return pl.BlockSpec((seq, LANES), lambda b, h: (blk0 + b, c0 + h))

    in_specs = [
        pl.BlockSpec((2, DEPTH, LANES), lambda b, h: (0, 0, h)),
        pl.BlockSpec((c, c), lambda b, h: (0, 0)),
        col(0), col(fd), col(fd + dm), col(2 * fd + dm), col(3 * fd + dm),
        pl.BlockSpec((None, 1, HGRN_DV), lambda b, h: (j, 0, 0)),
    ]
    args = [lb, _pair_levels(), z, z, z, z, z, ng.reshape(-1, 1, HGRN_DV)]
    if has_init:
        in_specs.append(pl.BlockSpec((None, None, 2, None, HGRN_DK, HGRN_DV),
                                     lambda b, h: (b, j, 0, h, 0, 0)))
        args.append(s0)
    out_shape = [jax.ShapeDtypeStruct((batch * seq, D_MODEL), BF16)]
    out_specs = [pl.BlockSpec((seq, LANES), lambda b, h: (b, h))]
    if emit_state:
        out_shape.append(jax.ShapeDtypeStruct((batch, 2, HGRN_HEADS, HGRN_DK, HGRN_DV), F32))
        out_specs.append(pl.BlockSpec((None, 2, None, HGRN_DK, HGRN_DV), lambda b, h: (b, 0, h, 0, 0)))
    return pl.pallas_call(
        functools.partial(_hgrn_kernel, seq=seq, li=li, has_init=has_init, emit_state=emit_state),
        out_shape=tuple(out_shape),
        grid=(batch, HGRN_HEADS),
        in_specs=in_specs,
        out_specs=tuple(out_specs),
        scratch_shapes=[
            pltpu.VMEM((seq, HGRN_DV), F32),
            pltpu.VMEM((seq, HGRN_DK), BF16),
            pltpu.VMEM((seq, HGRN_DK), BF16),
            pltpu.VMEM((seq // c, HGRN_DV, 2 * HGRN_DK), F32),
            pltpu.VMEM((seq // c, 1, 2 * HGRN_DK), F32),
            pltpu.VMEM((HGRN_DV, HGRN_DK), F32),
            pltpu.VMEM((HGRN_DV, HGRN_DK), F32),
        ],
        compiler_params=_params(("arbitrary", "arbitrary")),
        name=f"hgrn_core_{seq}",
    )(*args)


def kernel(x_prompt, x_sample, cache_k, cache_v, state_hgrn, c, c_ctx, w_ada, b_ada, norm_g, w_ffn_in, w_ffn_out, w_attn_qkv, w_attn_o, attn_sink, w_conv_in, conv_w, conv_b, w_conv_out, w_hgrn_in, hgrn_lb, hgrn_norm_g, w_hgrn_out):
    cond = jnp.concatenate([c, c_ctx[None, :],
                            jnp.zeros((N_COND - DEC_BATCH - 1, D_MODEL), F32)], axis=0)
    mods = _ada_table(cond, w_ada, b_ada)
    cos, sin = _rope_tables()
    ck = cache_k.reshape(DEC_BATCH, -1, PAST_LEN, KV_DIM)
    cv = cache_v.reshape(DEC_BATCH, -1, PAST_LEN, KV_DIM)
    bf = lambda w: w.astype(BF16)
    w_ffn_in, w_ffn_out = bf(w_ffn_in), bf(w_ffn_out)
    w_attn_qkv, w_attn_o = bf(w_attn_qkv), bf(w_attn_o)
    w_conv_in, w_conv_out = bf(w_conv_in), bf(w_conv_out)
    w_hgrn_in, w_hgrn_out = bf(w_hgrn_in), bf(w_hgrn_out)
    xs = (x_sample.reshape(N_SAMPLE_TOK, D_MODEL), x_prompt.reshape(N_PROMPT_TOK, D_MODEL))
    new_k, new_v, new_s = [], [], []
    for li in range(DEPTH):
        kind, j = li % N_MIXERS, li // N_MIXERS
        mixer = None
        x = _ffn(xs, mods, norm_g, w_ffn_in, w_ffn_out, li, 0)
        if kind == 0:
            q, k, v, kf, vf = _qkv(x, mods, norm_g, w_attn_qkv, li, j, cos, sin)
            o_s = _attn_latent(attn_sink[j], q, k, v, ck, cv, j)
            o_p = _attn_context(attn_sink[j], q, k, v)
            mixer = (o_s, o_p, w_attn_o, j)
            new_k.append(kf.reshape(BATCH, SEQ, ATTN_KV_HEADS, HEAD_DIM))
            new_v.append(vf.reshape(BATCH, SEQ, ATTN_KV_HEADS, HEAD_DIM))
        elif kind == 1:
            x = _conv_layer(x, mods, norm_g, w_conv_in, conv_w, conv_b, w_conv_out, li, j)
        else:
            z = _proj(x, mods, norm_g, w_hgrn_in, li, j, F32)
            o_s = _hgrn_core(z, hgrn_lb, hgrn_norm_g, state_hgrn, j=j, batch=DEC_BATCH,
                             seq=DEC_SEQ, row_off=0, li=li, emit_state=False)[0]
            o_p, st = _hgrn_core(z, hgrn_lb, hgrn_norm_g, None, j=j, batch=BATCH, seq=SEQ,
                                 row_off=N_SAMPLE_TOK, li=li, emit_state=True)
            mixer = (o_s, o_p, w_hgrn_out, j)
            new_s.append(st)
        out = _ffn((x,), mods, norm_g, w_ffn_in, w_ffn_out, li, 1, split_out=(li == DEPTH - 1),
                   mixer=mixer)
        xs = tuple(out) if isinstance(out, (tuple, list)) else (out,)
    y_sample = xs[0].reshape(DEC_BATCH, DEC_SEQ, D_MODEL)
    y_prompt = xs[1].reshape(BATCH, SEQ, D_MODEL)
    return (y_prompt, y_sample, jnp.stack(new_k, axis=1), jnp.stack(new_v, axis=1),
            jnp.stack(new_s, axis=1))
```

```python
import functools
import math

import jax
import jax.numpy as jnp
import numpy as np
from jax import lax
from jax.experimental import pallas as pl
from jax.experimental.pallas import tpu as pltpu

D_MODEL = 1024
BATCH = 16
SEQ = 256
DEPTH = 4
DEC_BATCH = 8
DEC_SEQ = 1024
PAST_LEN = 256
GRID_W = 64
N_MIXERS = 3
N_MOD = 9
N_NORMS = 6
D_FF = 2816
FFN_RES = 0.5
ATTN_HEADS = 16
ATTN_KV_HEADS = 4
HEAD_DIM = 64
ATTN_DIM = ATTN_HEADS * HEAD_DIM
KV_DIM = ATTN_KV_HEADS * HEAD_DIM
BLOCK = 128
WINDOW = 128
ATTN_SCALE = HEAD_DIM ** -0.5
LOG2E = math.log2(math.e)
ROPE_BASE = 10000.0
ROPE_PAIRS_PER_AXIS = HEAD_DIM // 4
HGRN_HEADS = 8
HGRN_DK = 128
HGRN_DV = D_MODEL // HGRN_HEADS
HGRN_FD = HGRN_HEADS * HGRN_DK
EPS = 1e-6

N_SAMPLE_TOK = DEC_BATCH * DEC_SEQ
N_PROMPT_TOK = BATCH * SEQ
N_TOK = N_SAMPLE_TOK + N_PROMPT_TOK
N_COND = 16
CTX_COND = DEC_BATCH

LANES = 128
SUBLANES = 8
TM = 512
CONV_TM = 1024
FFN_SPLITS = (0, 1024, 2048, D_FF)
GLA_CHUNK = 128
NEG_BIG = -1e30
VMEM_LIMIT = 56 * 1024 * 1024

F32 = jnp.float32
BF16 = jnp.bfloat16


def _rms(x):
    return x * lax.rsqrt(jnp.mean(x * x, axis=-1, keepdims=True) + EPS)


def _silu(x):
    return x * jax.nn.sigmoid(x)


def _dot(a, b):
    return jnp.dot(a, b, preferred_element_type=F32)


def _dot_nt(a, b):
    return lax.dot_general(a, b, (((1,), (1,)), ((), ())), preferred_element_type=F32)


def _params(sem):
    return pltpu.CompilerParams(dimension_semantics=sem, vmem_limit_bytes=VMEM_LIMIT)


def _resident(arr, *lead):
    shape = (None,) * len(lead) + arr.shape[len(lead):]
    idx = tuple(lead) + (0,) * (arr.ndim - len(lead))
    return pl.BlockSpec(shape, lambda *_: idx, pipeline_mode=pl.Buffered(1))


def _mods_spec(li, tm):
    per = DEC_SEQ // tm
    return pl.BlockSpec((None, None, N_MOD, D_MODEL),
                        lambda i: (li, jnp.minimum(i // per, CTX_COND), 0, 0))


def _norm_spec(li):
    return pl.BlockSpec((None, N_NORMS, D_MODEL), lambda i: (li, 0, 0))


def _tok_spec(tm, n):
    return pl.BlockSpec((tm, n), lambda i: (i, 0))


def _split_specs(tm, n):
    ns = N_SAMPLE_TOK // tm
    return [pl.BlockSpec((tm, n), lambda i: (jnp.minimum(i, ns - 1), 0)),
            pl.BlockSpec((tm, n), lambda i: (jnp.maximum(i - ns, 0), 0))]


def _load_split(s_ref, p_ref, tm):
    return jnp.where(pl.program_id(0) < N_SAMPLE_TOK // tm, s_ref[...], p_ref[...])


def _store_split(s_ref, p_ref, val, tm):
    ns = N_SAMPLE_TOK // tm
    i = pl.program_id(0)

    @pl.when(i < ns)
    def _():
        s_ref[...] = val

    @pl.when(i >= ns)
    def _():
        p_ref[...] = val


def _ada_kernel(c_ref, w_ref, b_ref, o_ref):
    s = _silu(c_ref[...]).astype(BF16)
    o_ref[...] = _dot(s, w_ref[...].astype(BF16)) + b_ref[...]


def _ada_table(cond, w_ada, b_ada):
    tn = 1024
    n = N_MOD * D_MODEL
    out = pl.pallas_call(
        _ada_kernel,
        out_shape=jax.ShapeDtypeStruct((DEPTH, N_COND, n), F32),
        grid=(DEPTH, n // tn),
        in_specs=[
            pl.BlockSpec((N_COND, D_MODEL), lambda l, j: (0, 0)),
            pl.BlockSpec((None, D_MODEL, tn), lambda l, j: (l, 0, j)),
            pl.BlockSpec((None, 1, tn), lambda l, j: (l, 0, j)),
        ],
        out_specs=pl.BlockSpec((None, N_COND, tn), lambda l, j: (l, 0, j)),
        compiler_params=_params(("arbitrary", "arbitrary")),
        name="ada_table",
    )(cond, w_ada, b_ada.reshape(DEPTH, 1, n))
    return out.reshape(DEPTH, N_COND, N_MOD, D_MODEL)


W_IN_ROWS = 128
W_OUT_ROWS = 256


def _stream_weight(w_hbm, lead, stage, sem, dst):
    rows = stage.shape[1]
    n = dst.shape[0] // rows

    def piece(c):
        return pltpu.make_async_copy(w_hbm.at[(*lead, pl.ds(c * rows, rows))], stage.at[c % 2], sem.at[c % 2])

    piece(0).start()
    for c in range(n):
        if c + 1 < n:
            piece(c + 1).start()
        piece(c).wait()
        dst[c * rows:(c + 1) * rows, :] = stage[c % 2].astype(BF16)


def _ffn_kernel(*refs, li, which, split_in, split_out, has_mixer):
    k0, g0 = (0, 0) if which == 0 else (6, 4)
    n_in = 2 if split_in else 1
    x_refs, refs = refs[:n_in], refs[n_in:]
    mix_refs = ()
    if has_mixer:
        mix_refs, refs = refs[:3], refs[3:]
    m_ref, g_ref, win_hbm, wout_hbm = refs[:4]
    n_out = 2 if split_out else 1
    o_refs, refs = refs[4:4 + n_out], refs[4 + n_out:]
    win_ref, wout_ref, stage_in, stage_out, sem_in, sem_out = refs

    @pl.when(pl.program_id(0) == 0)
    def _():
        _stream_weight(win_hbm, (li, which), stage_in, sem_in, win_ref)
        _stream_weight(wout_hbm, (li, which), stage_out, sem_out, wout_ref)

    x = _load_split(*x_refs, TM) if split_in else x_refs[0][...]
    m = m_ref[...]
    g = g_ref[...]
    if has_mixer:
        as_ref, ap_ref, wo_ref = mix_refs
        mixed = _dot(_load_split(as_ref, ap_ref, TM), wo_ref[...].astype(BF16))
        x = x + m[5:6] * (_rms(mixed) * g[3:4])
    h = _rms(x) * g[g0:g0 + 1]
    h = (h * (1.0 + m[k0 + 1:k0 + 2]) + m[k0:k0 + 1]).astype(BF16)
    acc = None
    for lo, hi in zip(FFN_SPLITS[:-1], FFN_SPLITS[1:]):
        a = _dot(h, win_ref[:, lo:hi])
        b = _dot(h, win_ref[:, D_FF + lo:D_FF + hi])
        u = (_silu(a) * b).astype(BF16)
        part = _dot(u, wout_ref[lo:hi, :])
        acc = part if acc is None else acc + part
    y = _rms(acc) * g[g0 + 1:g0 + 2]
    res = x + FFN_RES * m[k0 + 2:k0 + 3] * y
    if split_out:
        _store_split(*o_refs, res, TM)
    else:
        o_refs[0][...] = res


def _ffn(xs, mods, norm_g, w_in, w_out, li, which, split_out=False, mixer=None):
    split_in = len(xs) == 2
    x_specs = _split_specs(TM, D_MODEL) if split_in else [_tok_spec(TM, D_MODEL)]
    mix_specs, mix_args = [], []
    if mixer is not None:
        a_s, a_p, w_o, j = mixer
        mix_specs = _split_specs(TM, D_MODEL) + [_resident(w_o, j)]
        mix_args = [a_s, a_p, w_o]
    if split_out:
        out_shape = (jax.ShapeDtypeStruct((N_SAMPLE_TOK, D_MODEL), F32),
                     jax.ShapeDtypeStruct((N_PROMPT_TOK, D_MODEL), F32))
        out_specs = tuple(_split_specs(TM, D_MODEL))
    else:
        out_shape = jax.ShapeDtypeStruct((N_TOK, D_MODEL), F32)
        out_specs = _tok_spec(TM, D_MODEL)
    hbm = pl.BlockSpec(memory_space=pl.ANY)
    return pl.pallas_call(
        functools.partial(_ffn_kernel, li=li, which=which, split_in=split_in, split_out=split_out,
                          has_mixer=mixer is not None),
        out_shape=out_shape,
        grid=(N_TOK // TM,),
        in_specs=x_specs + mix_specs + [_mods_spec(li, TM), _norm_spec(li), hbm, hbm],
        out_specs=out_specs,
        scratch_shapes=[
            pltpu.VMEM((D_MODEL, 2 * D_FF), BF16),
            pltpu.VMEM((D_FF, D_MODEL), BF16),
            pltpu.VMEM((2, W_IN_ROWS, 2 * D_FF), F32),
            pltpu.VMEM((2, W_OUT_ROWS, D_MODEL), F32),
            pltpu.SemaphoreType.DMA((2,)),
            pltpu.SemaphoreType.DMA((2,)),
        ],
        compiler_params=_params(("arbitrary",)),
        name=f"ffn{which}",
    )(*xs, *mix_args, mods, norm_g, w_in, w_out)


def _mixer_in(x_ref, m_ref, g_ref):
    m = m_ref[...]
    h = _rms(x_ref[...]) * g_ref[2:3, :]
    return (h * (1.0 + m[4:5]) + m[3:4]).astype(BF16)


def _proj_kernel(x_ref, m_ref, g_ref, w_ref, o_ref):
    o_ref[...] = _dot(_mixer_in(x_ref, m_ref, g_ref), w_ref[...]).astype(o_ref.dtype)


def _proj(x, mods, norm_g, w, li, j, out_dtype):
    n = w.shape[-1]
    return pl.pallas_call(
        _proj_kernel,
        out_shape=jax.ShapeDtypeStruct((N_TOK, n), out_dtype),
        grid=(N_TOK // TM,),
        in_specs=[_tok_spec(TM, D_MODEL), _mods_spec(li, TM), _norm_spec(li), _resident(w, j)],
        out_specs=_tok_spec(TM, n),
        compiler_params=_params(("arbitrary",)),
        name="mixer_proj",
    )(x, mods, norm_g, w)


def _rope_cols(t, cos, sin):
    lane = lax.broadcasted_iota(jnp.int32, (t.shape[0], LANES), 1)
    low = (lane % HEAD_DIM) < (HEAD_DIM // 2)
    outs = []
    for c in range(t.shape[1] // LANES):
        tc = t[:, c * LANES:(c + 1) * LANES]
        partner = jnp.where(low, pltpu.roll(tc, LANES - HEAD_DIM // 2, 1),
                            pltpu.roll(tc, HEAD_DIM // 2, 1))
        outs.append(tc * cos + partner * sin)
    return jnp.concatenate(outs, axis=1)


def _qkv_kernel(x_ref, m_ref, g_ref, w_ref, cos_ref, sin_ref,
                q_ref, k_ref, v_ref, kf_ref, vf_ref):
    h = _mixer_in(x_ref, m_ref, g_ref)
    cos = cos_ref[...]
    sin = sin_ref[...]
    for c0 in range(0, ATTN_DIM, KV_DIM):
        q = _dot(h, w_ref[:, c0:c0 + KV_DIM].astype(BF16))
        q_ref[:, c0:c0 + KV_DIM] = (_rope_cols(q, cos, sin) * (ATTN_SCALE * LOG2E)).astype(BF16)
    k = _dot(h, w_ref[:, ATTN_DIM:ATTN_DIM + KV_DIM].astype(BF16))
    v = _dot(h, w_ref[:, ATTN_DIM + KV_DIM:].astype(BF16))

    @pl.when(pl.program_id(0) >= N_SAMPLE_TOK // TM)
    def _():
        kf_ref[...] = k
        vf_ref[...] = v

    k_ref[...] = _rope_cols(k, cos, sin).astype(BF16)
    v_ref[...] = v.astype(BF16)


def _rope_tables():
    pos = np.arange(DEC_SEQ)
    row = (pos // GRID_W).astype(np.float32)
    col = (pos % GRID_W).astype(np.float32)
    inv = (ROPE_BASE ** (-np.arange(ROPE_PAIRS_PER_AXIS, dtype=np.float32) / ROPE_PAIRS_PER_AXIS)).astype(np.float32)
    ang = jnp.asarray(np.concatenate([row[:, None] * inv, col[:, None] * inv], axis=-1).astype(np.float32))
    cos, sin = jnp.cos(ang), jnp.sin(ang)
    cos = jnp.concatenate([cos, cos, cos, cos], axis=-1)
    sin = jnp.concatenate([-sin, sin, -sin, sin], axis=-1)
    cos = jnp.concatenate([cos, jnp.ones((TM, LANES), F32)], axis=0)
    sin = jnp.concatenate([sin, jnp.zeros((TM, LANES), F32)], axis=0)
    return cos, sin


def _qkv(x, mods, norm_g, w, li, j, cos, sin):
    per = DEC_SEQ // TM
    ns = N_SAMPLE_TOK // TM
    tab = pl.BlockSpec((TM, LANES), lambda i: (jnp.where(i < ns, i % per, per), 0))
    cache = pl.BlockSpec((TM, KV_DIM), lambda i: (jnp.maximum(i - ns, 0), 0))
    return pl.pallas_call(
        _qkv_kernel,
        out_shape=(
            jax.ShapeDtypeStruct((N_TOK, ATTN_DIM), BF16),
            jax.ShapeDtypeStruct((N_TOK, KV_DIM), BF16),
            jax.ShapeDtypeStruct((N_TOK, KV_DIM), BF16),
            jax.ShapeDtypeStruct((N_PROMPT_TOK, KV_DIM), F32),
            jax.ShapeDtypeStruct((N_PROMPT_TOK, KV_DIM), F32),
        ),
        grid=(N_TOK // TM,),
        in_specs=[_tok_spec(TM, D_MODEL), _mods_spec(li, TM), _norm_spec(li), _resident(w, j), tab, tab],
        out_specs=(_tok_spec(TM, ATTN_DIM), _tok_spec(TM, KV_DIM), _tok_spec(TM, KV_DIM), cache, cache),
        compiler_params=_params(("arbitrary",)),
        name="attn_qkv",
    )(x, mods, norm_g, w, cos, sin)


N_HEAD_SLOTS = 2 * ATTN_KV_HEADS


def _fill_head_slots(x, k_dst, v_dst, row0, is_value):
    dst = v_dst if is_value else k_dst
    pad = 1.0 if is_value else 0.0
    rows = x.shape[0]
    low = lax.broadcasted_iota(jnp.int32, (rows, LANES), 1) < HEAD_DIM
    for g in range(ATTN_KV_HEADS):
        cg, pg = g // 2, g % 2
        xb = x[:, cg * LANES:(cg + 1) * LANES]
        xr = pltpu.roll(xb, HEAD_DIM, 1)
        x_lo, x_hi = (xb, xr) if pg == 0 else (xr, xb)
        dst[2 * g, pl.ds(row0, rows), :] = jnp.where(low, x_lo, pad).astype(BF16)
        dst[2 * g + 1, pl.ds(row0, rows), :] = jnp.where(low, pad, x_hi).astype(BF16)


def _attend_rows(q, key_blocks, sink_ref, o_ref, orow0, s_scr, p_scr):
    rows = q.shape[0]
    rid = lax.broadcasted_iota(jnp.int32, (2 * rows, 1), 0)
    low = lax.broadcasted_iota(jnp.int32, (2 * rows, LANES), 1) < HEAD_DIM
    sinks, maxes = [], []
    for slot in range(N_HEAD_SLOTS):
        g, side = slot // 2, slot % 2
        c0 = 2 * g * LANES
        qs = jnp.concatenate([q[:, c0:c0 + LANES], q[:, c0 + LANES:c0 + 2 * LANES]], axis=0)
        sk = jnp.where(rid < rows, sink_ref[4 * g + side], sink_ref[4 * g + 2 + side]) * LOG2E
        mx = sk
        col = 0
        for k_of, _, bias, n_keys in key_blocks:
            s = _dot_nt(qs, k_of(slot))
            if bias is not None:
                s = s + bias
            s_scr[slot, :, col:col + n_keys] = s
            mx = jnp.maximum(mx, jnp.max(s, axis=-1, keepdims=True))
            col += n_keys
        sinks.append(sk)
        maxes.append(mx)
    for slot in range(N_HEAD_SLOTS):
        p_scr[slot] = jnp.exp2(s_scr[slot] - maxes[slot]).astype(BF16)
    for g in range(ATTN_KV_HEADS):
        res = []
        for side in range(2):
            slot = 2 * g + side
            part = None
            col = 0
            for _, v_of, _, n_keys in key_blocks:
                pv = _dot(p_scr[slot, :, col:col + n_keys], v_of(slot))
                part = pv if part is None else part + pv
                col += n_keys
            den = pltpu.roll(part, HEAD_DIM, 1) + jnp.exp2(sinks[slot] - maxes[slot])
            res.append(part / den)
        out = jnp.where(low, res[0], res[1]).astype(o_ref.dtype)
        c0 = 2 * g * LANES
        o_ref[pl.ds(orow0, rows), c0:c0 + LANES] = out[:rows]
        o_ref[pl.ds(orow0, rows), c0 + LANES:c0 + 2 * LANES] = out[rows:]


def _attn_lat_kernel(sink_ref, q_ref, k_ref, v_ref, ck_ref, cv_ref, o_ref,
                     kl_ref, vl_ref, kc_ref, vc_ref, s_scr, p_scr):
    nb = DEC_SEQ // BLOCK
    zeros = jnp.zeros((BLOCK, LANES), BF16)
    for slot in range(N_HEAD_SLOTS):
        for ref in (kl_ref, vl_ref):
            ref[slot, pl.ds(0, BLOCK), :] = zeros
            ref[slot, pl.ds(DEC_SEQ + BLOCK, BLOCK), :] = zeros
    _fill_head_slots(k_ref[...].astype(F32), kl_ref, vl_ref, BLOCK, False)
    _fill_head_slots(v_ref[...].astype(F32), kl_ref, vl_ref, BLOCK, True)
    _fill_head_slots(ck_ref[...], kc_ref, vc_ref, 0, False)
    _fill_head_slots(cv_ref[...], kc_ref, vc_ref, 0, True)
    n_band = 3 * BLOCK
    qi = lax.broadcasted_iota(jnp.int32, (2 * BLOCK, n_band), 0) % BLOCK
    u = lax.broadcasted_iota(jnp.int32, (2 * BLOCK, n_band), 1)
    in_window = jnp.abs(qi + BLOCK - u) <= WINDOW

    def body(n, carry):
        r0 = pl.multiple_of(n * BLOCK, BLOCK)
        kpos = (n - 1) * BLOCK + u
        ok = in_window & (kpos >= 0) & (kpos < nb * BLOCK)
        bias = jnp.where(ok, 0.0, NEG_BIG)
        band = (lambda s: kl_ref[s, pl.ds(r0, n_band), :], lambda s: vl_ref[s, pl.ds(r0, n_band), :],
                bias, n_band)
        ctx = (lambda s: kc_ref[s], lambda s: vc_ref[s], None, PAST_LEN)
        _attend_rows(q_ref[pl.ds(r0, BLOCK), :], [band, ctx], sink_ref, o_ref, r0, s_scr, p_scr)
        return carry

    lax.fori_loop(0, nb, body, 0)


def _attn_latent(sink, q, k, v, ck, cv, j):
    seq = lambda n: pl.BlockSpec((DEC_SEQ, n), lambda b: (b, 0))
    ctx = pl.BlockSpec((None, None, PAST_LEN, KV_DIM), lambda b: (b, j, 0, 0))
    return pl.pallas_call(
        _attn_lat_kernel,
        out_shape=jax.ShapeDtypeStruct((N_SAMPLE_TOK, ATTN_DIM), BF16),
        grid=(DEC_BATCH,),
        in_specs=[pl.BlockSpec(memory_space=pltpu.SMEM), seq(ATTN_DIM), seq(KV_DIM), seq(KV_DIM), ctx, ctx],
        out_specs=seq(ATTN_DIM),
        scratch_shapes=[
            pltpu.VMEM((N_HEAD_SLOTS, DEC_SEQ + 2 * BLOCK, LANES), BF16),
            pltpu.VMEM((N_HEAD_SLOTS, DEC_SEQ + 2 * BLOCK, LANES), BF16),
            pltpu.VMEM((N_HEAD_SLOTS, PAST_LEN, LANES), BF16),
            pltpu.VMEM((N_HEAD_SLOTS, PAST_LEN, LANES), BF16),
            pltpu.VMEM((N_HEAD_SLOTS, 2 * BLOCK, 3 * BLOCK + PAST_LEN), F32),
            pltpu.VMEM((N_HEAD_SLOTS, 2 * BLOCK, 3 * BLOCK + PAST_LEN), BF16),
        ],
        compiler_params=_params(("arbitrary",)),
        name="attn_latent",
    )(sink, q, k, v, ck, cv)


def _attn_ctx_kernel(sink_ref, q_ref, k_ref, v_ref, o_ref, ks_ref, vs_ref, s_scr, p_scr):
    _fill_head_slots(k_ref[...].astype(F32), ks_ref, vs_ref, 0, False)
    _fill_head_slots(v_ref[...].astype(F32), ks_ref, vs_ref, 0, True)
    blk = (lambda s: ks_ref[s], lambda s: vs_ref[s], None, SEQ)
    _attend_rows(q_ref[...], [blk], sink_ref, o_ref, 0, s_scr, p_scr)


def _attn_context(sink, q, k, v):
    off = N_SAMPLE_TOK // SEQ
    row = lambda n: pl.BlockSpec((SEQ, n), lambda b: (off + b, 0))
    return pl.pallas_call(
        _attn_ctx_kernel,
        out_shape=jax.ShapeDtypeStruct((N_PROMPT_TOK, ATTN_DIM), BF16),
        grid=(BATCH,),
        in_specs=[pl.BlockSpec(memory_space=pltpu.SMEM), row(ATTN_DIM), row(KV_DIM), row(KV_DIM)],
        out_specs=pl.BlockSpec((SEQ, ATTN_DIM), lambda b: (b, 0)),
        scratch_shapes=[pltpu.VMEM((N_HEAD_SLOTS, SEQ, LANES), BF16),
                        pltpu.VMEM((N_HEAD_SLOTS, SEQ, LANES), BF16),
                        pltpu.VMEM((N_HEAD_SLOTS, 2 * SEQ, SEQ), F32),
                        pltpu.VMEM((N_HEAD_SLOTS, 2 * SEQ, SEQ), BF16)],
        compiler_params=_params(("arbitrary",)),
        name="attn_context",
    )(sink, q, k, v)


CONV_COLS = 256


def _conv_kernel(x_ref, m_ref, g_ref, win_ref, cw_ref, cb_ref, wout_ref, o_ref, a_ref):
    i = pl.program_id(0)
    h = _mixer_in(x_ref, m_ref, g_ref)
    tm = h.shape[0]
    seq = jnp.where(i < N_SAMPLE_TOK // tm, DEC_SEQ, SEQ)
    pos = lax.broadcasted_iota(jnp.int32, (tm, CONV_COLS), 0) & (seq - 1)
    first = pos == 0
    last = pos == seq - 1
    cw = cw_ref[...]
    cb = cb_ref[...]
    for c in range(D_MODEL // CONV_COLS):
        cols = slice(c * CONV_COLS, (c + 1) * CONV_COLS)
        part = lambda j: _dot(h, win_ref[:, j * D_MODEL + c * CONV_COLS:j * D_MODEL + (c + 1) * CONV_COLS])
        z = part(1) * part(2)
        zl = jnp.where(first, 0.0, pltpu.roll(z, 1, 0))
        zr = jnp.where(last, 0.0, pltpu.roll(z, tm - 1, 0))
        conv = cb[:, cols] + zl * cw[0:1, cols] + z * cw[1:2, cols] + zr * cw[2:3, cols]
        a_ref[:, cols] = (part(0) * conv).astype(BF16)
    y = _rms(_dot(a_ref[...], wout_ref[...].astype(BF16))) * g_ref[3:4, :]
    o_ref[...] = x_ref[...] + m_ref[5:6, :] * y


def _conv_layer(x, mods, norm_g, w_in, cw, cb, w_out, li, j):
    tm = CONV_TM
    return pl.pallas_call(
        _conv_kernel,
        out_shape=jax.ShapeDtypeStruct((N_TOK, D_MODEL), F32),
        grid=(N_TOK // tm,),
        in_specs=[
            _tok_spec(tm, D_MODEL), _mods_spec(li, tm), _norm_spec(li), _resident(w_in, j),
            pl.BlockSpec((None, 3, D_MODEL), lambda i: (j, 0, 0)),
            pl.BlockSpec((None, 1, D_MODEL), lambda i: (j, 0, 0)),
            _resident(w_out, j),
        ],
        out_specs=_tok_spec(tm, D_MODEL),
        scratch_shapes=[pltpu.VMEM((tm, D_MODEL), BF16)],
        compiler_params=_params(("arbitrary",)),
        name="conv_layer",
    )(x, mods, norm_g, w_in, cw, cb.reshape(-1, 1, D_MODEL), w_out)


def _block_row(x, w, off):
    c = x.shape[0]
    blk = 2 * w
    row = lambda r: jnp.broadcast_to(x[r:r + 1, :], (max(blk, SUBLANES), LANES))
    if blk >= SUBLANES:
        return jnp.concatenate([row(j * blk + off) for j in range(c // blk)], axis=0)
    if blk == 4:
        upper = lax.broadcasted_iota(jnp.int32, (SUBLANES, LANES), 0) < 4
        return jnp.concatenate(
            [jnp.where(upper, row(SUBLANES * v + off), row(SUBLANES * v + 4 + off))
             for v in range(c // SUBLANES)], axis=0)
    odd = (lax.broadcasted_iota(jnp.int32, (c, LANES), 0) & 1) != 0
    if off == 0:
        return jnp.where(odd, pltpu.roll(x, 1, 0), x)
    return jnp.where(odd, x, pltpu.roll(x, c - 1, 0))


def _split3(x):
    hi = x.astype(BF16)
    r = x - hi.astype(F32)
    mid = r.astype(BF16)
    lo = (r - mid.astype(F32)).astype(BF16)
    return jnp.concatenate([hi, mid, lo], axis=0)


def _hgrn_kernel(*refs, seq, li, hp, has_init, emit_state):
    lb_ref, lv_ref, zq_ref, zi_ref, zf_ref, zb_ref, zg_ref, ng_ref = refs[:8]
    refs = refs[8:]
    s0_ref = None
    if has_init:
        s0_ref, refs = refs[0], refs[1:]
    o_ref, refs = refs[0], refs[1:]
    sout_ref = None
    if emit_state:
        sout_ref, refs = refs[0], refs[1:]
    acc_ref, qf_ref, qb_ref, kv_ref, dec_ref, stf_ref, stb_ref = refs
    c = GLA_CHUNK
    nc = seq // c

    def lower_bound(x):
        e = jnp.exp(x - jnp.max(x, axis=0, keepdims=True))
        pr = e / jnp.sum(e, axis=0, keepdims=True)
        return jnp.sum(pr[1:li + 1], axis=0, keepdims=True)

    lbf, lbb = lower_bound(lb_ref[0]), lower_bound(lb_ref[1])
    ri = lax.broadcasted_iota(jnp.int32, (c, c), 0)
    ci = lax.broadcasted_iota(jnp.int32, (c, c), 1)
    tril = jnp.where(ci <= ri, 1.0, 0.0).astype(BF16)
    triu = jnp.where(ci >= ri, 1.0, 0.0).astype(BF16)
    tril3 = jnp.concatenate([tril, tril, tril], axis=1)
    triu3 = jnp.concatenate([triu, triu, triu], axis=1)
    lv = lv_ref[...]
    rowi = lax.broadcasted_iota(jnp.int32, (c, LANES), 0)

    def chunk_local(n, hd):
        rows = pl.ds(pl.multiple_of(n * c, c), c)
        cols = slice(hd * LANES, (hd + 1) * LANES)
        q = _silu(zq_ref[rows, cols]) * (HGRN_DK ** -0.5)
        zi = zi_ref[rows, cols]
        v = zi.astype(BF16)
        ff = lbf[:, cols] + (1.0 - lbf[:, cols]) * jax.nn.sigmoid(zf_ref[rows, cols])
        fb = lbb[:, cols] + (1.0 - lbb[:, cols]) * jax.nn.sigmoid(zb_ref[rows, cols])
        kf = 1.0 - ff
        kb = 1.0 - fb
        pre = _dot(tril3, _split3(jnp.log2(ff)))
        suf = _dot(triu3, _split3(jnp.log2(fb)))
        attn = jnp.where(lv == 0, _dot_nt(q.astype(BF16), (kf + kb).astype(BF16)), 0.0)
        w, level = 1, 1
        while w < c:
            d_f = pre - _block_row(pre, w, w - 1)
            d_b = suf - _block_row(suf, w, w)
            xq = jnp.minimum(d_f, d_b)
            xk = xq - (d_f + d_b)
            second = (rowi & w) != 0
            qh = (q * jnp.exp2(xq)).astype(BF16)
            kh = (jnp.where(second, kb, kf) * jnp.exp2(xk)).astype(BF16)
            attn = jnp.where(lv == level, _dot_nt(qh, kh), attn)
            w *= 2
            level += 1
        acc_ref[hd, rows, :] = _dot(attn.astype(BF16), v)
        tot_f = pre[c - 1:c, :]
        tot_b = suf[0:1, :]
        qf_ref[hd, rows, :] = (q * jnp.exp2(pre)).astype(BF16)
        qb_ref[hd, rows, :] = (q * jnp.exp2(suf)).astype(BF16)
        kt = jnp.concatenate([(kf * jnp.exp2(tot_f - pre)).astype(BF16),
                              (kb * jnp.exp2(tot_b - suf)).astype(BF16)], axis=1)
        kv_ref[hd, n] = _dot(zi.T.astype(BF16), kt)
        dec_ref[hd, n] = jnp.exp2(jnp.concatenate([tot_f, tot_b], axis=1))

    def local_body(n, carry):
        for hd in range(hp):
            chunk_local(n, hd)
        return carry

    lax.fori_loop(0, nc, local_body, 0, unroll=min(nc, 4 // hp))

    for hd in range(hp):
        if has_init:
            stf_ref[hd] = s0_ref[0, hd].T
            stb_ref[hd] = s0_ref[1, hd].T
        else:
            stf_ref[hd] = jnp.zeros((HGRN_DV, HGRN_DK), F32)
            stb_ref[hd] = jnp.zeros((HGRN_DV, HGRN_DK), F32)

    def carry_state(i, carry):
        n, m = i, nc - 1 - i
        rn = pl.ds(pl.multiple_of(n * c, c), c)
        rm = pl.ds(pl.multiple_of(m * c, c), c)
        for hd in range(hp):
            stf = stf_ref[hd]
            stb = stb_ref[hd]
            acc_ref[hd, rn, :] += _dot_nt(qf_ref[hd, rn, :], stf.astype(BF16))
            acc_ref[hd, rm, :] += _dot_nt(qb_ref[hd, rm, :], stb.astype(BF16))
            stf_ref[hd] = stf * dec_ref[hd, n][:, :HGRN_DK] + kv_ref[hd, n][:, :HGRN_DK]
            stb_ref[hd] = stb * dec_ref[hd, m][:, HGRN_DK:] + kv_ref[hd, m][:, HGRN_DK:]
        return carry

    lax.fori_loop(0, nc, carry_state, 0, unroll=True)
    for hd in range(hp):
        cols = slice(hd * LANES, (hd + 1) * LANES)
        o_ref[:, cols] = (_rms(acc_ref[hd]) * ng_ref[...] * _silu(zg_ref[:, cols])).astype(o_ref.dtype)
        if emit_state:
            sout_ref[0, hd] = stf_ref[hd].T
            sout_ref[1, hd] = stb_ref[hd].T


def _pair_levels():
    t = np.arange(GLA_CHUNK)
    x = t[:, None] ^ t[None, :]
    return jnp.asarray(np.where(x > 0, np.floor(np.log2(np.maximum(x, 1))) + 1, 0).astype(np.int32))


def _hgrn_core(z, lb, ng, s0, *, j, batch, seq, row_off, li, emit_state):
    has_init = s0 is not None
    blk0 = row_off // seq
    fd = HGRN_FD // LANES
    dm = D_MODEL // LANES
    c = GLA_CHUNK
    hp = max(1, 4 * c // seq)
    width = hp * LANES

    def col(c0):
        return pl.BlockSpec((seq, width), lambda b, h: (blk0 + b, c0 // hp + h))

    in_specs = [
        pl.BlockSpec((2, DEPTH, width), lambda b, h: (0, 0, h)),
        pl.BlockSpec((c, c), lambda b, h: (0, 0)),
        col(0), col(fd), col(fd + dm), col(2 * fd + dm), col(3 * fd + dm),
        pl.BlockSpec((None, 1, HGRN_DV), lambda b, h: (j, 0, 0)),
    ]
    args = [lb, _pair_levels(), z, z, z, z, z, ng.reshape(-1, 1, HGRN_DV)]
    if has_init:
        in_specs.append(pl.BlockSpec((None, None, 2, hp, HGRN_DK, HGRN_DV),
                                     lambda b, h: (b, j, 0, h, 0, 0)))
        args.append(s0)
    out_shape = [jax.ShapeDtypeStruct((batch * seq, D_MODEL), BF16)]
    out_specs = [pl.BlockSpec((seq, width), lambda b, h: (b, h))]
    if emit_state:
        out_shape.append(jax.ShapeDtypeStruct((batch, 2, HGRN_HEADS, HGRN_DK, HGRN_DV), F32))
        out_specs.append(pl.BlockSpec((None, 2, hp, HGRN_DK, HGRN_DV), lambda b, h: (b, 0, h, 0, 0)))
    return pl.pallas_call(
        functools.partial(_hgrn_kernel, seq=seq, li=li, hp=hp, has_init=has_init, emit_state=emit_state),
        out_shape=tuple(out_shape),
        grid=(batch, HGRN_HEADS // hp),
        in_specs=in_specs,
        out_specs=tuple(out_specs),
        scratch_shapes=[
            pltpu.VMEM((hp, seq, HGRN_DV), F32),
            pltpu.VMEM((hp, seq, HGRN_DK), BF16),
            pltpu.VMEM((hp, seq, HGRN_DK), BF16),
            pltpu.VMEM((hp, seq // c, HGRN_DV, 2 * HGRN_DK), F32),
            pltpu.VMEM((hp, seq // c, 1, 2 * HGRN_DK), F32),
            pltpu.VMEM((hp, HGRN_DV, HGRN_DK), F32),
            pltpu.VMEM((hp, HGRN_DV, HGRN_DK), F32),
        ],
        compiler_params=_params(("arbitrary", "arbitrary")),
        name=f"hgrn_core_{seq}",
    )(*args)


def kernel(x_prompt, x_sample, cache_k, cache_v, state_hgrn, c, c_ctx, w_ada, b_ada, norm_g, w_ffn_in, w_ffn_out, w_attn_qkv, w_attn_o, attn_sink, w_conv_in, conv_w, conv_b, w_conv_out, w_hgrn_in, hgrn_lb, hgrn_norm_g, w_hgrn_out):
    cond = jnp.concatenate([c, c_ctx[None, :],
                            jnp.zeros((N_COND - DEC_BATCH - 1, D_MODEL), F32)], axis=0)
    mods = _ada_table(cond, w_ada, b_ada)
    cos, sin = _rope_tables()
    ck = cache_k.reshape(DEC_BATCH, -1, PAST_LEN, KV_DIM)
    cv = cache_v.reshape(DEC_BATCH, -1, PAST_LEN, KV_DIM)
    bf = lambda w: w.astype(BF16)
    w_conv_in, w_hgrn_in = bf(w_conv_in), bf(w_hgrn_in)
    xs = (x_sample.reshape(N_SAMPLE_TOK, D_MODEL), x_prompt.reshape(N_PROMPT_TOK, D_MODEL))
    new_k, new_v, new_s = [], [], []
    for li in range(DEPTH):
        kind, j = li % N_MIXERS, li // N_MIXERS
        mixer = None
        x = _ffn(xs, mods, norm_g, w_ffn_in, w_ffn_out, li, 0)
        if kind == 0:
            q, k, v, kf, vf = _qkv(x, mods, norm_g, w_attn_qkv, li, j, cos, sin)
            o_s = _attn_latent(attn_sink[j], q, k, v, ck, cv, j)
            o_p = _attn_context(attn_sink[j], q, k, v)
            mixer = (o_s, o_p, w_attn_o, j)
            new_k.append(kf.reshape(BATCH, SEQ, ATTN_KV_HEADS, HEAD_DIM))
            new_v.append(vf.reshape(BATCH, SEQ, ATTN_KV_HEADS, HEAD_DIM))
        elif kind == 1:
            x = _conv_layer(x, mods, norm_g, w_conv_in, conv_w, conv_b, w_conv_out, li, j)
        else:
            z = _proj(x, mods, norm_g, w_hgrn_in, li, j, F32)
            o_s = _hgrn_core(z, hgrn_lb, hgrn_norm_g, state_hgrn, j=j, batch=DEC_BATCH,
                             seq=DEC_SEQ, row_off=0, li=li, emit_state=False)[0]
            o_p, st = _hgrn_core(z, hgrn_lb, hgrn_norm_g, None, j=j, batch=BATCH, seq=SEQ,
                                 row_off=N_SAMPLE_TOK, li=li, emit_state=True)
            mixer = (o_s, o_p, w_hgrn_out, j)
            new_s.append(st)
        out = _ffn((x,), mods, norm_g, w_ffn_in, w_ffn_out, li, 1, split_out=(li == DEPTH - 1),
                   mixer=mixer)
        xs = tuple(out) if isinstance(out, (tuple, list)) else (out,)
    y_sample = xs[0].reshape(DEC_BATCH, DEC_SEQ, D_MODEL)
    y_prompt = xs[1].reshape(BATCH, SEQ, D_MODEL)
    return (y_prompt, y_sample, jnp.stack(new_k, axis=1), jnp.stack(new_v, axis=1),
            jnp.stack(new_s, axis=1))
```

```python
import functools
import math

import jax
import jax.numpy as jnp
import numpy as np
from jax import lax
from jax.experimental import pallas as pl
from jax.experimental.pallas import tpu as pltpu

D_MODEL = 1024
BATCH = 16
SEQ = 256
DEPTH = 4
DEC_BATCH = 8
DEC_SEQ = 1024
PAST_LEN = 256
GRID_W = 64
N_MIXERS = 3
N_MOD = 9
N_NORMS = 6
D_FF = 2816
FFN_RES = 0.5
ATTN_HEADS = 16
ATTN_KV_HEADS = 4
HEAD_DIM = 64
ATTN_DIM = ATTN_HEADS * HEAD_DIM
KV_DIM = ATTN_KV_HEADS * HEAD_DIM
BLOCK = 128
WINDOW = 128
ATTN_SCALE = HEAD_DIM ** -0.5
LOG2E = math.log2(math.e)
ROPE_BASE = 10000.0
ROPE_PAIRS_PER_AXIS = HEAD_DIM // 4
HGRN_HEADS = 8
HGRN_DK = 128
HGRN_DV = D_MODEL // HGRN_HEADS
HGRN_FD = HGRN_HEADS * HGRN_DK
EPS = 1e-6

N_SAMPLE_TOK = DEC_BATCH * DEC_SEQ
N_PROMPT_TOK = BATCH * SEQ
N_TOK = N_SAMPLE_TOK + N_PROMPT_TOK
N_COND = 16
CTX_COND = DEC_BATCH

LANES = 128
SUBLANES = 8
TM = 512
CONV_TM = 1024
FFN_SPLITS = (0, 1024, 2048, D_FF)
GLA_CHUNK = 128
GLA_ROWS_IN_FLIGHT = 1024
NEG_BIG = -1e30
VMEM_LIMIT = 56 * 1024 * 1024

F32 = jnp.float32
BF16 = jnp.bfloat16


def _rms(x):
    return x * lax.rsqrt(jnp.mean(x * x, axis=-1, keepdims=True) + EPS)


def _silu(x):
    return x * jax.nn.sigmoid(x)


def _dot(a, b):
    return jnp.dot(a, b, preferred_element_type=F32)


def _dot_nt(a, b):
    return lax.dot_general(a, b, (((1,), (1,)), ((), ())), preferred_element_type=F32)


def _params(sem):
    return pltpu.CompilerParams(dimension_semantics=sem, vmem_limit_bytes=VMEM_LIMIT)


def _resident(arr, *lead):
    shape = (None,) * len(lead) + arr.shape[len(lead):]
    idx = tuple(lead) + (0,) * (arr.ndim - len(lead))
    return pl.BlockSpec(shape, lambda *_: idx, pipeline_mode=pl.Buffered(1))


def _mods_spec(li, tm):
    per = DEC_SEQ // tm
    return pl.BlockSpec((None, None, N_MOD, D_MODEL),
                        lambda i: (li, jnp.minimum(i // per, CTX_COND), 0, 0))


def _norm_spec(li):
    return pl.BlockSpec((None, N_NORMS, D_MODEL), lambda i: (li, 0, 0))


def _tok_spec(tm, n):
    return pl.BlockSpec((tm, n), lambda i: (i, 0))


def _split_specs(tm, n):
    ns = N_SAMPLE_TOK // tm
    return [pl.BlockSpec((tm, n), lambda i: (jnp.minimum(i, ns - 1), 0)),
            pl.BlockSpec((tm, n), lambda i: (jnp.maximum(i - ns, 0), 0))]


def _load_split(s_ref, p_ref, tm):
    return jnp.where(pl.program_id(0) < N_SAMPLE_TOK // tm, s_ref[...], p_ref[...])


def _store_split(s_ref, p_ref, val, tm):
    ns = N_SAMPLE_TOK // tm
    i = pl.program_id(0)

    @pl.when(i < ns)
    def _():
        s_ref[...] = val

    @pl.when(i >= ns)
    def _():
        p_ref[...] = val


def _ada_kernel(c_ref, w_ref, b_ref, o_ref):
    s = _silu(c_ref[...]).astype(BF16)
    o_ref[...] = _dot(s, w_ref[...].astype(BF16)) + b_ref[...]


def _ada_table(cond, w_ada, b_ada):
    tn = 1024
    n = N_MOD * D_MODEL
    out = pl.pallas_call(
        _ada_kernel,
        out_shape=jax.ShapeDtypeStruct((DEPTH, N_COND, n), F32),
        grid=(DEPTH, n // tn),
        in_specs=[
            pl.BlockSpec((N_COND, D_MODEL), lambda l, j: (0, 0)),
            pl.BlockSpec((None, D_MODEL, tn), lambda l, j: (l, 0, j)),
            pl.BlockSpec((None, 1, tn), lambda l, j: (l, 0, j)),
        ],
        out_specs=pl.BlockSpec((None, N_COND, tn), lambda l, j: (l, 0, j)),
        compiler_params=_params(("arbitrary", "arbitrary")),
        name="ada_table",
    )(cond, w_ada, b_ada.reshape(DEPTH, 1, n))
    return out.reshape(DEPTH, N_COND, N_MOD, D_MODEL)


NEXT_IN_ROWS = 64
NEXT_OUT_ROWS = 128


def _ffn_kernel(*refs, k0, g0, split_in, split_out, has_mixer, has_next):
    n_in = 2 if split_in else 1
    x_refs, refs = refs[:n_in], refs[n_in:]
    mix_refs = ()
    if has_mixer:
        mix_refs, refs = refs[:3], refs[3:]
    m_ref, g_ref, win_ref, wout_ref = refs[:4]
    refs = refs[4:]
    if has_next:
        nin_ref, nout_ref = refs[:2]
        refs[-2][...] = nin_ref[...].astype(BF16)
        refs[-1][...] = nout_ref[...].astype(BF16)
        refs = refs[2:-2]
    o_refs = refs
    x = _load_split(*x_refs, TM) if split_in else x_refs[0][...]
    m = m_ref[...]
    g = g_ref[...]
    if has_mixer:
        as_ref, ap_ref, wo_ref = mix_refs
        mixed = _dot(_load_split(as_ref, ap_ref, TM), wo_ref[...].astype(BF16))
        x = x + m[5:6] * (_rms(mixed) * g[3:4])
    h = _rms(x) * g[g0:g0 + 1]
    h = (h * (1.0 + m[k0 + 1:k0 + 2]) + m[k0:k0 + 1]).astype(BF16)
    acc = None
    for lo, hi in zip(FFN_SPLITS[:-1], FFN_SPLITS[1:]):
        a = _dot(h, win_ref[:, lo:hi])
        b = _dot(h, win_ref[:, D_FF + lo:D_FF + hi])
        u = (_silu(a) * b).astype(BF16)
        part = _dot(u, wout_ref[lo:hi, :])
        acc = part if acc is None else acc + part
    y = _rms(acc) * g[g0 + 1:g0 + 2]
    res = x + FFN_RES * m[k0 + 2:k0 + 3] * y
    if split_out:
        _store_split(*o_refs, res, TM)
    else:
        o_refs[0][...] = res


def _ffn(xs, mods, norm_g, w_bf, nxt, li, which, split_out=False, mixer=None):
    k0, g0 = (0, 0) if which == 0 else (6, 4)
    split_in = len(xs) == 2
    steps = N_TOK // TM
    x_specs = _split_specs(TM, D_MODEL) if split_in else [_tok_spec(TM, D_MODEL)]
    mix_specs, mix_args = [], []
    if mixer is not None:
        a_s, a_p, w_o, j = mixer
        mix_specs = _split_specs(TM, D_MODEL) + [_resident(w_o, j)]
        mix_args = [a_s, a_p, w_o]
    if split_out:
        out_shape = [jax.ShapeDtypeStruct((N_SAMPLE_TOK, D_MODEL), F32),
                     jax.ShapeDtypeStruct((N_PROMPT_TOK, D_MODEL), F32)]
        out_specs = _split_specs(TM, D_MODEL)
    else:
        out_shape = [jax.ShapeDtypeStruct((N_TOK, D_MODEL), F32)]
        out_specs = [_tok_spec(TM, D_MODEL)]
    nxt_specs, nxt_args = [], []
    if nxt is not None:
        w_in_all, w_out_all, li2, which2 = nxt
        last_in = D_MODEL // NEXT_IN_ROWS - 1
        last_out = D_FF // NEXT_OUT_ROWS - 1
        assert last_in < steps and last_out < steps
        nxt_specs = [
            pl.BlockSpec((None, None, NEXT_IN_ROWS, 2 * D_FF),
                         lambda i: (li2, which2, jnp.minimum(i, last_in), 0)),
            pl.BlockSpec((None, None, NEXT_OUT_ROWS, D_MODEL),
                         lambda i: (li2, which2, jnp.minimum(i, last_out), 0)),
        ]
        nxt_args = [w_in_all, w_out_all]
        out_shape += [jax.ShapeDtypeStruct((D_MODEL, 2 * D_FF), BF16),
                      jax.ShapeDtypeStruct((D_FF, D_MODEL), BF16)]
        out_specs += [pl.BlockSpec((NEXT_IN_ROWS, 2 * D_FF), lambda i: (jnp.minimum(i, last_in), 0)),
                      pl.BlockSpec((NEXT_OUT_ROWS, D_MODEL), lambda i: (jnp.minimum(i, last_out), 0))]
    outs = pl.pallas_call(
        functools.partial(_ffn_kernel, k0=k0, g0=g0, split_in=split_in, split_out=split_out,
                          has_mixer=mixer is not None, has_next=nxt is not None),
        out_shape=tuple(out_shape),
        grid=(steps,),
        in_specs=x_specs + mix_specs + [
            _mods_spec(li, TM), _norm_spec(li), _resident(w_bf[0]), _resident(w_bf[1])] + nxt_specs,
        out_specs=tuple(out_specs),
        compiler_params=_params(("arbitrary",)),
        name=f"ffn{which}",
    )(*xs, *mix_args, mods, norm_g, *w_bf, *nxt_args)
    n_x = 2 if split_out else 1
    return tuple(outs[:n_x]), tuple(outs[n_x:])


def _mixer_in(x_ref, m_ref, g_ref):
    m = m_ref[...]
    h = _rms(x_ref[...]) * g_ref[2:3, :]
    return (h * (1.0 + m[4:5]) + m[3:4]).astype(BF16)


def _proj_kernel(x_ref, m_ref, g_ref, w_ref, o_ref):
    o_ref[...] = _dot(_mixer_in(x_ref, m_ref, g_ref), w_ref[...]).astype(o_ref.dtype)


def _proj(x, mods, norm_g, w, li, j, out_dtype):
    n = w.shape[-1]
    return pl.pallas_call(
        _proj_kernel,
        out_shape=jax.ShapeDtypeStruct((N_TOK, n), out_dtype),
        grid=(N_TOK // TM,),
        in_specs=[_tok_spec(TM, D_MODEL), _mods_spec(li, TM), _norm_spec(li), _resident(w, j)],
        out_specs=_tok_spec(TM, n),
        compiler_params=_params(("arbitrary",)),
        name="mixer_proj",
    )(x, mods, norm_g, w)


def _rope_cols(t, cos, sin):
    lane = lax.broadcasted_iota(jnp.int32, (t.shape[0], LANES), 1)
    low = (lane % HEAD_DIM) < (HEAD_DIM // 2)
    outs = []
    for c in range(t.shape[1] // LANES):
        tc = t[:, c * LANES:(c + 1) * LANES]
        partner = jnp.where(low, pltpu.roll(tc, LANES - HEAD_DIM // 2, 1),
                            pltpu.roll(tc, HEAD_DIM // 2, 1))
        outs.append(tc * cos + partner * sin)
    return jnp.concatenate(outs, axis=1)


def _qkv_kernel(x_ref, m_ref, g_ref, w_ref, cos_ref, sin_ref,
                q_ref, k_ref, v_ref, kf_ref, vf_ref):
    h = _mixer_in(x_ref, m_ref, g_ref)
    cos = cos_ref[...]
    sin = sin_ref[...]
    for c0 in range(0, ATTN_DIM, KV_DIM):
        q = _dot(h, w_ref[:, c0:c0 + KV_DIM].astype(BF16))
        q_ref[:, c0:c0 + KV_DIM] = (_rope_cols(q, cos, sin) * (ATTN_SCALE * LOG2E)).astype(BF16)
    k = _dot(h, w_ref[:, ATTN_DIM:ATTN_DIM + KV_DIM].astype(BF16))
    v = _dot(h, w_ref[:, ATTN_DIM + KV_DIM:].astype(BF16))

    @pl.when(pl.program_id(0) >= N_SAMPLE_TOK // TM)
    def _():
        kf_ref[...] = k
        vf_ref[...] = v

    k_ref[...] = _rope_cols(k, cos, sin).astype(BF16)
    v_ref[...] = v.astype(BF16)


def _rope_tables():
    pos = np.arange(DEC_SEQ)
    row = (pos // GRID_W).astype(np.float32)
    col = (pos % GRID_W).astype(np.float32)
    inv = (ROPE_BASE ** (-np.arange(ROPE_PAIRS_PER_AXIS, dtype=np.float32) / ROPE_PAIRS_PER_AXIS)).astype(np.float32)
    ang = jnp.asarray(np.concatenate([row[:, None] * inv, col[:, None] * inv], axis=-1).astype(np.float32))
    cos, sin = jnp.cos(ang), jnp.sin(ang)
    cos = jnp.concatenate([cos, cos, cos, cos], axis=-1)
    sin = jnp.concatenate([-sin, sin, -sin, sin], axis=-1)
    cos = jnp.concatenate([cos, jnp.ones((TM, LANES), F32)], axis=0)
    sin = jnp.concatenate([sin, jnp.zeros((TM, LANES), F32)], axis=0)
    return cos, sin


def _qkv(x, mods, norm_g, w, li, j, cos, sin):
    per = DEC_SEQ // TM
    ns = N_SAMPLE_TOK // TM
    tab = pl.BlockSpec((TM, LANES), lambda i: (jnp.where(i < ns, i % per, per), 0))
    cache = pl.BlockSpec((TM, KV_DIM), lambda i: (jnp.maximum(i - ns, 0), 0))
    return pl.pallas_call(
        _qkv_kernel,
        out_shape=(
            jax.ShapeDtypeStruct((N_TOK, ATTN_DIM), BF16),
            jax.ShapeDtypeStruct((N_TOK, KV_DIM), BF16),
            jax.ShapeDtypeStruct((N_TOK, KV_DIM), BF16),
            jax.ShapeDtypeStruct((N_PROMPT_TOK, KV_DIM), F32),
            jax.ShapeDtypeStruct((N_PROMPT_TOK, KV_DIM), F32),
        ),
        grid=(N_TOK // TM,),
        in_specs=[_tok_spec(TM, D_MODEL), _mods_spec(li, TM), _norm_spec(li), _resident(w, j), tab, tab],
        out_specs=(_tok_spec(TM, ATTN_DIM), _tok_spec(TM, KV_DIM), _tok_spec(TM, KV_DIM), cache, cache),
        compiler_params=_params(("arbitrary",)),
        name="attn_qkv",
    )(x, mods, norm_g, w, cos, sin)


N_HEAD_SLOTS = 2 * ATTN_KV_HEADS


def _fill_head_slots(x, k_dst, v_dst, row0, is_value):
    dst = v_dst if is_value else k_dst
    pad = 1.0 if is_value else 0.0
    rows = x.shape[0]
    low = lax.broadcasted_iota(jnp.int32, (rows, LANES), 1) < HEAD_DIM
    for g in range(ATTN_KV_HEADS):
        cg, pg = g // 2, g % 2
        xb = x[:, cg * LANES:(cg + 1) * LANES]
        xr = pltpu.roll(xb, HEAD_DIM, 1)
        x_lo, x_hi = (xb, xr) if pg == 0 else (xr, xb)
        dst[2 * g, pl.ds(row0, rows), :] = jnp.where(low, x_lo, pad).astype(BF16)
        dst[2 * g + 1, pl.ds(row0, rows), :] = jnp.where(low, pad, x_hi).astype(BF16)


def _attend_rows(q, key_blocks, sink_ref, o_ref, orow0, s_scr, p_scr):
    rows = q.shape[0]
    rid = lax.broadcasted_iota(jnp.int32, (2 * rows, 1), 0)
    low = lax.broadcasted_iota(jnp.int32, (2 * rows, LANES), 1) < HEAD_DIM
    sinks, maxes = [], []
    for slot in range(N_HEAD_SLOTS):
        g, side = slot // 2, slot % 2
        c0 = 2 * g * LANES
        qs = jnp.concatenate([q[:, c0:c0 + LANES], q[:, c0 + LANES:c0 + 2 * LANES]], axis=0)
        sk = jnp.where(rid < rows, sink_ref[4 * g + side], sink_ref[4 * g + 2 + side]) * LOG2E
        mx = sk
        col = 0
        for k_of, _, bias, n_keys in key_blocks:
            s = _dot_nt(qs, k_of(slot))
            if bias is not None:
                s = s + bias
            s_scr[slot, :, col:col + n_keys] = s
            mx = jnp.maximum(mx, jnp.max(s, axis=-1, keepdims=True))
            col += n_keys
        sinks.append(sk)
        maxes.append(mx)
    for slot in range(N_HEAD_SLOTS):
        p_scr[slot] = jnp.exp2(s_scr[slot] - maxes[slot]).astype(BF16)
    for g in range(ATTN_KV_HEADS):
        res = []
        for side in range(2):
            slot = 2 * g + side
            part = None
            col = 0
            for _, v_of, _, n_keys in key_blocks:
                pv = _dot(p_scr[slot, :, col:col + n_keys], v_of(slot))
                part = pv if part is None else part + pv
                col += n_keys
            den = pltpu.roll(part, HEAD_DIM, 1) + jnp.exp2(sinks[slot] - maxes[slot])
            res.append(part / den)
        out = jnp.where(low, res[0], res[1]).astype(o_ref.dtype)
        c0 = 2 * g * LANES
        o_ref[pl.ds(orow0, rows), c0:c0 + LANES] = out[:rows]
        o_ref[pl.ds(orow0, rows), c0 + LANES:c0 + 2 * LANES] = out[rows:]


def _attn_lat_kernel(sink_ref, q_ref, k_ref, v_ref, ck_ref, cv_ref, o_ref,
                     kl_ref, vl_ref, kc_ref, vc_ref, s_scr, p_scr):
    nb = DEC_SEQ // BLOCK
    zeros = jnp.zeros((BLOCK, LANES), BF16)
    for slot in range(N_HEAD_SLOTS):
        for ref in (kl_ref, vl_ref):
            ref[slot, pl.ds(0, BLOCK), :] = zeros
            ref[slot, pl.ds(DEC_SEQ + BLOCK, BLOCK), :] = zeros
    _fill_head_slots(k_ref[...].astype(F32), kl_ref, vl_ref, BLOCK, False)
    _fill_head_slots(v_ref[...].astype(F32), kl_ref, vl_ref, BLOCK, True)
    _fill_head_slots(ck_ref[...], kc_ref, vc_ref, 0, False)
    _fill_head_slots(cv_ref[...], kc_ref, vc_ref, 0, True)
    n_band = 3 * BLOCK
    qi = lax.broadcasted_iota(jnp.int32, (2 * BLOCK, n_band), 0) % BLOCK
    u = lax.broadcasted_iota(jnp.int32, (2 * BLOCK, n_band), 1)
    in_window = jnp.abs(qi + BLOCK - u) <= WINDOW

    def body(n, carry):
        r0 = pl.multiple_of(n * BLOCK, BLOCK)
        kpos = (n - 1) * BLOCK + u
        ok = in_window & (kpos >= 0) & (kpos < nb * BLOCK)
        bias = jnp.where(ok, 0.0, NEG_BIG)
        band = (lambda s: kl_ref[s, pl.ds(r0, n_band), :], lambda s: vl_ref[s, pl.ds(r0, n_band), :],
                bias, n_band)
        ctx = (lambda s: kc_ref[s], lambda s: vc_ref[s], None, PAST_LEN)
        _attend_rows(q_ref[pl.ds(r0, BLOCK), :], [band, ctx], sink_ref, o_ref, r0, s_scr, p_scr)
        return carry

    lax.fori_loop(0, nb, body, 0)


def _attn_latent(sink, q, k, v, ck, cv, j):
    seq = lambda n: pl.BlockSpec((DEC_SEQ, n), lambda b: (b, 0))
    ctx = pl.BlockSpec((None, None, PAST_LEN, KV_DIM), lambda b: (b, j, 0, 0))
    return pl.pallas_call(
        _attn_lat_kernel,
        out_shape=jax.ShapeDtypeStruct((N_SAMPLE_TOK, ATTN_DIM), BF16),
        grid=(DEC_BATCH,),
        in_specs=[pl.BlockSpec(memory_space=pltpu.SMEM), seq(ATTN_DIM), seq(KV_DIM), seq(KV_DIM), ctx, ctx],
        out_specs=seq(ATTN_DIM),
        scratch_shapes=[
            pltpu.VMEM((N_HEAD_SLOTS, DEC_SEQ + 2 * BLOCK, LANES), BF16),
            pltpu.VMEM((N_HEAD_SLOTS, DEC_SEQ + 2 * BLOCK, LANES), BF16),
            pltpu.VMEM((N_HEAD_SLOTS, PAST_LEN, LANES), BF16),
            pltpu.VMEM((N_HEAD_SLOTS, PAST_LEN, LANES), BF16),
            pltpu.VMEM((N_HEAD_SLOTS, 2 * BLOCK, 3 * BLOCK + PAST_LEN), F32),
            pltpu.VMEM((N_HEAD_SLOTS, 2 * BLOCK, 3 * BLOCK + PAST_LEN), BF16),
        ],
        compiler_params=_params(("arbitrary",)),
        name="attn_latent",
    )(sink, q, k, v, ck, cv)


def _attn_ctx_kernel(sink_ref, q_ref, k_ref, v_ref, o_ref, ks_ref, vs_ref, s_scr, p_scr):
    _fill_head_slots(k_ref[...].astype(F32), ks_ref, vs_ref, 0, False)
    _fill_head_slots(v_ref[...].astype(F32), ks_ref, vs_ref, 0, True)
    blk = (lambda s: ks_ref[s], lambda s: vs_ref[s], None, SEQ)
    _attend_rows(q_ref[...], [blk], sink_ref, o_ref, 0, s_scr, p_scr)


def _attn_context(sink, q, k, v):
    off = N_SAMPLE_TOK // SEQ
    row = lambda n: pl.BlockSpec((SEQ, n), lambda b: (off + b, 0))
    return pl.pallas_call(
        _attn_ctx_kernel,
        out_shape=jax.ShapeDtypeStruct((N_PROMPT_TOK, ATTN_DIM), BF16),
        grid=(BATCH,),
        in_specs=[pl.BlockSpec(memory_space=pltpu.SMEM), row(ATTN_DIM), row(KV_DIM), row(KV_DIM)],
        out_specs=pl.BlockSpec((SEQ, ATTN_DIM), lambda b: (b, 0)),
        scratch_shapes=[pltpu.VMEM((N_HEAD_SLOTS, SEQ, LANES), BF16),
                        pltpu.VMEM((N_HEAD_SLOTS, SEQ, LANES), BF16),
                        pltpu.VMEM((N_HEAD_SLOTS, 2 * SEQ, SEQ), F32),
                        pltpu.VMEM((N_HEAD_SLOTS, 2 * SEQ, SEQ), BF16)],
        compiler_params=_params(("arbitrary",)),
        name="attn_context",
    )(sink, q, k, v)


CONV_COLS = 256


def _conv_kernel(x_ref, m_ref, g_ref, win_ref, cw_ref, cb_ref, wout_ref, o_ref, a_ref):
    i = pl.program_id(0)
    h = _mixer_in(x_ref, m_ref, g_ref)
    tm = h.shape[0]
    seq = jnp.where(i < N_SAMPLE_TOK // tm, DEC_SEQ, SEQ)
    pos = lax.broadcasted_iota(jnp.int32, (tm, CONV_COLS), 0) & (seq - 1)
    first = pos == 0
    last = pos == seq - 1
    cw = cw_ref[...]
    cb = cb_ref[...]
    for c in range(D_MODEL // CONV_COLS):
        cols = slice(c * CONV_COLS, (c + 1) * CONV_COLS)
        part = lambda j: _dot(h, win_ref[:, j * D_MODEL + c * CONV_COLS:j * D_MODEL + (c + 1) * CONV_COLS])
        z = part(1) * part(2)
        zl = jnp.where(first, 0.0, pltpu.roll(z, 1, 0))
        zr = jnp.where(last, 0.0, pltpu.roll(z, tm - 1, 0))
        conv = cb[:, cols] + zl * cw[0:1, cols] + z * cw[1:2, cols] + zr * cw[2:3, cols]
        a_ref[:, cols] = (part(0) * conv).astype(BF16)
    y = _rms(_dot(a_ref[...], wout_ref[...].astype(BF16))) * g_ref[3:4, :]
    o_ref[...] = x_ref[...] + m_ref[5:6, :] * y


def _conv_layer(x, mods, norm_g, w_in, cw, cb, w_out, li, j):
    tm = CONV_TM
    return pl.pallas_call(
        _conv_kernel,
        out_shape=jax.ShapeDtypeStruct((N_TOK, D_MODEL), F32),
        grid=(N_TOK // tm,),
        in_specs=[
            _tok_spec(tm, D_MODEL), _mods_spec(li, tm), _norm_spec(li), _resident(w_in, j),
            pl.BlockSpec((None, 3, D_MODEL), lambda i: (j, 0, 0)),
            pl.BlockSpec((None, 1, D_MODEL), lambda i: (j, 0, 0)),
            _resident(w_out, j),
        ],
        out_specs=_tok_spec(tm, D_MODEL),
        scratch_shapes=[pltpu.VMEM((tm, D_MODEL), BF16)],
        compiler_params=_params(("arbitrary",)),
        name="conv_layer",
    )(x, mods, norm_g, w_in, cw, cb.reshape(-1, 1, D_MODEL), w_out)


def _block_row(x, w, off):
    c = x.shape[0]
    blk = 2 * w
    row = lambda r: jnp.broadcast_to(x[r:r + 1, :], (max(blk, SUBLANES), LANES))
    if blk >= SUBLANES:
        return jnp.concatenate([row(j * blk + off) for j in range(c // blk)], axis=0)
    if blk == 4:
        upper = lax.broadcasted_iota(jnp.int32, (SUBLANES, LANES), 0) < 4
        return jnp.concatenate(
            [jnp.where(upper, row(SUBLANES * v + off), row(SUBLANES * v + 4 + off))
             for v in range(c // SUBLANES)], axis=0)
    odd = (lax.broadcasted_iota(jnp.int32, (c, LANES), 0) & 1) != 0
    if off == 0:
        return jnp.where(odd, pltpu.roll(x, 1, 0), x)
    return jnp.where(odd, x, pltpu.roll(x, c - 1, 0))


def _split3(x):
    hi = x.astype(BF16)
    r = x - hi.astype(F32)
    mid = r.astype(BF16)
    lo = (r - mid.astype(F32)).astype(BF16)
    return jnp.concatenate([hi, mid, lo], axis=0)


def _hgrn_kernel(*refs, seq, li, hp, has_init, emit_state):
    lb_ref, lv_ref, zq_ref, zi_ref, zf_ref, zb_ref, zg_ref, ng_ref = refs[:8]
    refs = refs[8:]
    s0_ref = None
    if has_init:
        s0_ref, refs = refs[0], refs[1:]
    o_ref, refs = refs[0], refs[1:]
    sout_ref = None
    if emit_state:
        sout_ref, refs = refs[0], refs[1:]
    acc_ref, qf_ref, qb_ref, kv_ref, dec_ref, stf_ref, stb_ref = refs
    c = GLA_CHUNK
    nc = seq // c

    def lower_bound(x):
        e = jnp.exp(x - jnp.max(x, axis=0, keepdims=True))
        pr = e / jnp.sum(e, axis=0, keepdims=True)
        return jnp.sum(pr[1:li + 1], axis=0, keepdims=True)

    lbf, lbb = lower_bound(lb_ref[0]), lower_bound(lb_ref[1])
    ri = lax.broadcasted_iota(jnp.int32, (c, c), 0)
    ci = lax.broadcasted_iota(jnp.int32, (c, c), 1)
    tril = jnp.where(ci <= ri, 1.0, 0.0).astype(BF16)
    triu = jnp.where(ci >= ri, 1.0, 0.0).astype(BF16)
    tril3 = jnp.concatenate([tril, tril, tril], axis=1)
    triu3 = jnp.concatenate([triu, triu, triu], axis=1)
    lv = lv_ref[...]
    rowi = lax.broadcasted_iota(jnp.int32, (c, LANES), 0)

    def chunk_local(n, hd):
        rows = pl.ds(pl.multiple_of(n * c, c), c)
        cols = slice(hd * LANES, (hd + 1) * LANES)
        q = _silu(zq_ref[rows, cols]) * (HGRN_DK ** -0.5)
        zi = zi_ref[rows, cols]
        v = zi.astype(BF16)
        ff = lbf[:, cols] + (1.0 - lbf[:, cols]) * jax.nn.sigmoid(zf_ref[rows, cols])
        fb = lbb[:, cols] + (1.0 - lbb[:, cols]) * jax.nn.sigmoid(zb_ref[rows, cols])
        kf = 1.0 - ff
        kb = 1.0 - fb
        pre = _dot(tril3, _split3(jnp.log2(ff)))
        suf = _dot(triu3, _split3(jnp.log2(fb)))
        attn = jnp.where(lv == 0, _dot_nt(q.astype(BF16), (kf + kb).astype(BF16)), 0.0)
        w, level = 1, 1
        while w < c:
            d_f = pre - _block_row(pre, w, w - 1)
            d_b = suf - _block_row(suf, w, w)
            xq = jnp.minimum(d_f, d_b)
            xk = xq - (d_f + d_b)
            second = (rowi & w) != 0
            qh = (q * jnp.exp2(xq)).astype(BF16)
            kh = (jnp.where(second, kb, kf) * jnp.exp2(xk)).astype(BF16)
            attn = jnp.where(lv == level, _dot_nt(qh, kh), attn)
            w *= 2
            level += 1
        acc_ref[hd, rows, :] = _dot(attn.astype(BF16), v)
        tot_f = pre[c - 1:c, :]
        tot_b = suf[0:1, :]
        qf_ref[hd, rows, :] = (q * jnp.exp2(pre)).astype(BF16)
        qb_ref[hd, rows, :] = (q * jnp.exp2(suf)).astype(BF16)
        kt = jnp.concatenate([(kf * jnp.exp2(tot_f - pre)).astype(BF16),
                              (kb * jnp.exp2(tot_b - suf)).astype(BF16)], axis=1)
        kv_ref[hd, n] = _dot(zi.T.astype(BF16), kt)
        dec_ref[hd, n] = jnp.exp2(jnp.concatenate([tot_f, tot_b], axis=1))

    def local_body(n, carry):
        for hd in range(hp):
            chunk_local(n, hd)
        return carry

    lax.fori_loop(0, nc, local_body, 0, unroll=min(nc, GLA_ROWS_IN_FLIGHT // (c * hp)))

    for hd in range(hp):
        if has_init:
            stf_ref[hd] = s0_ref[0, hd].T
            stb_ref[hd] = s0_ref[1, hd].T
        else:
            stf_ref[hd] = jnp.zeros((HGRN_DV, HGRN_DK), F32)
            stb_ref[hd] = jnp.zeros((HGRN_DV, HGRN_DK), F32)

    def carry_state(i, carry):
        n, m = i, nc - 1 - i
        rn = pl.ds(pl.multiple_of(n * c, c), c)
        rm = pl.ds(pl.multiple_of(m * c, c), c)
        for hd in range(hp):
            stf = stf_ref[hd]
            stb = stb_ref[hd]
            acc_ref[hd, rn, :] += _dot_nt(qf_ref[hd, rn, :], stf.astype(BF16))
            acc_ref[hd, rm, :] += _dot_nt(qb_ref[hd, rm, :], stb.astype(BF16))
            stf_ref[hd] = stf * dec_ref[hd, n][:, :HGRN_DK] + kv_ref[hd, n][:, :HGRN_DK]
            stb_ref[hd] = stb * dec_ref[hd, m][:, HGRN_DK:] + kv_ref[hd, m][:, HGRN_DK:]
        return carry

    lax.fori_loop(0, nc, carry_state, 0, unroll=True)
    for hd in range(hp):
        cols = slice(hd * LANES, (hd + 1) * LANES)
        o_ref[:, cols] = (_rms(acc_ref[hd]) * ng_ref[...] * _silu(zg_ref[:, cols])).astype(o_ref.dtype)
        if emit_state:
            sout_ref[0, hd] = stf_ref[hd].T
            sout_ref[1, hd] = stb_ref[hd].T


def _pair_levels():
    t = np.arange(GLA_CHUNK)
    x = t[:, None] ^ t[None, :]
    return jnp.asarray(np.where(x > 0, np.floor(np.log2(np.maximum(x, 1))) + 1, 0).astype(np.int32))


def _hgrn_core(z, lb, ng, s0, *, j, batch, seq, row_off, li, emit_state):
    has_init = s0 is not None
    blk0 = row_off // seq
    fd = HGRN_FD // LANES
    dm = D_MODEL // LANES
    c = GLA_CHUNK
    hp = max(1, GLA_ROWS_IN_FLIGHT // seq)
    width = hp * LANES

    def col(c0):
        return pl.BlockSpec((seq, width), lambda b, h: (blk0 + b, c0 // hp + h))

    in_specs = [
        pl.BlockSpec((2, DEPTH, width), lambda b, h: (0, 0, h)),
        pl.BlockSpec((c, c), lambda b, h: (0, 0)),
        col(0), col(fd), col(fd + dm), col(2 * fd + dm), col(3 * fd + dm),
        pl.BlockSpec((None, 1, HGRN_DV), lambda b, h: (j, 0, 0)),
    ]
    args = [lb, _pair_levels(), z, z, z, z, z, ng.reshape(-1, 1, HGRN_DV)]
    if has_init:
        in_specs.append(pl.BlockSpec((None, None, 2, hp, HGRN_DK, HGRN_DV),
                                     lambda b, h: (b, j, 0, h, 0, 0)))
        args.append(s0)
    out_shape = [jax.ShapeDtypeStruct((batch * seq, D_MODEL), BF16)]
    out_specs = [pl.BlockSpec((seq, width), lambda b, h: (b, h))]
    if emit_state:
        out_shape.append(jax.ShapeDtypeStruct((batch, 2, HGRN_HEADS, HGRN_DK, HGRN_DV), F32))
        out_specs.append(pl.BlockSpec((None, 2, hp, HGRN_DK, HGRN_DV), lambda b, h: (b, 0, h, 0, 0)))
    return pl.pallas_call(
        functools.partial(_hgrn_kernel, seq=seq, li=li, hp=hp, has_init=has_init, emit_state=emit_state),
        out_shape=tuple(out_shape),
        grid=(batch, HGRN_HEADS // hp),
        in_specs=in_specs,
        out_specs=tuple(out_specs),
        scratch_shapes=[
            pltpu.VMEM((hp, seq, HGRN_DV), F32),
            pltpu.VMEM((hp, seq, HGRN_DK), BF16),
            pltpu.VMEM((hp, seq, HGRN_DK), BF16),
            pltpu.VMEM((hp, seq // c, HGRN_DV, 2 * HGRN_DK), F32),
            pltpu.VMEM((hp, seq // c, 1, 2 * HGRN_DK), F32),
            pltpu.VMEM((hp, HGRN_DV, HGRN_DK), F32),
            pltpu.VMEM((hp, HGRN_DV, HGRN_DK), F32),
        ],
        compiler_params=_params(("arbitrary", "arbitrary")),
        name=f"hgrn_core_{seq}",
    )(*args)


def kernel(x_prompt, x_sample, cache_k, cache_v, state_hgrn, c, c_ctx, w_ada, b_ada, norm_g, w_ffn_in, w_ffn_out, w_attn_qkv, w_attn_o, attn_sink, w_conv_in, conv_w, conv_b, w_conv_out, w_hgrn_in, hgrn_lb, hgrn_norm_g, w_hgrn_out):
    cond = jnp.concatenate([c, c_ctx[None, :],
                            jnp.zeros((N_COND - DEC_BATCH - 1, D_MODEL), F32)], axis=0)
    mods = _ada_table(cond, w_ada, b_ada)
    cos, sin = _rope_tables()
    ck = cache_k.reshape(DEC_BATCH, -1, PAST_LEN, KV_DIM)
    cv = cache_v.reshape(DEC_BATCH, -1, PAST_LEN, KV_DIM)
    bf = lambda w: w.astype(BF16)
    w_conv_in, w_hgrn_in = bf(w_conv_in), bf(w_hgrn_in)
    xs = (x_sample.reshape(N_SAMPLE_TOK, D_MODEL), x_prompt.reshape(N_PROMPT_TOK, D_MODEL))
    ffn_w = (bf(w_ffn_in[0, 0]), bf(w_ffn_out[0, 0]))
    new_k, new_v, new_s = [], [], []
    for li in range(DEPTH):
        kind, j = li % N_MIXERS, li // N_MIXERS
        mixer = None
        (x,), ffn_w = _ffn(xs, mods, norm_g, ffn_w, (w_ffn_in, w_ffn_out, li, 1), li, 0)
        if kind == 0:
            q, k, v, kf, vf = _qkv(x, mods, norm_g, w_attn_qkv, li, j, cos, sin)
            o_s = _attn_latent(attn_sink[j], q, k, v, ck, cv, j)
            o_p = _attn_context(attn_sink[j], q, k, v)
            mixer = (o_s, o_p, w_attn_o, j)
            new_k.append(kf.reshape(BATCH, SEQ, ATTN_KV_HEADS, HEAD_DIM))
            new_v.append(vf.reshape(BATCH, SEQ, ATTN_KV_HEADS, HEAD_DIM))
        elif kind == 1:
            x = _conv_layer(x, mods, norm_g, w_conv_in, conv_w, conv_b, w_conv_out, li, j)
        else:
            z = _proj(x, mods, norm_g, w_hgrn_in, li, j, F32)
            o_s = _hgrn_core(z, hgrn_lb, hgrn_norm_g, state_hgrn, j=j, batch=DEC_BATCH,
                             seq=DEC_SEQ, row_off=0, li=li, emit_state=False)[0]
            o_p, st = _hgrn_core(z, hgrn_lb, hgrn_norm_g, None, j=j, batch=BATCH, seq=SEQ,
                                 row_off=N_SAMPLE_TOK, li=li, emit_state=True)
            mixer = (o_s, o_p, w_hgrn_out, j)
            new_s.append(st)
        last = li == DEPTH - 1
        nxt = None if last else (w_ffn_in, w_ffn_out, li + 1, 0)
        xs, ffn_w = _ffn((x,), mods, norm_g, ffn_w, nxt, li, 1, split_out=last, mixer=mixer)
    y_sample = xs[0].reshape(DEC_BATCH, DEC_SEQ, D_MODEL)
    y_prompt = xs[1].reshape(BATCH, SEQ, D_MODEL)
    return (y_prompt, y_sample, jnp.stack(new_k, axis=1), jnp.stack(new_v, axis=1),
            jnp.stack(new_s, axis=1))
```

```python
import functools
import math

import jax
import jax.numpy as jnp
import numpy as np
from jax import lax
from jax.experimental import pallas as pl
from jax.experimental.pallas import tpu as pltpu

D_MODEL = 1024
BATCH = 16
SEQ = 256
DEPTH = 4
DEC_BATCH = 8
DEC_SEQ = 1024
PAST_LEN = 256
GRID_W = 64
N_MIXERS = 3
N_MOD = 9
N_NORMS = 6
D_FF = 2816
FFN_RES = 0.5
ATTN_HEADS = 16
ATTN_KV_HEADS = 4
HEAD_DIM = 64
ATTN_DIM = ATTN_HEADS * HEAD_DIM
KV_DIM = ATTN_KV_HEADS * HEAD_DIM
BLOCK = 128
WINDOW = 128
ATTN_SCALE = HEAD_DIM ** -0.5
LOG2E = math.log2(math.e)
ROPE_BASE = 10000.0
ROPE_PAIRS_PER_AXIS = HEAD_DIM // 4
HGRN_HEADS = 8
HGRN_DK = 128
HGRN_DV = D_MODEL // HGRN_HEADS
HGRN_FD = HGRN_HEADS * HGRN_DK
EPS = 1e-6

N_SAMPLE_TOK = DEC_BATCH * DEC_SEQ
N_PROMPT_TOK = BATCH * SEQ
N_TOK = N_SAMPLE_TOK + N_PROMPT_TOK
N_COND = 16
CTX_COND = DEC_BATCH

LANES = 128
SUBLANES = 8
TM = 512
CONV_TM = 1024
FFN_SPLITS = (0, 1024, 2048, D_FF)
GLA_CHUNK = 128
GLA_ROWS_IN_FLIGHT = 1024
NEG_BIG = -1e30
VMEM_LIMIT = 56 * 1024 * 1024

F32 = jnp.float32
BF16 = jnp.bfloat16


def _rms(x):
    return x * lax.rsqrt(jnp.mean(x * x, axis=-1, keepdims=True) + EPS)


def _silu(x):
    return x * jax.nn.sigmoid(x)


def _dot(a, b):
    return jnp.dot(a, b, preferred_element_type=F32)


def _dot_nt(a, b):
    return lax.dot_general(a, b, (((1,), (1,)), ((), ())), preferred_element_type=F32)


def _params(sem):
    return pltpu.CompilerParams(dimension_semantics=sem, vmem_limit_bytes=VMEM_LIMIT)


def _resident(arr, *lead):
    shape = (None,) * len(lead) + arr.shape[len(lead):]
    idx = tuple(lead) + (0,) * (arr.ndim - len(lead))
    return pl.BlockSpec(shape, lambda *_: idx, pipeline_mode=pl.Buffered(1))


def _mods_spec(li, tm):
    per = DEC_SEQ // tm
    return pl.BlockSpec((None, None, N_MOD, D_MODEL),
                        lambda i: (li, jnp.minimum(i // per, CTX_COND), 0, 0))


def _norm_spec(li):
    return pl.BlockSpec((None, N_NORMS, D_MODEL), lambda i: (li, 0, 0))


def _tok_spec(tm, n):
    return pl.BlockSpec((tm, n), lambda i: (i, 0))


def _split_specs(tm, n):
    ns = N_SAMPLE_TOK // tm
    return [pl.BlockSpec((tm, n), lambda i: (jnp.minimum(i, ns - 1), 0)),
            pl.BlockSpec((tm, n), lambda i: (jnp.maximum(i - ns, 0), 0))]


def _load_split(s_ref, p_ref, tm):
    return jnp.where(pl.program_id(0) < N_SAMPLE_TOK // tm, s_ref[...], p_ref[...])


def _store_split(s_ref, p_ref, val, tm):
    ns = N_SAMPLE_TOK // tm
    i = pl.program_id(0)

    @pl.when(i < ns)
    def _():
        s_ref[...] = val

    @pl.when(i >= ns)
    def _():
        p_ref[...] = val


def _ada_kernel(c_ref, w_ref, b_ref, o_ref):
    s = _silu(c_ref[...]).astype(BF16)
    o_ref[...] = _dot(s, w_ref[...].astype(BF16)) + b_ref[...]


def _ada_table(cond, w_ada, b_ada):
    tn = 1024
    n = N_MOD * D_MODEL
    out = pl.pallas_call(
        _ada_kernel,
        out_shape=jax.ShapeDtypeStruct((DEPTH, N_COND, n), F32),
        grid=(DEPTH, n // tn),
        in_specs=[
            pl.BlockSpec((N_COND, D_MODEL), lambda l, j: (0, 0)),
            pl.BlockSpec((None, D_MODEL, tn), lambda l, j: (l, 0, j)),
            pl.BlockSpec((None, 1, tn), lambda l, j: (l, 0, j)),
        ],
        out_specs=pl.BlockSpec((None, N_COND, tn), lambda l, j: (l, 0, j)),
        compiler_params=_params(("arbitrary", "arbitrary")),
        name="ada_table",
    )(cond, w_ada, b_ada.reshape(DEPTH, 1, n))
    return out.reshape(DEPTH, N_COND, N_MOD, D_MODEL)


NEXT_IN_ROWS = 64
NEXT_OUT_ROWS = 128


def _ffn_kernel(*refs, k0, g0, split_in, split_out, has_mixer, has_next):
    n_in = 2 if split_in else 1
    x_refs, refs = refs[:n_in], refs[n_in:]
    mix_refs = ()
    if has_mixer:
        mix_refs, refs = refs[:3], refs[3:]
    m_ref, g_ref, win_ref, wout_ref = refs[:4]
    refs = refs[4:]
    if has_next:
        nin_ref, nout_ref = refs[:2]
        refs[-2][...] = nin_ref[...].astype(BF16)
        refs[-1][...] = nout_ref[...].astype(BF16)
        refs = refs[2:-2]
    o_refs = refs
    x = _load_split(*x_refs, TM) if split_in else x_refs[0][...]
    m = m_ref[...]
    g = g_ref[...]
    if has_mixer:
        as_ref, ap_ref, wo_ref = mix_refs
        mixed = _dot(_load_split(as_ref, ap_ref, TM), wo_ref[...].astype(BF16))
        x = x + m[5:6] * (_rms(mixed) * g[3:4])
    h = _rms(x) * g[g0:g0 + 1]
    h = (h * (1.0 + m[k0 + 1:k0 + 2]) + m[k0:k0 + 1]).astype(BF16)
    acc = None
    for lo, hi in zip(FFN_SPLITS[:-1], FFN_SPLITS[1:]):
        a = _dot(h, win_ref[:, lo:hi])
        b = _dot(h, win_ref[:, D_FF + lo:D_FF + hi])
        u = (_silu(a) * b).astype(BF16)
        part = _dot(u, wout_ref[lo:hi, :])
        acc = part if acc is None else acc + part
    y = _rms(acc) * g[g0 + 1:g0 + 2]
    res = x + FFN_RES * m[k0 + 2:k0 + 3] * y
    if split_out:
        _store_split(*o_refs, res, TM)
    else:
        o_refs[0][...] = res


def _ffn(xs, mods, norm_g, w_bf, nxt, li, which, split_out=False, mixer=None):
    k0, g0 = (0, 0) if which == 0 else (6, 4)
    split_in = len(xs) == 2
    steps = N_TOK // TM
    x_specs = _split_specs(TM, D_MODEL) if split_in else [_tok_spec(TM, D_MODEL)]
    mix_specs, mix_args = [], []
    if mixer is not None:
        a_s, a_p, w_o, j = mixer
        mix_specs = _split_specs(TM, D_MODEL) + [_resident(w_o, j)]
        mix_args = [a_s, a_p, w_o]
    if split_out:
        out_shape = [jax.ShapeDtypeStruct((N_SAMPLE_TOK, D_MODEL), F32),
                     jax.ShapeDtypeStruct((N_PROMPT_TOK, D_MODEL), F32)]
        out_specs = _split_specs(TM, D_MODEL)
    else:
        out_shape = [jax.ShapeDtypeStruct((N_TOK, D_MODEL), F32)]
        out_specs = [_tok_spec(TM, D_MODEL)]
    nxt_specs, nxt_args = [], []
    if nxt is not None:
        w_in_all, w_out_all, li2, which2 = nxt
        last_in = D_MODEL // NEXT_IN_ROWS - 1
        last_out = D_FF // NEXT_OUT_ROWS - 1
        assert last_in < steps and last_out < steps
        nxt_specs = [
            pl.BlockSpec((None, None, NEXT_IN_ROWS, 2 * D_FF),
                         lambda i: (li2, which2, jnp.minimum(i, last_in), 0)),
            pl.BlockSpec((None, None, NEXT_OUT_ROWS, D_MODEL),
                         lambda i: (li2, which2, jnp.minimum(i, last_out), 0)),
        ]
        nxt_args = [w_in_all, w_out_all]
        out_shape += [jax.ShapeDtypeStruct((D_MODEL, 2 * D_FF), BF16),
                      jax.ShapeDtypeStruct((D_FF, D_MODEL), BF16)]
        out_specs += [pl.BlockSpec((NEXT_IN_ROWS, 2 * D_FF), lambda i: (jnp.minimum(i, last_in), 0)),
                      pl.BlockSpec((NEXT_OUT_ROWS, D_MODEL), lambda i: (jnp.minimum(i, last_out), 0))]
    outs = pl.pallas_call(
        functools.partial(_ffn_kernel, k0=k0, g0=g0, split_in=split_in, split_out=split_out,
                          has_mixer=mixer is not None, has_next=nxt is not None),
        out_shape=tuple(out_shape),
        grid=(steps,),
        in_specs=x_specs + mix_specs + [
            _mods_spec(li, TM), _norm_spec(li), _resident(w_bf[0]), _resident(w_bf[1])] + nxt_specs,
        out_specs=tuple(out_specs),
        compiler_params=_params(("arbitrary",)),
        name=f"ffn{which}",
    )(*xs, *mix_args, mods, norm_g, *w_bf, *nxt_args)
    n_x = 2 if split_out else 1
    return tuple(outs[:n_x]), tuple(outs[n_x:])


def _mixer_in(x_ref, m_ref, g_ref):
    m = m_ref[...]
    h = _rms(x_ref[...]) * g_ref[2:3, :]
    return (h * (1.0 + m[4:5]) + m[3:4]).astype(BF16)


PROJ_COLS = 512


def _proj_kernel(x_ref, m_ref, g_ref, w_ref, o_ref):
    h = _mixer_in(x_ref, m_ref, g_ref)
    for c0 in range(0, o_ref.shape[1], PROJ_COLS):
        o_ref[:, c0:c0 + PROJ_COLS] = _dot(h, w_ref[:, c0:c0 + PROJ_COLS].astype(BF16)).astype(o_ref.dtype)


def _proj(x, mods, norm_g, w, li, j, out_dtype):
    n = w.shape[-1]
    return pl.pallas_call(
        _proj_kernel,
        out_shape=jax.ShapeDtypeStruct((N_TOK, n), out_dtype),
        grid=(N_TOK // TM,),
        in_specs=[_tok_spec(TM, D_MODEL), _mods_spec(li, TM), _norm_spec(li), _resident(w, j)],
        out_specs=_tok_spec(TM, n),
        compiler_params=_params(("arbitrary",)),
        name="mixer_proj",
    )(x, mods, norm_g, w)


def _rope_cols(t, cos, sin):
    lane = lax.broadcasted_iota(jnp.int32, (t.shape[0], LANES), 1)
    low = (lane % HEAD_DIM) < (HEAD_DIM // 2)
    outs = []
    for c in range(t.shape[1] // LANES):
        tc = t[:, c * LANES:(c + 1) * LANES]
        partner = jnp.where(low, pltpu.roll(tc, LANES - HEAD_DIM // 2, 1),
                            pltpu.roll(tc, HEAD_DIM // 2, 1))
        outs.append(tc * cos + partner * sin)
    return jnp.concatenate(outs, axis=1)


def _qkv_kernel(x_ref, m_ref, g_ref, w_ref, cos_ref, sin_ref,
                q_ref, k_ref, v_ref, kf_ref, vf_ref):
    h = _mixer_in(x_ref, m_ref, g_ref)
    cos = cos_ref[...]
    sin = sin_ref[...]
    for c0 in range(0, ATTN_DIM, KV_DIM):
        q = _dot(h, w_ref[:, c0:c0 + KV_DIM].astype(BF16))
        q_ref[:, c0:c0 + KV_DIM] = (_rope_cols(q, cos, sin) * (ATTN_SCALE * LOG2E)).astype(BF16)
    k = _dot(h, w_ref[:, ATTN_DIM:ATTN_DIM + KV_DIM].astype(BF16))
    v = _dot(h, w_ref[:, ATTN_DIM + KV_DIM:].astype(BF16))

    @pl.when(pl.program_id(0) >= N_SAMPLE_TOK // TM)
    def _():
        kf_ref[...] = k
        vf_ref[...] = v

    k_ref[...] = _rope_cols(k, cos, sin).astype(BF16)
    v_ref[...] = v.astype(BF16)


def _rope_tables():
    pos = np.arange(DEC_SEQ)
    row = (pos // GRID_W).astype(np.float32)
    col = (pos % GRID_W).astype(np.float32)
    inv = (ROPE_BASE ** (-np.arange(ROPE_PAIRS_PER_AXIS, dtype=np.float32) / ROPE_PAIRS_PER_AXIS)).astype(np.float32)
    ang = jnp.asarray(np.concatenate([row[:, None] * inv, col[:, None] * inv], axis=-1).astype(np.float32))
    cos, sin = jnp.cos(ang), jnp.sin(ang)
    cos = jnp.concatenate([cos, cos, cos, cos], axis=-1)
    sin = jnp.concatenate([-sin, sin, -sin, sin], axis=-1)
    cos = jnp.concatenate([cos, jnp.ones((TM, LANES), F32)], axis=0)
    sin = jnp.concatenate([sin, jnp.zeros((TM, LANES), F32)], axis=0)
    return cos, sin


def _qkv(x, mods, norm_g, w, li, j, cos, sin):
    per = DEC_SEQ // TM
    ns = N_SAMPLE_TOK // TM
    tab = pl.BlockSpec((TM, LANES), lambda i: (jnp.where(i < ns, i % per, per), 0))
    cache = pl.BlockSpec((TM, KV_DIM), lambda i: (jnp.maximum(i - ns, 0), 0))
    return pl.pallas_call(
        _qkv_kernel,
        out_shape=(
            jax.ShapeDtypeStruct((N_TOK, ATTN_DIM), BF16),
            jax.ShapeDtypeStruct((N_TOK, KV_DIM), BF16),
            jax.ShapeDtypeStruct((N_TOK, KV_DIM), BF16),
            jax.ShapeDtypeStruct((N_PROMPT_TOK, KV_DIM), F32),
            jax.ShapeDtypeStruct((N_PROMPT_TOK, KV_DIM), F32),
        ),
        grid=(N_TOK // TM,),
        in_specs=[_tok_spec(TM, D_MODEL), _mods_spec(li, TM), _norm_spec(li), _resident(w, j), tab, tab],
        out_specs=(_tok_spec(TM, ATTN_DIM), _tok_spec(TM, KV_DIM), _tok_spec(TM, KV_DIM), cache, cache),
        compiler_params=_params(("arbitrary",)),
        name="attn_qkv",
    )(x, mods, norm_g, w, cos, sin)


N_HEAD_SLOTS = 2 * ATTN_KV_HEADS


def _fill_head_slots(x, k_dst, v_dst, row0, is_value):
    dst = v_dst if is_value else k_dst
    pad = 1.0 if is_value else 0.0
    rows = x.shape[0]
    low = lax.broadcasted_iota(jnp.int32, (rows, LANES), 1) < HEAD_DIM
    for g in range(ATTN_KV_HEADS):
        cg, pg = g // 2, g % 2
        xb = x[:, cg * LANES:(cg + 1) * LANES]
        xr = pltpu.roll(xb, HEAD_DIM, 1)
        x_lo, x_hi = (xb, xr) if pg == 0 else (xr, xb)
        dst[2 * g, pl.ds(row0, rows), :] = jnp.where(low, x_lo, pad).astype(BF16)
        dst[2 * g + 1, pl.ds(row0, rows), :] = jnp.where(low, pad, x_hi).astype(BF16)


def _attend_rows(q, key_blocks, sink_ref, o_ref, orow0, s_scr, p_scr):
    rows = q.shape[0]
    rid = lax.broadcasted_iota(jnp.int32, (2 * rows, 1), 0)
    low = lax.broadcasted_iota(jnp.int32, (2 * rows, LANES), 1) < HEAD_DIM
    sinks, maxes = [], []
    for slot in range(N_HEAD_SLOTS):
        g, side = slot // 2, slot % 2
        c0 = 2 * g * LANES
        qs = jnp.concatenate([q[:, c0:c0 + LANES], q[:, c0 + LANES:c0 + 2 * LANES]], axis=0)
        sk = jnp.where(rid < rows, sink_ref[4 * g + side], sink_ref[4 * g + 2 + side]) * LOG2E
        mx = sk
        col = 0
        for k_of, _, bias, n_keys in key_blocks:
            s = _dot_nt(qs, k_of(slot))
            if bias is not None:
                s = s + bias
            s_scr[slot, :, col:col + n_keys] = s
            mx = jnp.maximum(mx, jnp.max(s, axis=-1, keepdims=True))
            col += n_keys
        sinks.append(sk)
        maxes.append(mx)
    for slot in range(N_HEAD_SLOTS):
        p_scr[slot] = jnp.exp2(s_scr[slot] - maxes[slot]).astype(BF16)
    for g in range(ATTN_KV_HEADS):
        res = []
        for side in range(2):
            slot = 2 * g + side
            part = None
            col = 0
            for _, v_of, _, n_keys in key_blocks:
                pv = _dot(p_scr[slot, :, col:col + n_keys], v_of(slot))
                part = pv if part is None else part + pv
                col += n_keys
            den = pltpu.roll(part, HEAD_DIM, 1) + jnp.exp2(sinks[slot] - maxes[slot])
            res.append(part / den)
        out = jnp.where(low, res[0], res[1]).astype(o_ref.dtype)
        c0 = 2 * g * LANES
        o_ref[pl.ds(orow0, rows), c0:c0 + LANES] = out[:rows]
        o_ref[pl.ds(orow0, rows), c0 + LANES:c0 + 2 * LANES] = out[rows:]


def _attn_lat_kernel(sink_ref, q_ref, k_ref, v_ref, ck_ref, cv_ref, o_ref,
                     kl_ref, vl_ref, kc_ref, vc_ref, s_scr, p_scr):
    nb = DEC_SEQ // BLOCK
    zeros = jnp.zeros((BLOCK, LANES), BF16)
    for slot in range(N_HEAD_SLOTS):
        for ref in (kl_ref, vl_ref):
            ref[slot, pl.ds(0, BLOCK), :] = zeros
            ref[slot, pl.ds(DEC_SEQ + BLOCK, BLOCK), :] = zeros
    _fill_head_slots(k_ref[...].astype(F32), kl_ref, vl_ref, BLOCK, False)
    _fill_head_slots(v_ref[...].astype(F32), kl_ref, vl_ref, BLOCK, True)
    _fill_head_slots(ck_ref[...], kc_ref, vc_ref, 0, False)
    _fill_head_slots(cv_ref[...], kc_ref, vc_ref, 0, True)
    n_band = 3 * BLOCK
    qi = lax.broadcasted_iota(jnp.int32, (2 * BLOCK, n_band), 0) % BLOCK
    u = lax.broadcasted_iota(jnp.int32, (2 * BLOCK, n_band), 1)
    in_window = jnp.abs(qi + BLOCK - u) <= WINDOW

    def body(n, carry):
        r0 = pl.multiple_of(n * BLOCK, BLOCK)
        kpos = (n - 1) * BLOCK + u
        ok = in_window & (kpos >= 0) & (kpos < nb * BLOCK)
        bias = jnp.where(ok, 0.0, NEG_BIG)
        band = (lambda s: kl_ref[s, pl.ds(r0, n_band), :], lambda s: vl_ref[s, pl.ds(r0, n_band), :],
                bias, n_band)
        ctx = (lambda s: kc_ref[s], lambda s: vc_ref[s], None, PAST_LEN)
        _attend_rows(q_ref[pl.ds(r0, BLOCK), :], [band, ctx], sink_ref, o_ref, r0, s_scr, p_scr)
        return carry

    lax.fori_loop(0, nb, body, 0)


def _attn_latent(sink, q, k, v, ck, cv, j):
    seq = lambda n: pl.BlockSpec((DEC_SEQ, n), lambda b: (b, 0))
    ctx = pl.BlockSpec((None, None, PAST_LEN, KV_DIM), lambda b: (b, j, 0, 0))
    return pl.pallas_call(
        _attn_lat_kernel,
        out_shape=jax.ShapeDtypeStruct((N_SAMPLE_TOK, ATTN_DIM), BF16),
        grid=(DEC_BATCH,),
        in_specs=[pl.BlockSpec(memory_space=pltpu.SMEM), seq(ATTN_DIM), seq(KV_DIM), seq(KV_DIM), ctx, ctx],
        out_specs=seq(ATTN_DIM),
        scratch_shapes=[
            pltpu.VMEM((N_HEAD_SLOTS, DEC_SEQ + 2 * BLOCK, LANES), BF16),
            pltpu.VMEM((N_HEAD_SLOTS, DEC_SEQ + 2 * BLOCK, LANES), BF16),
            pltpu.VMEM((N_HEAD_SLOTS, PAST_LEN, LANES), BF16),
            pltpu.VMEM((N_HEAD_SLOTS, PAST_LEN, LANES), BF16),
            pltpu.VMEM((N_HEAD_SLOTS, 2 * BLOCK, 3 * BLOCK + PAST_LEN), F32),
            pltpu.VMEM((N_HEAD_SLOTS, 2 * BLOCK, 3 * BLOCK + PAST_LEN), BF16),
        ],
        compiler_params=_params(("arbitrary",)),
        name="attn_latent",
    )(sink, q, k, v, ck, cv)


def _attn_ctx_kernel(sink_ref, q_ref, k_ref, v_ref, o_ref, ks_ref, vs_ref, s_scr, p_scr):
    _fill_head_slots(k_ref[...].astype(F32), ks_ref, vs_ref, 0, False)
    _fill_head_slots(v_ref[...].astype(F32), ks_ref, vs_ref, 0, True)
    blk = (lambda s: ks_ref[s], lambda s: vs_ref[s], None, SEQ)
    _attend_rows(q_ref[...], [blk], sink_ref, o_ref, 0, s_scr, p_scr)


def _attn_context(sink, q, k, v):
    off = N_SAMPLE_TOK // SEQ
    row = lambda n: pl.BlockSpec((SEQ, n), lambda b: (off + b, 0))
    return pl.pallas_call(
        _attn_ctx_kernel,
        out_shape=jax.ShapeDtypeStruct((N_PROMPT_TOK, ATTN_DIM), BF16),
        grid=(BATCH,),
        in_specs=[pl.BlockSpec(memory_space=pltpu.SMEM), row(ATTN_DIM), row(KV_DIM), row(KV_DIM)],
        out_specs=pl.BlockSpec((SEQ, ATTN_DIM), lambda b: (b, 0)),
        scratch_shapes=[pltpu.VMEM((N_HEAD_SLOTS, SEQ, LANES), BF16),
                        pltpu.VMEM((N_HEAD_SLOTS, SEQ, LANES), BF16),
                        pltpu.VMEM((N_HEAD_SLOTS, 2 * SEQ, SEQ), F32),
                        pltpu.VMEM((N_HEAD_SLOTS, 2 * SEQ, SEQ), BF16)],
        compiler_params=_params(("arbitrary",)),
        name="attn_context",
    )(sink, q, k, v)


CONV_COLS = 256


def _conv_kernel(x_ref, m_ref, g_ref, win_ref, cw_ref, cb_ref, wout_ref, o_ref, a_ref):
    i = pl.program_id(0)
    h = _mixer_in(x_ref, m_ref, g_ref)
    tm = h.shape[0]
    seq = jnp.where(i < N_SAMPLE_TOK // tm, DEC_SEQ, SEQ)
    pos = lax.broadcasted_iota(jnp.int32, (tm, CONV_COLS), 0) & (seq - 1)
    first = pos == 0
    last = pos == seq - 1
    cw = cw_ref[...]
    cb = cb_ref[...]
    for c in range(D_MODEL // CONV_COLS):
        cols = slice(c * CONV_COLS, (c + 1) * CONV_COLS)
        part = lambda j: _dot(
            h, win_ref[:, j * D_MODEL + c * CONV_COLS:j * D_MODEL + (c + 1) * CONV_COLS].astype(BF16))
        z = part(1) * part(2)
        zl = jnp.where(first, 0.0, pltpu.roll(z, 1, 0))
        zr = jnp.where(last, 0.0, pltpu.roll(z, tm - 1, 0))
        conv = cb[:, cols] + zl * cw[0:1, cols] + z * cw[1:2, cols] + zr * cw[2:3, cols]
        a_ref[:, cols] = (part(0) * conv).astype(BF16)
    y = _rms(_dot(a_ref[...], wout_ref[...].astype(BF16))) * g_ref[3:4, :]
    o_ref[...] = x_ref[...] + m_ref[5:6, :] * y


def _conv_layer(x, mods, norm_g, w_in, cw, cb, w_out, li, j):
    tm = CONV_TM
    return pl.pallas_call(
        _conv_kernel,
        out_shape=jax.ShapeDtypeStruct((N_TOK, D_MODEL), F32),
        grid=(N_TOK // tm,),
        in_specs=[
            _tok_spec(tm, D_MODEL), _mods_spec(li, tm), _norm_spec(li), _resident(w_in, j),
            pl.BlockSpec((None, 3, D_MODEL), lambda i: (j, 0, 0)),
            pl.BlockSpec((None, 1, D_MODEL), lambda i: (j, 0, 0)),
            _resident(w_out, j),
        ],
        out_specs=_tok_spec(tm, D_MODEL),
        scratch_shapes=[pltpu.VMEM((tm, D_MODEL), BF16)],
        compiler_params=_params(("arbitrary",)),
        name="conv_layer",
    )(x, mods, norm_g, w_in, cw, cb.reshape(-1, 1, D_MODEL), w_out)


def _block_row(x, w, off):
    c = x.shape[0]
    blk = 2 * w
    row = lambda r: jnp.broadcast_to(x[r:r + 1, :], (max(blk, SUBLANES), LANES))
    if blk >= SUBLANES:
        return jnp.concatenate([row(j * blk + off) for j in range(c // blk)], axis=0)
    if blk == 4:
        upper = lax.broadcasted_iota(jnp.int32, (SUBLANES, LANES), 0) < 4
        return jnp.concatenate(
            [jnp.where(upper, row(SUBLANES * v + off), row(SUBLANES * v + 4 + off))
             for v in range(c // SUBLANES)], axis=0)
    odd = (lax.broadcasted_iota(jnp.int32, (c, LANES), 0) & 1) != 0
    if off == 0:
        return jnp.where(odd, pltpu.roll(x, 1, 0), x)
    return jnp.where(odd, x, pltpu.roll(x, c - 1, 0))


def _split3(x):
    hi = x.astype(BF16)
    r = x - hi.astype(F32)
    mid = r.astype(BF16)
    lo = (r - mid.astype(F32)).astype(BF16)
    return jnp.concatenate([hi, mid, lo], axis=0)


def _hgrn_kernel(*refs, seq, li, hp, has_init, emit_state):
    lb_ref, lv_ref, zq_ref, zi_ref, zf_ref, zb_ref, zg_ref, ng_ref = refs[:8]
    refs = refs[8:]
    s0_ref = None
    if has_init:
        s0_ref, refs = refs[0], refs[1:]
    o_ref, refs = refs[0], refs[1:]
    sout_ref = None
    if emit_state:
        sout_ref, refs = refs[0], refs[1:]
    acc_ref, qf_ref, qb_ref, kv_ref, dec_ref, stf_ref, stb_ref = refs
    c = GLA_CHUNK
    nc = seq // c

    def lower_bound(x):
        e = jnp.exp(x - jnp.max(x, axis=0, keepdims=True))
        pr = e / jnp.sum(e, axis=0, keepdims=True)
        return jnp.sum(pr[1:li + 1], axis=0, keepdims=True)

    lbf, lbb = lower_bound(lb_ref[0]), lower_bound(lb_ref[1])
    ri = lax.broadcasted_iota(jnp.int32, (c, c), 0)
    ci = lax.broadcasted_iota(jnp.int32, (c, c), 1)
    tril = jnp.where(ci <= ri, 1.0, 0.0).astype(BF16)
    triu = jnp.where(ci >= ri, 1.0, 0.0).astype(BF16)
    tril3 = jnp.concatenate([tril, tril, tril], axis=1)
    triu3 = jnp.concatenate([triu, triu, triu], axis=1)
    lv = lv_ref[...]
    rowi = lax.broadcasted_iota(jnp.int32, (c, LANES), 0)

    def chunk_local(n, hd):
        rows = pl.ds(pl.multiple_of(n * c, c), c)
        cols = slice(hd * LANES, (hd + 1) * LANES)
        q = _silu(zq_ref[rows, cols]) * (HGRN_DK ** -0.5)
        zi = zi_ref[rows, cols]
        v = zi.astype(BF16)
        ff = lbf[:, cols] + (1.0 - lbf[:, cols]) * jax.nn.sigmoid(zf_ref[rows, cols])
        fb = lbb[:, cols] + (1.0 - lbb[:, cols]) * jax.nn.sigmoid(zb_ref[rows, cols])
        kf = 1.0 - ff
        kb = 1.0 - fb
        pre = _dot(tril3, _split3(jnp.log2(ff)))
        suf = _dot(triu3, _split3(jnp.log2(fb)))
        attn = jnp.where(lv == 0, _dot_nt(q.astype(BF16), (kf + kb).astype(BF16)), 0.0)
        odd = (rowi & 1) != 0
        qh = (q * jnp.where(odd, ff, fb)).astype(BF16)
        kh = jnp.where(odd, kb, kf).astype(BF16)
        attn = jnp.where(lv == 1, _dot_nt(qh, kh), attn)
        w, level = 2, 2
        while w < c:
            d_f = pre - _block_row(pre, w, w - 1)
            d_b = suf - _block_row(suf, w, w)
            xq = jnp.minimum(d_f, d_b)
            xk = xq - (d_f + d_b)
            second = (rowi & w) != 0
            qh = (q * jnp.exp2(xq)).astype(BF16)
            kh = (jnp.where(second, kb, kf) * jnp.exp2(xk)).astype(BF16)
            attn = jnp.where(lv == level, _dot_nt(qh, kh), attn)
            w *= 2
            level += 1
        acc_ref[hd, rows, :] = _dot(attn.astype(BF16), v)
        tot_f = pre[c - 1:c, :]
        tot_b = suf[0:1, :]
        qf_ref[hd, rows, :] = (q * jnp.exp2(pre)).astype(BF16)
        qb_ref[hd, rows, :] = (q * jnp.exp2(suf)).astype(BF16)
        kt = jnp.concatenate([(kf * jnp.exp2(tot_f - pre)).astype(BF16),
                              (kb * jnp.exp2(tot_b - suf)).astype(BF16)], axis=1)
        kv_ref[hd, n] = _dot(zi.T.astype(BF16), kt)
        dec_ref[hd, n] = jnp.exp2(jnp.concatenate([tot_f, tot_b], axis=1))

    def local_body(n, carry):
        for hd in range(hp):
            chunk_local(n, hd)
        return carry

    lax.fori_loop(0, nc, local_body, 0, unroll=min(nc, GLA_ROWS_IN_FLIGHT // (c * hp)))

    for hd in range(hp):
        if has_init:
            stf_ref[hd] = s0_ref[0, hd].T
            stb_ref[hd] = s0_ref[1, hd].T
        else:
            stf_ref[hd] = jnp.zeros((HGRN_DV, HGRN_DK), F32)
            stb_ref[hd] = jnp.zeros((HGRN_DV, HGRN_DK), F32)

    def carry_state(i, carry):
        n, m = i, nc - 1 - i
        rn = pl.ds(pl.multiple_of(n * c, c), c)
        rm = pl.ds(pl.multiple_of(m * c, c), c)
        for hd in range(hp):
            stf = stf_ref[hd]
            stb = stb_ref[hd]
            acc_ref[hd, rn, :] += _dot_nt(qf_ref[hd, rn, :], stf.astype(BF16))
            acc_ref[hd, rm, :] += _dot_nt(qb_ref[hd, rm, :], stb.astype(BF16))
            stf_ref[hd] = stf * dec_ref[hd, n][:, :HGRN_DK] + kv_ref[hd, n][:, :HGRN_DK]
            stb_ref[hd] = stb * dec_ref[hd, m][:, HGRN_DK:] + kv_ref[hd, m][:, HGRN_DK:]
        return carry

    lax.fori_loop(0, nc, carry_state, 0, unroll=True)
    for hd in range(hp):
        cols = slice(hd * LANES, (hd + 1) * LANES)
        o_ref[:, cols] = (_rms(acc_ref[hd]) * ng_ref[...] * _silu(zg_ref[:, cols])).astype(o_ref.dtype)
        if emit_state:
            sout_ref[0, hd] = stf_ref[hd].T
            sout_ref[1, hd] = stb_ref[hd].T


def _pair_levels():
    t = np.arange(GLA_CHUNK)
    x = t[:, None] ^ t[None, :]
    return jnp.asarray(np.where(x > 0, np.floor(np.log2(np.maximum(x, 1))) + 1, 0).astype(np.int32))


def _hgrn_core(z, lb, ng, s0, *, j, batch, seq, row_off, li, emit_state):
    has_init = s0 is not None
    blk0 = row_off // seq
    fd = HGRN_FD // LANES
    dm = D_MODEL // LANES
    c = GLA_CHUNK
    hp = max(1, GLA_ROWS_IN_FLIGHT // seq)
    width = hp * LANES

    def col(c0):
        return pl.BlockSpec((seq, width), lambda b, h: (blk0 + b, c0 // hp + h))

    in_specs = [
        pl.BlockSpec((2, DEPTH, width), lambda b, h: (0, 0, h)),
        pl.BlockSpec((c, c), lambda b, h: (0, 0)),
        col(0), col(fd), col(fd + dm), col(2 * fd + dm), col(3 * fd + dm),
        pl.BlockSpec((None, 1, HGRN_DV), lambda b, h: (j, 0, 0)),
    ]
    args = [lb, _pair_levels(), z, z, z, z, z, ng.reshape(-1, 1, HGRN_DV)]
    if has_init:
        in_specs.append(pl.BlockSpec((None, None, 2, hp, HGRN_DK, HGRN_DV),
                                     lambda b, h: (b, j, 0, h, 0, 0)))
        args.append(s0)
    out_shape = [jax.ShapeDtypeStruct((batch * seq, D_MODEL), BF16)]
    out_specs = [pl.BlockSpec((seq, width), lambda b, h: (b, h))]
    if emit_state:
        out_shape.append(jax.ShapeDtypeStruct((batch, 2, HGRN_HEADS, HGRN_DK, HGRN_DV), F32))
        out_specs.append(pl.BlockSpec((None, 2, hp, HGRN_DK, HGRN_DV), lambda b, h: (b, 0, h, 0, 0)))
    return pl.pallas_call(
        functools.partial(_hgrn_kernel, seq=seq, li=li, hp=hp, has_init=has_init, emit_state=emit_state),
        out_shape=tuple(out_shape),
        grid=(batch, HGRN_HEADS // hp),
        in_specs=in_specs,
        out_specs=tuple(out_specs),
        scratch_shapes=[
            pltpu.VMEM((hp, seq, HGRN_DV), F32),
            pltpu.VMEM((hp, seq, HGRN_DK), BF16),
            pltpu.VMEM((hp, seq, HGRN_DK), BF16),
            pltpu.VMEM((hp, seq // c, HGRN_DV, 2 * HGRN_DK), F32),
            pltpu.VMEM((hp, seq // c, 1, 2 * HGRN_DK), F32),
            pltpu.VMEM((hp, HGRN_DV, HGRN_DK), F32),
            pltpu.VMEM((hp, HGRN_DV, HGRN_DK), F32),
        ],
        compiler_params=_params(("arbitrary", "arbitrary")),
        name=f"hgrn_core_{seq}",
    )(*args)


def kernel(x_prompt, x_sample, cache_k, cache_v, state_hgrn, c, c_ctx, w_ada, b_ada, norm_g, w_ffn_in, w_ffn_out, w_attn_qkv, w_attn_o, attn_sink, w_conv_in, conv_w, conv_b, w_conv_out, w_hgrn_in, hgrn_lb, hgrn_norm_g, w_hgrn_out):
    cond = jnp.concatenate([c, c_ctx[None, :],
                            jnp.zeros((N_COND - DEC_BATCH - 1, D_MODEL), F32)], axis=0)
    mods = _ada_table(cond, w_ada, b_ada)
    cos, sin = _rope_tables()
    ck = cache_k.reshape(DEC_BATCH, -1, PAST_LEN, KV_DIM)
    cv = cache_v.reshape(DEC_BATCH, -1, PAST_LEN, KV_DIM)
    bf = lambda w: w.astype(BF16)
    xs = (x_sample.reshape(N_SAMPLE_TOK, D_MODEL), x_prompt.reshape(N_PROMPT_TOK, D_MODEL))
    ffn_w = (bf(w_ffn_in[0, 0]), bf(w_ffn_out[0, 0]))
    new_k, new_v, new_s = [], [], []
    for li in range(DEPTH):
        kind, j = li % N_MIXERS, li // N_MIXERS
        mixer = None
        (x,), ffn_w = _ffn(xs, mods, norm_g, ffn_w, (w_ffn_in, w_ffn_out, li, 1), li, 0)
        if kind == 0:
            q, k, v, kf, vf = _qkv(x, mods, norm_g, w_attn_qkv, li, j, cos, sin)
            o_s = _attn_latent(attn_sink[j], q, k, v, ck, cv, j)
            o_p = _attn_context(attn_sink[j], q, k, v)
            mixer = (o_s, o_p, w_attn_o, j)
            new_k.append(kf.reshape(BATCH, SEQ, ATTN_KV_HEADS, HEAD_DIM))
            new_v.append(vf.reshape(BATCH, SEQ, ATTN_KV_HEADS, HEAD_DIM))
        elif kind == 1:
            x = _conv_layer(x, mods, norm_g, w_conv_in, conv_w, conv_b, w_conv_out, li, j)
        else:
            z = _proj(x, mods, norm_g, w_hgrn_in, li, j, F32)
            o_s = _hgrn_core(z, hgrn_lb, hgrn_norm_g, state_hgrn, j=j, batch=DEC_BATCH,
                             seq=DEC_SEQ, row_off=0, li=li, emit_state=False)[0]
            o_p, st = _hgrn_core(z, hgrn_lb, hgrn_norm_g, None, j=j, batch=BATCH, seq=SEQ,
                                 row_off=N_SAMPLE_TOK, li=li, emit_state=True)
            mixer = (o_s, o_p, w_hgrn_out, j)
            new_s.append(st)
        last = li == DEPTH - 1
        nxt = None if last else (w_ffn_in, w_ffn_out, li + 1, 0)
        xs, ffn_w = _ffn((x,), mods, norm_g, ffn_w, nxt, li, 1, split_out=last, mixer=mixer)
    y_sample = xs[0].reshape(DEC_BATCH, DEC_SEQ, D_MODEL)
    y_prompt = xs[1].reshape(BATCH, SEQ, D_MODEL)
    return (y_prompt, y_sample, jnp.stack(new_k, axis=1), jnp.stack(new_v, axis=1),
            jnp.stack(new_s, axis=1))
```

```python
import functools
import math

import jax
import jax.numpy as jnp
import numpy as np
from jax import lax
from jax.experimental import pallas as pl
from jax.experimental.pallas import tpu as pltpu

D_MODEL = 1024
BATCH = 16
SEQ = 256
DEPTH = 4
DEC_BATCH = 8
DEC_SEQ = 1024
PAST_LEN = 256
GRID_W = 64
N_MIXERS = 3
N_MOD = 9
N_NORMS = 6
D_FF = 2816
FFN_RES = 0.5
ATTN_HEADS = 16
ATTN_KV_HEADS = 4
HEAD_DIM = 64
ATTN_DIM = ATTN_HEADS * HEAD_DIM
KV_DIM = ATTN_KV_HEADS * HEAD_DIM
BLOCK = 128
WINDOW = 128
ATTN_SCALE = HEAD_DIM ** -0.5
LOG2E = math.log2(math.e)
ROPE_BASE = 10000.0
ROPE_PAIRS_PER_AXIS = HEAD_DIM // 4
HGRN_HEADS = 8
HGRN_DK = 128
HGRN_DV = D_MODEL // HGRN_HEADS
HGRN_FD = HGRN_HEADS * HGRN_DK
EPS = 1e-6

N_SAMPLE_TOK = DEC_BATCH * DEC_SEQ
N_PROMPT_TOK = BATCH * SEQ
N_TOK = N_SAMPLE_TOK + N_PROMPT_TOK
N_COND = 16
CTX_COND = DEC_BATCH

LANES = 128
SUBLANES = 8
TM = 512
CONV_TM = 1024
FFN_SPLITS = (0, 1024, 2048, D_FF)
GLA_CHUNK = 128
GLA_ROWS_IN_FLIGHT = 2048
NEG_BIG = -1e30
VMEM_LIMIT = 56 * 1024 * 1024

F32 = jnp.float32
BF16 = jnp.bfloat16


def _rms(x):
    return x * lax.rsqrt(jnp.mean(x * x, axis=-1, keepdims=True) + EPS)


def _silu(x):
    return x * jax.nn.sigmoid(x)


def _dot(a, b):
    return jnp.dot(a, b, preferred_element_type=F32)


def _dot_nt(a, b):
    return lax.dot_general(a, b, (((1,), (1,)), ((), ())), preferred_element_type=F32)


def _params(sem):
    return pltpu.CompilerParams(dimension_semantics=sem, vmem_limit_bytes=VMEM_LIMIT)


def _resident(arr, *lead):
    shape = (None,) * len(lead) + arr.shape[len(lead):]
    idx = tuple(lead) + (0,) * (arr.ndim - len(lead))
    return pl.BlockSpec(shape, lambda *_: idx, pipeline_mode=pl.Buffered(1))


def _mods_spec(li, tm):
    per = DEC_SEQ // tm
    return pl.BlockSpec((None, None, N_MOD, D_MODEL),
                        lambda i: (li, jnp.minimum(i // per, CTX_COND), 0, 0))


def _norm_spec(li):
    return pl.BlockSpec((None, N_NORMS, D_MODEL), lambda i: (li, 0, 0))


def _tok_spec(tm, n):
    return pl.BlockSpec((tm, n), lambda i: (i, 0))


def _split_specs(tm, n):
    ns = N_SAMPLE_TOK // tm
    return [pl.BlockSpec((tm, n), lambda i: (jnp.minimum(i, ns - 1), 0)),
            pl.BlockSpec((tm, n), lambda i: (jnp.maximum(i - ns, 0), 0))]


def _load_split(s_ref, p_ref, tm):
    return jnp.where(pl.program_id(0) < N_SAMPLE_TOK // tm, s_ref[...], p_ref[...])


def _store_split(s_ref, p_ref, val, tm):
    ns = N_SAMPLE_TOK // tm
    i = pl.program_id(0)

    @pl.when(i < ns)
    def _():
        s_ref[...] = val

    @pl.when(i >= ns)
    def _():
        p_ref[...] = val


def _ada_kernel(c_ref, w_ref, b_ref, o_ref):
    s = _silu(c_ref[...]).astype(BF16)
    o_ref[...] = _dot(s, w_ref[...].astype(BF16)) + b_ref[...]


def _ada_table(cond, w_ada, b_ada):
    tn = 3 * D_MODEL
    n = N_MOD * D_MODEL
    out = pl.pallas_call(
        _ada_kernel,
        out_shape=jax.ShapeDtypeStruct((DEPTH, N_COND, n), F32),
        grid=(DEPTH, n // tn),
        in_specs=[
            pl.BlockSpec((N_COND, D_MODEL), lambda l, j: (0, 0)),
            pl.BlockSpec((None, D_MODEL, tn), lambda l, j: (l, 0, j)),
            pl.BlockSpec((None, 1, tn), lambda l, j: (l, 0, j)),
        ],
        out_specs=pl.BlockSpec((None, N_COND, tn), lambda l, j: (l, 0, j)),
        compiler_params=_params(("arbitrary", "arbitrary")),
        name="ada_table",
    )(cond, w_ada, b_ada.reshape(DEPTH, 1, n))
    return out.reshape(DEPTH, N_COND, N_MOD, D_MODEL)


NEXT_IN_ROWS = 64
NEXT_OUT_ROWS = 128


def _ffn_kernel(*refs, k0, g0, split_in, split_out, has_mixer, has_next):
    n_in = 2 if split_in else 1
    x_refs, refs = refs[:n_in], refs[n_in:]
    mix_refs = ()
    if has_mixer:
        mix_refs, refs = refs[:3], refs[3:]
    m_ref, g_ref, win_ref, wout_ref = refs[:4]
    refs = refs[4:]
    if has_next:
        nin_ref, nout_ref = refs[:2]
        refs[-2][...] = nin_ref[...].astype(BF16)
        refs[-1][...] = nout_ref[...].astype(BF16)
        refs = refs[2:-2]
    o_refs = refs
    x = _load_split(*x_refs, TM) if split_in else x_refs[0][...]
    m = m_ref[...]
    g = g_ref[...]
    if has_mixer:
        as_ref, ap_ref, wo_ref = mix_refs
        mixed = _dot(_load_split(as_ref, ap_ref, TM), wo_ref[...].astype(BF16))
        x = x + m[5:6] * (_rms(mixed) * g[3:4])
    h = _rms(x) * g[g0:g0 + 1]
    h = (h * (1.0 + m[k0 + 1:k0 + 2]) + m[k0:k0 + 1]).astype(BF16)
    acc = None
    for lo, hi in zip(FFN_SPLITS[:-1], FFN_SPLITS[1:]):
        a = _dot(h, win_ref[:, lo:hi])
        b = _dot(h, win_ref[:, D_FF + lo:D_FF + hi])
        u = (_silu(a) * b).astype(BF16)
        part = _dot(u, wout_ref[lo:hi, :])
        acc = part if acc is None else acc + part
    y = _rms(acc) * g[g0 + 1:g0 + 2]
    res = x + FFN_RES * m[k0 + 2:k0 + 3] * y
    if split_out:
        _store_split(*o_refs, res, TM)
    else:
        o_refs[0][...] = res


def _ffn(xs, mods, norm_g, w_bf, nxt, li, which, split_out=False, mixer=None):
    k0, g0 = (0, 0) if which == 0 else (6, 4)
    split_in = len(xs) == 2
    steps = N_TOK // TM
    x_specs = _split_specs(TM, D_MODEL) if split_in else [_tok_spec(TM, D_MODEL)]
    mix_specs, mix_args = [], []
    if mixer is not None:
        a_s, a_p, w_o, j = mixer
        mix_specs = _split_specs(TM, D_MODEL) + [_resident(w_o, j)]
        mix_args = [a_s, a_p, w_o]
    if split_out:
        out_shape = [jax.ShapeDtypeStruct((N_SAMPLE_TOK, D_MODEL), F32),
                     jax.ShapeDtypeStruct((N_PROMPT_TOK, D_MODEL), F32)]
        out_specs = _split_specs(TM, D_MODEL)
    else:
        out_shape = [jax.ShapeDtypeStruct((N_TOK, D_MODEL), F32)]
        out_specs = [_tok_spec(TM, D_MODEL)]
    nxt_specs, nxt_args = [], []
    if nxt is not None:
        w_in_all, w_out_all, li2, which2 = nxt
        last_in = D_MODEL // NEXT_IN_ROWS - 1
        last_out = D_FF // NEXT_OUT_ROWS - 1
        assert last_in < steps and last_out < steps
        nxt_specs = [
            pl.BlockSpec((None, None, NEXT_IN_ROWS, 2 * D_FF),
                         lambda i: (li2, which2, jnp.minimum(i, last_in), 0)),
            pl.BlockSpec((None, None, NEXT_OUT_ROWS, D_MODEL),
                         lambda i: (li2, which2, jnp.minimum(i, last_out), 0)),
        ]
        nxt_args = [w_in_all, w_out_all]
        out_shape += [jax.ShapeDtypeStruct((D_MODEL, 2 * D_FF), BF16),
                      jax.ShapeDtypeStruct((D_FF, D_MODEL), BF16)]
        out_specs += [pl.BlockSpec((NEXT_IN_ROWS, 2 * D_FF), lambda i: (jnp.minimum(i, last_in), 0)),
                      pl.BlockSpec((NEXT_OUT_ROWS, D_MODEL), lambda i: (jnp.minimum(i, last_out), 0))]
    outs = pl.pallas_call(
        functools.partial(_ffn_kernel, k0=k0, g0=g0, split_in=split_in, split_out=split_out,
                          has_mixer=mixer is not None, has_next=nxt is not None),
        out_shape=tuple(out_shape),
        grid=(steps,),
        in_specs=x_specs + mix_specs + [
            _mods_spec(li, TM), _norm_spec(li), _resident(w_bf[0]), _resident(w_bf[1])] + nxt_specs,
        out_specs=tuple(out_specs),
        compiler_params=_params(("arbitrary",)),
        name=f"ffn{which}",
    )(*xs, *mix_args, mods, norm_g, *w_bf, *nxt_args)
    n_x = 2 if split_out else 1
    return tuple(outs[:n_x]), tuple(outs[n_x:])


def _mixer_in(x_ref, m_ref, g_ref):
    m = m_ref[...]
    h = _rms(x_ref[...]) * g_ref[2:3, :]
    return (h * (1.0 + m[4:5]) + m[3:4]).astype(BF16)


PROJ_COLS = 512


def _proj_kernel(x_ref, m_ref, g_ref, w_ref, o_ref):
    h = _mixer_in(x_ref, m_ref, g_ref)
    for c0 in range(0, o_ref.shape[1], PROJ_COLS):
        o_ref[:, c0:c0 + PROJ_COLS] = _dot(h, w_ref[:, c0:c0 + PROJ_COLS].astype(BF16)).astype(o_ref.dtype)


def _proj(x, mods, norm_g, w, li, j, out_dtype):
    n = w.shape[-1]
    return pl.pallas_call(
        _proj_kernel,
        out_shape=jax.ShapeDtypeStruct((N_TOK, n), out_dtype),
        grid=(N_TOK // TM,),
        in_specs=[_tok_spec(TM, D_MODEL), _mods_spec(li, TM), _norm_spec(li), _resident(w, j)],
        out_specs=_tok_spec(TM, n),
        compiler_params=_params(("arbitrary",)),
        name="mixer_proj",
    )(x, mods, norm_g, w)


def _rope_cols(t, cos, sin):
    lane = lax.broadcasted_iota(jnp.int32, (t.shape[0], LANES), 1)
    low = (lane % HEAD_DIM) < (HEAD_DIM // 2)
    outs = []
    for c in range(t.shape[1] // LANES):
        tc = t[:, c * LANES:(c + 1) * LANES]
        partner = jnp.where(low, pltpu.roll(tc, LANES - HEAD_DIM // 2, 1),
                            pltpu.roll(tc, HEAD_DIM // 2, 1))
        outs.append(tc * cos + partner * sin)
    return jnp.concatenate(outs, axis=1)


def _qkv_kernel(x_ref, m_ref, g_ref, w_ref, cos_ref, sin_ref,
                q_ref, k_ref, v_ref, kf_ref, vf_ref):
    h = _mixer_in(x_ref, m_ref, g_ref)
    cos = cos_ref[...]
    sin = sin_ref[...]
    for c0 in range(0, ATTN_DIM, KV_DIM):
        q = _dot(h, w_ref[:, c0:c0 + KV_DIM].astype(BF16))
        q_ref[:, c0:c0 + KV_DIM] = (_rope_cols(q, cos, sin) * (ATTN_SCALE * LOG2E)).astype(BF16)
    k = _dot(h, w_ref[:, ATTN_DIM:ATTN_DIM + KV_DIM].astype(BF16))
    v = _dot(h, w_ref[:, ATTN_DIM + KV_DIM:].astype(BF16))

    @pl.when(pl.program_id(0) >= N_SAMPLE_TOK // TM)
    def _():
        kf_ref[...] = k
        vf_ref[...] = v

    k_ref[...] = _rope_cols(k, cos, sin).astype(BF16)
    v_ref[...] = v.astype(BF16)


def _rope_tables():
    pos = np.arange(DEC_SEQ)
    row = (pos // GRID_W).astype(np.float32)
    col = (pos % GRID_W).astype(np.float32)
    inv = (ROPE_BASE ** (-np.arange(ROPE_PAIRS_PER_AXIS, dtype=np.float32) / ROPE_PAIRS_PER_AXIS)).astype(np.float32)
    ang = jnp.asarray(np.concatenate([row[:, None] * inv, col[:, None] * inv], axis=-1).astype(np.float32))
    cos, sin = jnp.cos(ang), jnp.sin(ang)
    cos = jnp.concatenate([cos, cos, cos, cos], axis=-1)
    sin = jnp.concatenate([-sin, sin, -sin, sin], axis=-1)
    cos = jnp.concatenate([cos, jnp.ones((TM, LANES), F32)], axis=0)
    sin = jnp.concatenate([sin, jnp.zeros((TM, LANES), F32)], axis=0)
    return cos, sin


def _qkv(x, mods, norm_g, w, li, j, cos, sin):
    per = DEC_SEQ // TM
    ns = N_SAMPLE_TOK // TM
    tab = pl.BlockSpec((TM, LANES), lambda i: (jnp.where(i < ns, i % per, per), 0))
    cache = pl.BlockSpec((TM, KV_DIM), lambda i: (jnp.maximum(i - ns, 0), 0))
    return pl.pallas_call(
        _qkv_kernel,
        out_shape=(
            jax.ShapeDtypeStruct((N_TOK, ATTN_DIM), BF16),
            jax.ShapeDtypeStruct((N_TOK, KV_DIM), BF16),
            jax.ShapeDtypeStruct((N_TOK, KV_DIM), BF16),
            jax.ShapeDtypeStruct((N_PROMPT_TOK, KV_DIM), F32),
            jax.ShapeDtypeStruct((N_PROMPT_TOK, KV_DIM), F32),
        ),
        grid=(N_TOK // TM,),
        in_specs=[_tok_spec(TM, D_MODEL), _mods_spec(li, TM), _norm_spec(li), _resident(w, j), tab, tab],
        out_specs=(_tok_spec(TM, ATTN_DIM), _tok_spec(TM, KV_DIM), _tok_spec(TM, KV_DIM), cache, cache),
        compiler_params=_params(("arbitrary",)),
        name="attn_qkv",
    )(x, mods, norm_g, w, cos, sin)


N_HEAD_SLOTS = 2 * ATTN_KV_HEADS


def _fill_head_slots(x, k_dst, v_dst, row0, is_value):
    dst = v_dst if is_value else k_dst
    pad = 1.0 if is_value else 0.0
    rows = x.shape[0]
    low = lax.broadcasted_iota(jnp.int32, (rows, LANES), 1) < HEAD_DIM
    for g in range(ATTN_KV_HEADS):
        cg, pg = g // 2, g % 2
        xb = x[:, cg * LANES:(cg + 1) * LANES]
        xr = pltpu.roll(xb, HEAD_DIM, 1)
        x_lo, x_hi = (xb, xr) if pg == 0 else (xr, xb)
        dst[2 * g, pl.ds(row0, rows), :] = jnp.where(low, x_lo, pad).astype(BF16)
        dst[2 * g + 1, pl.ds(row0, rows), :] = jnp.where(low, pad, x_hi).astype(BF16)


def _attend_rows(q, key_blocks, sink_ref, o_ref, orow0, s_scr, p_scr):
    rows = q.shape[0]
    rid = lax.broadcasted_iota(jnp.int32, (2 * rows, 1), 0)
    low = lax.broadcasted_iota(jnp.int32, (2 * rows, LANES), 1) < HEAD_DIM
    sinks, maxes = [], []
    for slot in range(N_HEAD_SLOTS):
        g, side = slot // 2, slot % 2
        c0 = 2 * g * LANES
        qs = jnp.concatenate([q[:, c0:c0 + LANES], q[:, c0 + LANES:c0 + 2 * LANES]], axis=0)
        sk = jnp.where(rid < rows, sink_ref[4 * g + side], sink_ref[4 * g + 2 + side]) * LOG2E
        mx = sk
        col = 0
        for k_of, _, bias, n_keys in key_blocks:
            s = _dot_nt(qs, k_of(slot))
            if bias is not None:
                s = s + bias
            s_scr[slot, :, col:col + n_keys] = s
            mx = jnp.maximum(mx, jnp.max(s, axis=-1, keepdims=True))
            col += n_keys
        sinks.append(sk)
        maxes.append(mx)
    for slot in range(N_HEAD_SLOTS):
        p_scr[slot] = jnp.exp2(s_scr[slot] - maxes[slot]).astype(BF16)
    for g in range(ATTN_KV_HEADS):
        res = []
        for side in range(2):
            slot = 2 * g + side
            part = None
            col = 0
            for _, v_of, _, n_keys in key_blocks:
                pv = _dot(p_scr[slot, :, col:col + n_keys], v_of(slot))
                part = pv if part is None else part + pv
                col += n_keys
            den = pltpu.roll(part, HEAD_DIM, 1) + jnp.exp2(sinks[slot] - maxes[slot])
            res.append(part / den)
        out = jnp.where(low, res[0], res[1]).astype(o_ref.dtype)
        c0 = 2 * g * LANES
        o_ref[pl.ds(orow0, rows), c0:c0 + LANES] = out[:rows]
        o_ref[pl.ds(orow0, rows), c0 + LANES:c0 + 2 * LANES] = out[rows:]


def _attn_lat_kernel(sink_ref, q_ref, k_ref, v_ref, ck_ref, cv_ref, o_ref,
                     kl_ref, vl_ref, kc_ref, vc_ref, s_scr, p_scr):
    nb = DEC_SEQ // BLOCK
    zeros = jnp.zeros((BLOCK, LANES), BF16)
    for slot in range(N_HEAD_SLOTS):
        for ref in (kl_ref, vl_ref):
            ref[slot, pl.ds(0, BLOCK), :] = zeros
            ref[slot, pl.ds(DEC_SEQ + BLOCK, BLOCK), :] = zeros
    _fill_head_slots(k_ref[...].astype(F32), kl_ref, vl_ref, BLOCK, False)
    _fill_head_slots(v_ref[...].astype(F32), kl_ref, vl_ref, BLOCK, True)
    _fill_head_slots(ck_ref[...], kc_ref, vc_ref, 0, False)
    _fill_head_slots(cv_ref[...], kc_ref, vc_ref, 0, True)
    n_band = 3 * BLOCK
    qi = lax.broadcasted_iota(jnp.int32, (2 * BLOCK, n_band), 0) % BLOCK
    u = lax.broadcasted_iota(jnp.int32, (2 * BLOCK, n_band), 1)
    in_window = jnp.abs(qi + BLOCK - u) <= WINDOW

    def body(n, carry):
        r0 = pl.multiple_of(n * BLOCK, BLOCK)
        kpos = (n - 1) * BLOCK + u
        ok = in_window & (kpos >= 0) & (kpos < nb * BLOCK)
        bias = jnp.where(ok, 0.0, NEG_BIG)
        band = (lambda s: kl_ref[s, pl.ds(r0, n_band), :], lambda s: vl_ref[s, pl.ds(r0, n_band), :],
                bias, n_band)
        ctx = (lambda s: kc_ref[s], lambda s: vc_ref[s], None, PAST_LEN)
        _attend_rows(q_ref[pl.ds(r0, BLOCK), :], [band, ctx], sink_ref, o_ref, r0, s_scr, p_scr)
        return carry

    lax.fori_loop(0, nb, body, 0)


def _attn_latent(sink, q, k, v, ck, cv, j):
    seq = lambda n: pl.BlockSpec((DEC_SEQ, n), lambda b: (b, 0))
    ctx = pl.BlockSpec((None, None, PAST_LEN, KV_DIM), lambda b: (b, j, 0, 0))
    return pl.pallas_call(
        _attn_lat_kernel,
        out_shape=jax.ShapeDtypeStruct((N_SAMPLE_TOK, ATTN_DIM), BF16),
        grid=(DEC_BATCH,),
        in_specs=[pl.BlockSpec(memory_space=pltpu.SMEM), seq(ATTN_DIM), seq(KV_DIM), seq(KV_DIM), ctx, ctx],
        out_specs=seq(ATTN_DIM),
        scratch_shapes=[
            pltpu.VMEM((N_HEAD_SLOTS, DEC_SEQ + 2 * BLOCK, LANES), BF16),
            pltpu.VMEM((N_HEAD_SLOTS, DEC_SEQ + 2 * BLOCK, LANES), BF16),
            pltpu.VMEM((N_HEAD_SLOTS, PAST_LEN, LANES), BF16),
            pltpu.VMEM((N_HEAD_SLOTS, PAST_LEN, LANES), BF16),
            pltpu.VMEM((N_HEAD_SLOTS, 2 * BLOCK, 3 * BLOCK + PAST_LEN), F32),
            pltpu.VMEM((N_HEAD_SLOTS, 2 * BLOCK, 3 * BLOCK + PAST_LEN), BF16),
        ],
        compiler_params=_params(("arbitrary",)),
        name="attn_latent",
    )(sink, q, k, v, ck, cv)


def _attn_ctx_kernel(sink_ref, q_ref, k_ref, v_ref, o_ref, ks_ref, vs_ref, s_scr, p_scr):
    _fill_head_slots(k_ref[...].astype(F32), ks_ref, vs_ref, 0, False)
    _fill_head_slots(v_ref[...].astype(F32), ks_ref, vs_ref, 0, True)
    blk = (lambda s: ks_ref[s], lambda s: vs_ref[s], None, SEQ)
    _attend_rows(q_ref[...], [blk], sink_ref, o_ref, 0, s_scr, p_scr)


def _attn_context(sink, q, k, v):
    off = N_SAMPLE_TOK // SEQ
    row = lambda n: pl.BlockSpec((SEQ, n), lambda b: (off + b, 0))
    return pl.pallas_call(
        _attn_ctx_kernel,
        out_shape=jax.ShapeDtypeStruct((N_PROMPT_TOK, ATTN_DIM), BF16),
        grid=(BATCH,),
        in_specs=[pl.BlockSpec(memory_space=pltpu.SMEM), row(ATTN_DIM), row(KV_DIM), row(KV_DIM)],
        out_specs=pl.BlockSpec((SEQ, ATTN_DIM), lambda b: (b, 0)),
        scratch_shapes=[pltpu.VMEM((N_HEAD_SLOTS, SEQ, LANES), BF16),
                        pltpu.VMEM((N_HEAD_SLOTS, SEQ, LANES), BF16),
                        pltpu.VMEM((N_HEAD_SLOTS, 2 * SEQ, SEQ), F32),
                        pltpu.VMEM((N_HEAD_SLOTS, 2 * SEQ, SEQ), BF16)],
        compiler_params=_params(("arbitrary",)),
        name="attn_context",
    )(sink, q, k, v)


CONV_COLS = 256


def _conv_kernel(x_ref, m_ref, g_ref, win_ref, cw_ref, cb_ref, wout_ref, o_ref, a_ref):
    i = pl.program_id(0)
    h = _mixer_in(x_ref, m_ref, g_ref)
    tm = h.shape[0]
    seq = jnp.where(i < N_SAMPLE_TOK // tm, DEC_SEQ, SEQ)
    pos = lax.broadcasted_iota(jnp.int32, (tm, CONV_COLS), 0) & (seq - 1)
    first = pos == 0
    last = pos == seq - 1
    cw = cw_ref[...]
    cb = cb_ref[...]
    for c in range(D_MODEL // CONV_COLS):
        cols = slice(c * CONV_COLS, (c + 1) * CONV_COLS)
        part = lambda j: _dot(
            h, win_ref[:, j * D_MODEL + c * CONV_COLS:j * D_MODEL + (c + 1) * CONV_COLS].astype(BF16))
        z = part(1) * part(2)
        zl = jnp.where(first, 0.0, pltpu.roll(z, 1, 0))
        zr = jnp.where(last, 0.0, pltpu.roll(z, tm - 1, 0))
        conv = cb[:, cols] + zl * cw[0:1, cols] + z * cw[1:2, cols] + zr * cw[2:3, cols]
        a_ref[:, cols] = (part(0) * conv).astype(BF16)
    y = _rms(_dot(a_ref[...], wout_ref[...].astype(BF16))) * g_ref[3:4, :]
    o_ref[...] = x_ref[...] + m_ref[5:6, :] * y


def _conv_layer(x, mods, norm_g, w_in, cw, cb, w_out, li, j):
    tm = CONV_TM
    return pl.pallas_call(
        _conv_kernel,
        out_shape=jax.ShapeDtypeStruct((N_TOK, D_MODEL), F32),
        grid=(N_TOK // tm,),
        in_specs=[
            _tok_spec(tm, D_MODEL), _mods_spec(li, tm), _norm_spec(li), _resident(w_in, j),
            pl.BlockSpec((None, 3, D_MODEL), lambda i: (j, 0, 0)),
            pl.BlockSpec((None, 1, D_MODEL), lambda i: (j, 0, 0)),
            _resident(w_out, j),
        ],
        out_specs=_tok_spec(tm, D_MODEL),
        scratch_shapes=[pltpu.VMEM((tm, D_MODEL), BF16)],
        compiler_params=_params(("arbitrary",)),
        name="conv_layer",
    )(x, mods, norm_g, w_in, cw, cb.reshape(-1, 1, D_MODEL), w_out)


def _block_row(x, w, off):
    c = x.shape[0]
    blk = 2 * w
    row = lambda r: jnp.broadcast_to(x[r:r + 1, :], (max(blk, SUBLANES), LANES))
    if blk >= SUBLANES:
        return jnp.concatenate([row(j * blk + off) for j in range(c // blk)], axis=0)
    if blk == 4:
        upper = lax.broadcasted_iota(jnp.int32, (SUBLANES, LANES), 0) < 4
        return jnp.concatenate(
            [jnp.where(upper, row(SUBLANES * v + off), row(SUBLANES * v + 4 + off))
             for v in range(c // SUBLANES)], axis=0)
    odd = (lax.broadcasted_iota(jnp.int32, (c, LANES), 0) & 1) != 0
    if off == 0:
        return jnp.where(odd, pltpu.roll(x, 1, 0), x)
    return jnp.where(odd, x, pltpu.roll(x, c - 1, 0))


def _split3(x):
    hi = x.astype(BF16)
    r = x - hi.astype(F32)
    mid = r.astype(BF16)
    lo = (r - mid.astype(F32)).astype(BF16)
    return jnp.concatenate([hi, mid, lo], axis=0)


def _hgrn_kernel(*refs, seq, li, hp, has_init, emit_state):
    lb_ref, lv_ref, zq_ref, zi_ref, zf_ref, zb_ref, zg_ref, ng_ref = refs[:8]
    refs = refs[8:]
    s0_ref = None
    if has_init:
        s0_ref, refs = refs[0], refs[1:]
    o_ref, refs = refs[0], refs[1:]
    sout_ref = None
    if emit_state:
        sout_ref, refs = refs[0], refs[1:]
    acc_ref, qf_ref, qb_ref, kv_ref, dec_ref, stf_ref, stb_ref = refs
    c = GLA_CHUNK
    nc = seq // c

    def lower_bound(x):
        e = jnp.exp(x - jnp.max(x, axis=0, keepdims=True))
        pr = e / jnp.sum(e, axis=0, keepdims=True)
        return jnp.sum(pr[1:li + 1], axis=0, keepdims=True)

    lbf, lbb = lower_bound(lb_ref[0]), lower_bound(lb_ref[1])
    ri = lax.broadcasted_iota(jnp.int32, (c, c), 0)
    ci = lax.broadcasted_iota(jnp.int32, (c, c), 1)
    tril = jnp.where(ci <= ri, 1.0, 0.0).astype(BF16)
    triu = jnp.where(ci >= ri, 1.0, 0.0).astype(BF16)
    tril3 = jnp.concatenate([tril, tril, tril], axis=1)
    triu3 = jnp.concatenate([triu, triu, triu], axis=1)
    lv = lv_ref[...]
    rowi = lax.broadcasted_iota(jnp.int32, (c, LANES), 0)

    def chunk_local(n, hd):
        rows = pl.ds(pl.multiple_of(n * c, c), c)
        cols = slice(hd * LANES, (hd + 1) * LANES)
        q = _silu(zq_ref[rows, cols]) * (HGRN_DK ** -0.5)
        zi = zi_ref[rows, cols]
        v = zi.astype(BF16)
        ff = lbf[:, cols] + (1.0 - lbf[:, cols]) * jax.nn.sigmoid(zf_ref[rows, cols])
        fb = lbb[:, cols] + (1.0 - lbb[:, cols]) * jax.nn.sigmoid(zb_ref[rows, cols])
        kf = 1.0 - ff
        kb = 1.0 - fb
        pre = _dot(tril3, _split3(jnp.log2(ff)))
        suf = _dot(triu3, _split3(jnp.log2(fb)))
        attn = jnp.where(lv == 0, _dot_nt(q.astype(BF16), (kf + kb).astype(BF16)), 0.0)
        odd = (rowi & 1) != 0
        qh = (q * jnp.where(odd, ff, fb)).astype(BF16)
        kh = jnp.where(odd, kb, kf).astype(BF16)
        attn = jnp.where(lv == 1, _dot_nt(qh, kh), attn)
        w, level = 2, 2
        while w < c:
            d_f = pre - _block_row(pre, w, w - 1)
            d_b = suf - _block_row(suf, w, w)
            xq = jnp.minimum(d_f, d_b)
            xk = xq - (d_f + d_b)
            second = (rowi & w) != 0
            qh = (q * jnp.exp2(xq)).astype(BF16)
            kh = (jnp.where(second, kb, kf) * jnp.exp2(xk)).astype(BF16)
            attn = jnp.where(lv == level, _dot_nt(qh, kh), attn)
            w *= 2
            level += 1
        acc_ref[hd, rows, :] = _dot(attn.astype(BF16), v)
        tot_f = pre[c - 1:c, :]
        tot_b = suf[0:1, :]
        qf_ref[hd, rows, :] = (q * jnp.exp2(pre)).astype(BF16)
        qb_ref[hd, rows, :] = (q * jnp.exp2(suf)).astype(BF16)
        kt = jnp.concatenate([(kf * jnp.exp2(tot_f - pre)).astype(BF16),
                              (kb * jnp.exp2(tot_b - suf)).astype(BF16)], axis=1)
        kv_ref[hd, n] = _dot(zi.T.astype(BF16), kt)
        dec_ref[hd, n] = jnp.exp2(jnp.concatenate([tot_f, tot_b], axis=1))

    def local_body(n, carry):
        for hd in range(hp):
            chunk_local(n, hd)
        return carry

    lax.fori_loop(0, nc, local_body, 0, unroll=min(nc, GLA_ROWS_IN_FLIGHT // (c * hp)))

    for hd in range(hp):
        if has_init:
            stf_ref[hd] = s0_ref[0, hd].T
            stb_ref[hd] = s0_ref[1, hd].T
        else:
            stf_ref[hd] = jnp.zeros((HGRN_DV, HGRN_DK), F32)
            stb_ref[hd] = jnp.zeros((HGRN_DV, HGRN_DK), F32)

    def carry_state(i, carry):
        n, m = i, nc - 1 - i
        rn = pl.ds(pl.multiple_of(n * c, c), c)
        rm = pl.ds(pl.multiple_of(m * c, c), c)
        for hd in range(hp):
            stf = stf_ref[hd]
            stb = stb_ref[hd]
            acc_ref[hd, rn, :] += _dot_nt(qf_ref[hd, rn, :], stf.astype(BF16))
            acc_ref[hd, rm, :] += _dot_nt(qb_ref[hd, rm, :], stb.astype(BF16))
            stf_ref[hd] = stf * dec_ref[hd, n][:, :HGRN_DK] + kv_ref[hd, n][:, :HGRN_DK]
            stb_ref[hd] = stb * dec_ref[hd, m][:, HGRN_DK:] + kv_ref[hd, m][:, HGRN_DK:]
        return carry

    lax.fori_loop(0, nc, carry_state, 0, unroll=True)
    for hd in range(hp):
        cols = slice(hd * LANES, (hd + 1) * LANES)
        o_ref[:, cols] = (_rms(acc_ref[hd]) * ng_ref[...] * _silu(zg_ref[:, cols])).astype(o_ref.dtype)
        if emit_state:
            sout_ref[0, hd] = stf_ref[hd].T
            sout_ref[1, hd] = stb_ref[hd].T


def _pair_levels():
    t = np.arange(GLA_CHUNK)
    x = t[:, None] ^ t[None, :]
    return jnp.asarray(np.where(x > 0, np.floor(np.log2(np.maximum(x, 1))) + 1, 0).astype(np.int32))


def _hgrn_core(z, lb, ng, s0, *, j, batch, seq, row_off, li, emit_state):
    has_init = s0 is not None
    blk0 = row_off // seq
    fd = HGRN_FD // LANES
    dm = D_MODEL // LANES
    c = GLA_CHUNK
    hp = max(1, GLA_ROWS_IN_FLIGHT // seq)
    width = hp * LANES

    def col(c0):
        return pl.BlockSpec((seq, width), lambda b, h: (blk0 + b, c0 // hp + h))

    in_specs = [
        pl.BlockSpec((2, DEPTH, width), lambda b, h: (0, 0, h)),
        pl.BlockSpec((c, c), lambda b, h: (0, 0)),
        col(0), col(fd), col(fd + dm), col(2 * fd + dm), col(3 * fd + dm),
        pl.BlockSpec((None, 1, HGRN_DV), lambda b, h: (j, 0, 0)),
    ]
    args = [lb, _pair_levels(), z, z, z, z, z, ng.reshape(-1, 1, HGRN_DV)]
    if has_init:
        in_specs.append(pl.BlockSpec((None, None, 2, hp, HGRN_DK, HGRN_DV),
                                     lambda b, h: (b, j, 0, h, 0, 0)))
        args.append(s0)
    out_shape = [jax.ShapeDtypeStruct((batch * seq, D_MODEL), BF16)]
    out_specs = [pl.BlockSpec((seq, width), lambda b, h: (b, h))]
    if emit_state:
        out_shape.append(jax.ShapeDtypeStruct((batch, 2, HGRN_HEADS, HGRN_DK, HGRN_DV), F32))
        out_specs.append(pl.BlockSpec((None, 2, hp, HGRN_DK, HGRN_DV), lambda b, h: (b, 0, h, 0, 0)))
    return pl.pallas_call(
        functools.partial(_hgrn_kernel, seq=seq, li=li, hp=hp, has_init=has_init, emit_state=emit_state),
        out_shape=tuple(out_shape),
        grid=(batch, HGRN_HEADS // hp),
        in_specs=in_specs,
        out_specs=tuple(out_specs),
        scratch_shapes=[
            pltpu.VMEM((hp, seq, HGRN_DV), F32),
            pltpu.VMEM((hp, seq, HGRN_DK), BF16),
            pltpu.VMEM((hp, seq, HGRN_DK), BF16),
            pltpu.VMEM((hp, seq // c, HGRN_DV, 2 * HGRN_DK), F32),
            pltpu.VMEM((hp, seq // c, 1, 2 * HGRN_DK), F32),
            pltpu.VMEM((hp, HGRN_DV, HGRN_DK), F32),
            pltpu.VMEM((hp, HGRN_DV, HGRN_DK), F32),
        ],
        compiler_params=_params(("arbitrary", "arbitrary")),
        name=f"hgrn_core_{seq}",
    )(*args)


def kernel(x_prompt, x_sample, cache_k, cache_v, state_hgrn, c, c_ctx, w_ada, b_ada, norm_g, w_ffn_in, w_ffn_out, w_attn_qkv, w_attn_o, attn_sink, w_conv_in, conv_w, conv_b, w_conv_out, w_hgrn_in, hgrn_lb, hgrn_norm_g, w_hgrn_out):
    cond = jnp.concatenate([c, c_ctx[None, :],
                            jnp.zeros((N_COND - DEC_BATCH - 1, D_MODEL), F32)], axis=0)
    mods = _ada_table(cond, w_ada, b_ada)
    cos, sin = _rope_tables()
    ck = cache_k.reshape(DEC_BATCH, -1, PAST_LEN, KV_DIM)
    cv = cache_v.reshape(DEC_BATCH, -1, PAST_LEN, KV_DIM)
    bf = lambda w: w.astype(BF16)
    xs = (x_sample.reshape(N_SAMPLE_TOK, D_MODEL), x_prompt.reshape(N_PROMPT_TOK, D_MODEL))
    ffn_w = (bf(w_ffn_in[0, 0]), bf(w_ffn_out[0, 0]))
    new_k, new_v, new_s = [], [], []
    for li in range(DEPTH):
        kind, j = li % N_MIXERS, li // N_MIXERS
        mixer = None
        (x,), ffn_w = _ffn(xs, mods, norm_g, ffn_w, (w_ffn_in, w_ffn_out, li, 1), li, 0)
        if kind == 0:
            q, k, v, kf, vf = _qkv(x, mods, norm_g, w_attn_qkv, li, j, cos, sin)
            o_s = _attn_latent(attn_sink[j], q, k, v, ck, cv, j)
            o_p = _attn_context(attn_sink[j], q, k, v)
            mixer = (o_s, o_p, w_attn_o, j)
            new_k.append(kf.reshape(BATCH, SEQ, ATTN_KV_HEADS, HEAD_DIM))
            new_v.append(vf.reshape(BATCH, SEQ, ATTN_KV_HEADS, HEAD_DIM))
        elif kind == 1:
            x = _conv_layer(x, mods, norm_g, w_conv_in, conv_w, conv_b, w_conv_out, li, j)
        else:
            z = _proj(x, mods, norm_g, w_hgrn_in, li, j, F32)
            o_s = _hgrn_core(z, hgrn_lb, hgrn_norm_g, state_hgrn, j=j, batch=DEC_BATCH,
                             seq=DEC_SEQ, row_off=0, li=li, emit_state=False)[0]
            o_p, st = _hgrn_core(z, hgrn_lb, hgrn_norm_g, None, j=j, batch=BATCH, seq=SEQ,
                                 row_off=N_SAMPLE_TOK, li=li, emit_state=True)
            mixer = (o_s, o_p, w_hgrn_out, j)
            new_s.append(st)
        last = li == DEPTH - 1
        nxt = None if last else (w_ffn_in, w_ffn_out, li + 1, 0)
        xs, ffn_w = _ffn((x,), mods, norm_g, ffn_w, nxt, li, 1, split_out=last, mixer=mixer)
    y_sample = xs[0].reshape(DEC_BATCH, DEC_SEQ, D_MODEL)
    y_prompt = xs[1].reshape(BATCH, SEQ, D_MODEL)
    return (y_prompt, y_sample, jnp.stack(new_k, axis=1), jnp.stack(new_v, axis=1),
            jnp.stack(new_s, axis=1))
```

```python
import functools
import math

import jax
import jax.numpy as jnp
import numpy as np
from jax import lax
from jax.experimental import pallas as pl
from jax.experimental.pallas import tpu as pltpu

D_MODEL = 1024
BATCH = 16
SEQ = 256
DEPTH = 4
DEC_BATCH = 8
DEC_SEQ = 1024
PAST_LEN = 256
GRID_W = 64
N_MIXERS = 3
N_MOD = 9
N_NORMS = 6
D_FF = 2816
FFN_RES = 0.5
ATTN_HEADS = 16
ATTN_KV_HEADS = 4
HEAD_DIM = 64
ATTN_DIM = ATTN_HEADS * HEAD_DIM
KV_DIM = ATTN_KV_HEADS * HEAD_DIM
BLOCK = 128
WINDOW = 128
ATTN_SCALE = HEAD_DIM ** -0.5
LOG2E = math.log2(math.e)
ROPE_BASE = 10000.0
ROPE_PAIRS_PER_AXIS = HEAD_DIM // 4
HGRN_HEADS = 8
HGRN_DK = 128
HGRN_DV = D_MODEL // HGRN_HEADS
HGRN_FD = HGRN_HEADS * HGRN_DK
EPS = 1e-6

N_SAMPLE_TOK = DEC_BATCH * DEC_SEQ
N_PROMPT_TOK = BATCH * SEQ
N_TOK = N_SAMPLE_TOK + N_PROMPT_TOK
N_COND = 16
CTX_COND = DEC_BATCH

LANES = 128
SUBLANES = 8
TM = 512
CONV_TM = 1024
FFN_SPLITS = (0, 1024, 2048, D_FF)
GLA_CHUNK = 128
GLA_ROWS_IN_FLIGHT = 2048
NEG_BIG = -1e30
VMEM_LIMIT = 56 * 1024 * 1024

F32 = jnp.float32
BF16 = jnp.bfloat16


def _rms(x):
    return x * lax.rsqrt(jnp.mean(x * x, axis=-1, keepdims=True) + EPS)


def _silu(x):
    return x * jax.nn.sigmoid(x)


def _dot(a, b):
    return jnp.dot(a, b, preferred_element_type=F32)


def _dot_nt(a, b):
    return lax.dot_general(a, b, (((1,), (1,)), ((), ())), preferred_element_type=F32)


def _params(sem):
    return pltpu.CompilerParams(dimension_semantics=sem, vmem_limit_bytes=VMEM_LIMIT)


def _resident(arr, *lead):
    shape = (None,) * len(lead) + arr.shape[len(lead):]
    idx = tuple(lead) + (0,) * (arr.ndim - len(lead))
    return pl.BlockSpec(shape, lambda *_: idx, pipeline_mode=pl.Buffered(1))


def _mods_spec(tm):
    per = DEC_SEQ // tm
    return pl.BlockSpec((None, N_MOD, D_MODEL), lambda i: (jnp.minimum(i // per, CTX_COND), 0, 0))


def _norm_spec(li):
    return pl.BlockSpec((None, N_NORMS, D_MODEL), lambda i: (li, 0, 0))


def _tok_spec(tm, n):
    return pl.BlockSpec((tm, n), lambda i: (i, 0))


def _split_specs(tm, n):
    ns = N_SAMPLE_TOK // tm
    return [pl.BlockSpec((tm, n), lambda i: (jnp.minimum(i, ns - 1), 0)),
            pl.BlockSpec((tm, n), lambda i: (jnp.maximum(i - ns, 0), 0))]


def _load_split(s_ref, p_ref, tm):
    return jnp.where(pl.program_id(0) < N_SAMPLE_TOK // tm, s_ref[...], p_ref[...])


def _store_split(s_ref, p_ref, val, tm):
    ns = N_SAMPLE_TOK // tm
    i = pl.program_id(0)

    @pl.when(i < ns)
    def _():
        s_ref[...] = val

    @pl.when(i >= ns)
    def _():
        p_ref[...] = val


def _ada_kernel(c_ref, w_ref, b_ref, o_ref):
    s = _silu(c_ref[...]).astype(BF16)
    o_ref[...] = _dot(s, w_ref[...].astype(BF16)) + b_ref[...]


ADA_COLS = N_MOD * D_MODEL
ADA_SIDE_COLS = ADA_COLS // (N_TOK // TM)


def _ada_table(cond, w_ada, b_ada, li):
    tn = 3 * D_MODEL
    out = pl.pallas_call(
        _ada_kernel,
        out_shape=jax.ShapeDtypeStruct((N_COND, ADA_COLS), F32),
        grid=(ADA_COLS // tn,),
        in_specs=[
            pl.BlockSpec((N_COND, D_MODEL), lambda j: (0, 0)),
            pl.BlockSpec((None, D_MODEL, tn), lambda j: (li, 0, j)),
            pl.BlockSpec((None, 1, tn), lambda j: (li, 0, j)),
        ],
        out_specs=pl.BlockSpec((N_COND, tn), lambda j: (0, j)),
        compiler_params=_params(("arbitrary",)),
        name="ada_table",
    )(cond, w_ada, b_ada)
    return out.reshape(N_COND, N_MOD, D_MODEL)


NEXT_IN_ROWS = 64
NEXT_OUT_ROWS = 128


def _ffn_kernel(*refs, k0, g0, split_in, split_out, has_mixer, has_next, has_ada):
    n_in = 2 if split_in else 1
    x_refs, refs = refs[:n_in], refs[n_in:]
    mix_refs = ()
    if has_mixer:
        mix_refs, refs = refs[:3], refs[3:]
    m_ref, g_ref, win_ref, wout_ref = refs[:4]
    refs = refs[4:]
    n_side = (2 if has_next else 0) + (3 if has_ada else 0)
    side_in, outs = refs[:n_side], refs[n_side:]
    n_x = 2 if split_out else 1
    o_refs, side_out = outs[:n_x], outs[n_x:]
    if has_next:
        side_out[0][...] = side_in[0][...].astype(BF16)
        side_out[1][...] = side_in[1][...].astype(BF16)
        side_in, side_out = side_in[2:], side_out[2:]
    if has_ada:
        c_ref, wa_ref, ba_ref = side_in
        side_out[0][...] = _dot(_silu(c_ref[...]).astype(BF16), wa_ref[...].astype(BF16)) + ba_ref[...]
    x = _load_split(*x_refs, TM) if split_in else x_refs[0][...]
    m = m_ref[...]
    g = g_ref[...]
    if has_mixer:
        as_ref, ap_ref, wo_ref = mix_refs
        mixed = _dot(_load_split(as_ref, ap_ref, TM), wo_ref[...].astype(BF16))
        x = x + m[5:6] * (_rms(mixed) * g[3:4])
    h = _rms(x) * g[g0:g0 + 1]
    h = (h * (1.0 + m[k0 + 1:k0 + 2]) + m[k0:k0 + 1]).astype(BF16)
    acc = None
    for lo, hi in zip(FFN_SPLITS[:-1], FFN_SPLITS[1:]):
        a = _dot(h, win_ref[:, lo:hi])
        b = _dot(h, win_ref[:, D_FF + lo:D_FF + hi])
        u = (_silu(a) * b).astype(BF16)
        part = _dot(u, wout_ref[lo:hi, :])
        acc = part if acc is None else acc + part
    y = _rms(acc) * g[g0 + 1:g0 + 2]
    res = x + FFN_RES * m[k0 + 2:k0 + 3] * y
    if split_out:
        _store_split(*o_refs, res, TM)
    else:
        o_refs[0][...] = res


def _ffn(xs, mods, norm_g, w_bf, nxt, li, which, split_out=False, mixer=None, ada=None):
    k0, g0 = (0, 0) if which == 0 else (6, 4)
    split_in = len(xs) == 2
    steps = N_TOK // TM
    x_specs = _split_specs(TM, D_MODEL) if split_in else [_tok_spec(TM, D_MODEL)]
    mix_specs, mix_args = [], []
    if mixer is not None:
        a_s, a_p, w_o, j = mixer
        mix_specs = _split_specs(TM, D_MODEL) + [_resident(w_o, j)]
        mix_args = [a_s, a_p, w_o]
    if split_out:
        out_shape = [jax.ShapeDtypeStruct((N_SAMPLE_TOK, D_MODEL), F32),
                     jax.ShapeDtypeStruct((N_PROMPT_TOK, D_MODEL), F32)]
        out_specs = _split_specs(TM, D_MODEL)
    else:
        out_shape = [jax.ShapeDtypeStruct((N_TOK, D_MODEL), F32)]
        out_specs = [_tok_spec(TM, D_MODEL)]
    nxt_specs, nxt_args = [], []
    if nxt is not None:
        w_in_all, w_out_all, li2, which2 = nxt
        last_in = D_MODEL // NEXT_IN_ROWS - 1
        last_out = D_FF // NEXT_OUT_ROWS - 1
        assert last_in < steps and last_out < steps
        nxt_specs = [
            pl.BlockSpec((None, None, NEXT_IN_ROWS, 2 * D_FF),
                         lambda i: (li2, which2, jnp.minimum(i, last_in), 0)),
            pl.BlockSpec((None, None, NEXT_OUT_ROWS, D_MODEL),
                         lambda i: (li2, which2, jnp.minimum(i, last_out), 0)),
        ]
        nxt_args = [w_in_all, w_out_all]
        out_shape += [jax.ShapeDtypeStruct((D_MODEL, 2 * D_FF), BF16),
                      jax.ShapeDtypeStruct((D_FF, D_MODEL), BF16)]
        out_specs += [pl.BlockSpec((NEXT_IN_ROWS, 2 * D_FF), lambda i: (jnp.minimum(i, last_in), 0)),
                      pl.BlockSpec((NEXT_OUT_ROWS, D_MODEL), lambda i: (jnp.minimum(i, last_out), 0))]
    ada_specs, ada_args = [], []
    if ada is not None:
        cond, w_ada, b_ada, li_ada = ada
        ada_specs = [
            pl.BlockSpec((N_COND, D_MODEL), lambda i: (0, 0)),
            pl.BlockSpec((None, D_MODEL, ADA_SIDE_COLS), lambda i: (li_ada, 0, i)),
            pl.BlockSpec((None, 1, ADA_SIDE_COLS), lambda i: (li_ada, 0, i)),
        ]
        ada_args = [cond, w_ada, b_ada]
        out_shape += [jax.ShapeDtypeStruct((N_COND, ADA_COLS), F32)]
        out_specs += [pl.BlockSpec((N_COND, ADA_SIDE_COLS), lambda i: (0, i))]
    outs = pl.pallas_call(
        functools.partial(_ffn_kernel, k0=k0, g0=g0, split_in=split_in, split_out=split_out,
                          has_mixer=mixer is not None, has_next=nxt is not None, has_ada=ada is not None),
        out_shape=tuple(out_shape),
        grid=(steps,),
        in_specs=x_specs + mix_specs + [
            _mods_spec(TM), _norm_spec(li), _resident(w_bf[0]), _resident(w_bf[1])] + nxt_specs + ada_specs,
        out_specs=tuple(out_specs),
        compiler_params=_params(("arbitrary",)),
        name=f"ffn{which}",
    )(*xs, *mix_args, mods, norm_g, *w_bf, *nxt_args, *ada_args)
    n_x = 2 if split_out else 1
    n_w = 2 if nxt is not None else 0
    mods_next = outs[n_x + n_w].reshape(N_COND, N_MOD, D_MODEL) if ada is not None else None
    return tuple(outs[:n_x]), tuple(outs[n_x:n_x + n_w]), mods_next


def _mixer_in(x_ref, m_ref, g_ref):
    m = m_ref[...]
    h = _rms(x_ref[...]) * g_ref[2:3, :]
    return (h * (1.0 + m[4:5]) + m[3:4]).astype(BF16)


PROJ_COLS = 512


def _proj_kernel(x_ref, m_ref, g_ref, w_ref, o_ref):
    h = _mixer_in(x_ref, m_ref, g_ref)
    for c0 in range(0, o_ref.shape[1], PROJ_COLS):
        o_ref[:, c0:c0 + PROJ_COLS] = _dot(h, w_ref[:, c0:c0 + PROJ_COLS].astype(BF16)).astype(o_ref.dtype)


def _proj(x, mods, norm_g, w, li, j, out_dtype):
    n = w.shape[-1]
    return pl.pallas_call(
        _proj_kernel,
        out_shape=jax.ShapeDtypeStruct((N_TOK, n), out_dtype),
        grid=(N_TOK // TM,),
        in_specs=[_tok_spec(TM, D_MODEL), _mods_spec(TM), _norm_spec(li), _resident(w, j)],
        out_specs=_tok_spec(TM, n),
        compiler_params=_params(("arbitrary",)),
        name="mixer_proj",
    )(x, mods, norm_g, w)


def _rope_cols(t, cos, sin):
    lane = lax.broadcasted_iota(jnp.int32, (t.shape[0], LANES), 1)
    low = (lane % HEAD_DIM) < (HEAD_DIM // 2)
    outs = []
    for c in range(t.shape[1] // LANES):
        tc = t[:, c * LANES:(c + 1) * LANES]
        partner = jnp.where(low, pltpu.roll(tc, LANES - HEAD_DIM // 2, 1),
                            pltpu.roll(tc, HEAD_DIM // 2, 1))
        outs.append(tc * cos + partner * sin)
    return jnp.concatenate(outs, axis=1)


def _qkv_kernel(x_ref, m_ref, g_ref, w_ref, cos_ref, sin_ref,
                q_ref, k_ref, v_ref, kf_ref, vf_ref):
    h = _mixer_in(x_ref, m_ref, g_ref)
    cos = cos_ref[...]
    sin = sin_ref[...]
    for c0 in range(0, ATTN_DIM, KV_DIM):
        q = _dot(h, w_ref[:, c0:c0 + KV_DIM].astype(BF16))
        q_ref[:, c0:c0 + KV_DIM] = (_rope_cols(q, cos, sin) * (ATTN_SCALE * LOG2E)).astype(BF16)
    k = _dot(h, w_ref[:, ATTN_DIM:ATTN_DIM + KV_DIM].astype(BF16))
    v = _dot(h, w_ref[:, ATTN_DIM + KV_DIM:].astype(BF16))

    @pl.when(pl.program_id(0) >= N_SAMPLE_TOK // TM)
    def _():
        kf_ref[...] = k
        vf_ref[...] = v

    k_ref[...] = _rope_cols(k, cos, sin).astype(BF16)
    v_ref[...] = v.astype(BF16)


def _rope_tables():
    pos = np.arange(DEC_SEQ)
    row = (pos // GRID_W).astype(np.float32)
    col = (pos % GRID_W).astype(np.float32)
    inv = (ROPE_BASE ** (-np.arange(ROPE_PAIRS_PER_AXIS, dtype=np.float32) / ROPE_PAIRS_PER_AXIS)).astype(np.float32)
    ang = jnp.asarray(np.concatenate([row[:, None] * inv, col[:, None] * inv], axis=-1).astype(np.float32))
    cos, sin = jnp.cos(ang), jnp.sin(ang)
    cos = jnp.concatenate([cos, cos, cos, cos], axis=-1)
    sin = jnp.concatenate([-sin, sin, -sin, sin], axis=-1)
    cos = jnp.concatenate([cos, jnp.ones((TM, LANES), F32)], axis=0)
    sin = jnp.concatenate([sin, jnp.zeros((TM, LANES), F32)], axis=0)
    return cos, sin


def _qkv(x, mods, norm_g, w, li, j, cos, sin):
    per = DEC_SEQ // TM
    ns = N_SAMPLE_TOK // TM
    tab = pl.BlockSpec((TM, LANES), lambda i: (jnp.where(i < ns, i % per, per), 0))
    cache = pl.BlockSpec((TM, KV_DIM), lambda i: (jnp.maximum(i - ns, 0), 0))
    return pl.pallas_call(
        _qkv_kernel,
        out_shape=(
            jax.ShapeDtypeStruct((N_TOK, ATTN_DIM), BF16),
            jax.ShapeDtypeStruct((N_TOK, KV_DIM), BF16),
            jax.ShapeDtypeStruct((N_TOK, KV_DIM), BF16),
            jax.ShapeDtypeStruct((N_PROMPT_TOK, KV_DIM), F32),
            jax.ShapeDtypeStruct((N_PROMPT_TOK, KV_DIM), F32),
        ),
        grid=(N_TOK // TM,),
        in_specs=[_tok_spec(TM, D_MODEL), _mods_spec(TM), _norm_spec(li), _resident(w, j), tab, tab],
        out_specs=(_tok_spec(TM, ATTN_DIM), _tok_spec(TM, KV_DIM), _tok_spec(TM, KV_DIM), cache, cache),
        compiler_params=_params(("arbitrary",)),
        name="attn_qkv",
    )(x, mods, norm_g, w, cos, sin)


N_HEAD_SLOTS = 2 * ATTN_KV_HEADS


def _fill_head_slots(x, k_dst, v_dst, row0, is_value):
    dst = v_dst if is_value else k_dst
    pad = 1.0 if is_value else 0.0
    rows = x.shape[0]
    low = lax.broadcasted_iota(jnp.int32, (rows, LANES), 1) < HEAD_DIM
    for g in range(ATTN_KV_HEADS):
        cg, pg = g // 2, g % 2
        xb = x[:, cg * LANES:(cg + 1) * LANES]
        xr = pltpu.roll(xb, HEAD_DIM, 1)
        x_lo, x_hi = (xb, xr) if pg == 0 else (xr, xb)
        dst[2 * g, pl.ds(row0, rows), :] = jnp.where(low, x_lo, pad).astype(BF16)
        dst[2 * g + 1, pl.ds(row0, rows), :] = jnp.where(low, pad, x_hi).astype(BF16)


def _attend_rows(q, key_blocks, sink_ref, o_ref, orow0, s_scr, p_scr):
    rows = q.shape[0]
    rid = lax.broadcasted_iota(jnp.int32, (2 * rows, 1), 0)
    low = lax.broadcasted_iota(jnp.int32, (2 * rows, LANES), 1) < HEAD_DIM
    sinks, maxes = [], []
    for slot in range(N_HEAD_SLOTS):
        g, side = slot // 2, slot % 2
        c0 = 2 * g * LANES
        qs = jnp.concatenate([q[:, c0:c0 + LANES], q[:, c0 + LANES:c0 + 2 * LANES]], axis=0)
        sk = jnp.where(rid < rows, sink_ref[4 * g + side], sink_ref[4 * g + 2 + side]) * LOG2E
        mx = sk
        col = 0
        for k_of, _, bias, n_keys in key_blocks:
            s = _dot_nt(qs, k_of(slot))
            if bias is not None:
                s = s + bias
            s_scr[slot, :, col:col + n_keys] = s
            mx = jnp.maximum(mx, jnp.max(s, axis=-1, keepdims=True))
            col += n_keys
        sinks.append(sk)
        maxes.append(mx)
    for slot in range(N_HEAD_SLOTS):
        p_scr[slot] = jnp.exp2(s_scr[slot] - maxes[slot]).astype(BF16)
    for g in range(ATTN_KV_HEADS):
        res = []
        for side in range(2):
            slot = 2 * g + side
            part = None
            col = 0
            for _, v_of, _, n_keys in key_blocks:
                pv = _dot(p_scr[slot, :, col:col + n_keys], v_of(slot))
                part = pv if part is None else part + pv
                col += n_keys
            den = pltpu.roll(part, HEAD_DIM, 1) + jnp.exp2(sinks[slot] - maxes[slot])
            res.append(part / den)
        out = jnp.where(low, res[0], res[1]).astype(o_ref.dtype)
        c0 = 2 * g * LANES
        o_ref[pl.ds(orow0, rows), c0:c0 + LANES] = out[:rows]
        o_ref[pl.ds(orow0, rows), c0 + LANES:c0 + 2 * LANES] = out[rows:]


def _attn_lat_kernel(sink_ref, q_ref, k_ref, v_ref, ck_ref, cv_ref, o_ref,
                     kl_ref, vl_ref, kc_ref, vc_ref, s_scr, p_scr):
    nb = DEC_SEQ // BLOCK
    zeros = jnp.zeros((BLOCK, LANES), BF16)
    for slot in range(N_HEAD_SLOTS):
        for ref in (kl_ref, vl_ref):
            ref[slot, pl.ds(0, BLOCK), :] = zeros
            ref[slot, pl.ds(DEC_SEQ + BLOCK, BLOCK), :] = zeros
    _fill_head_slots(k_ref[...].astype(F32), kl_ref, vl_ref, BLOCK, False)
    _fill_head_slots(v_ref[...].astype(F32), kl_ref, vl_ref, BLOCK, True)
    _fill_head_slots(ck_ref[...], kc_ref, vc_ref, 0, False)
    _fill_head_slots(cv_ref[...], kc_ref, vc_ref, 0, True)
    n_band = 3 * BLOCK
    qi = lax.broadcasted_iota(jnp.int32, (2 * BLOCK, n_band), 0) % BLOCK
    u = lax.broadcasted_iota(jnp.int32, (2 * BLOCK, n_band), 1)
    in_window = jnp.abs(qi + BLOCK - u) <= WINDOW

    def body(n, carry):
        r0 = pl.multiple_of(n * BLOCK, BLOCK)
        kpos = (n - 1) * BLOCK + u
        ok = in_window & (kpos >= 0) & (kpos < nb * BLOCK)
        bias = jnp.where(ok, 0.0, NEG_BIG)
        band = (lambda s: kl_ref[s, pl.ds(r0, n_band), :], lambda s: vl_ref[s, pl.ds(r0, n_band), :],
                bias, n_band)
        ctx = (lambda s: kc_ref[s], lambda s: vc_ref[s], None, PAST_LEN)
        _attend_rows(q_ref[pl.ds(r0, BLOCK), :], [band, ctx], sink_ref, o_ref, r0, s_scr, p_scr)
        return carry

    lax.fori_loop(0, nb, body, 0)


def _attn_latent(sink, q, k, v, ck, cv, j):
    seq = lambda n: pl.BlockSpec((DEC_SEQ, n), lambda b: (b, 0))
    ctx = pl.BlockSpec((None, None, PAST_LEN, KV_DIM), lambda b: (b, j, 0, 0))
    return pl.pallas_call(
        _attn_lat_kernel,
        out_shape=jax.ShapeDtypeStruct((N_SAMPLE_TOK, ATTN_DIM), BF16),
        grid=(DEC_BATCH,),
        in_specs=[pl.BlockSpec(memory_space=pltpu.SMEM), seq(ATTN_DIM), seq(KV_DIM), seq(KV_DIM), ctx, ctx],
        out_specs=seq(ATTN_DIM),
        scratch_shapes=[
            pltpu.VMEM((N_HEAD_SLOTS, DEC_SEQ + 2 * BLOCK, LANES), BF16),
            pltpu.VMEM((N_HEAD_SLOTS, DEC_SEQ + 2 * BLOCK, LANES), BF16),
            pltpu.VMEM((N_HEAD_SLOTS, PAST_LEN, LANES), BF16),
            pltpu.VMEM((N_HEAD_SLOTS, PAST_LEN, LANES), BF16),
            pltpu.VMEM((N_HEAD_SLOTS, 2 * BLOCK, 3 * BLOCK + PAST_LEN), F32),
            pltpu.VMEM((N_HEAD_SLOTS, 2 * BLOCK, 3 * BLOCK + PAST_LEN), BF16),
        ],
        compiler_params=_params(("arbitrary",)),
        name="attn_latent",
    )(sink, q, k, v, ck, cv)


def _attn_ctx_kernel(sink_ref, q_ref, k_ref, v_ref, o_ref, ks_ref, vs_ref, s_scr, p_scr):
    _fill_head_slots(k_ref[...].astype(F32), ks_ref, vs_ref, 0, False)
    _fill_head_slots(v_ref[...].astype(F32), ks_ref, vs_ref, 0, True)
    blk = (lambda s: ks_ref[s], lambda s: vs_ref[s], None, SEQ)
    _attend_rows(q_ref[...], [blk], sink_ref, o_ref, 0, s_scr, p_scr)


def _attn_context(sink, q, k, v):
    off = N_SAMPLE_TOK // SEQ
    row = lambda n: pl.BlockSpec((SEQ, n), lambda b: (off + b, 0))
    return pl.pallas_call(
        _attn_ctx_kernel,
        out_shape=jax.ShapeDtypeStruct((N_PROMPT_TOK, ATTN_DIM), BF16),
        grid=(BATCH,),
        in_specs=[pl.BlockSpec(memory_space=pltpu.SMEM), row(ATTN_DIM), row(KV_DIM), row(KV_DIM)],
        out_specs=pl.BlockSpec((SEQ, ATTN_DIM), lambda b: (b, 0)),
        scratch_shapes=[pltpu.VMEM((N_HEAD_SLOTS, SEQ, LANES), BF16),
                        pltpu.VMEM((N_HEAD_SLOTS, SEQ, LANES), BF16),
                        pltpu.VMEM((N_HEAD_SLOTS, 2 * SEQ, SEQ), F32),
                        pltpu.VMEM((N_HEAD_SLOTS, 2 * SEQ, SEQ), BF16)],
        compiler_params=_params(("arbitrary",)),
        name="attn_context",
    )(sink, q, k, v)


CONV_COLS = 256


def _conv_kernel(x_ref, m_ref, g_ref, win_ref, cw_ref, cb_ref, wout_ref, o_ref, a_ref):
    i = pl.program_id(0)
    h = _mixer_in(x_ref, m_ref, g_ref)
    tm = h.shape[0]
    seq = jnp.where(i < N_SAMPLE_TOK // tm, DEC_SEQ, SEQ)
    pos = lax.broadcasted_iota(jnp.int32, (tm, CONV_COLS), 0) & (seq - 1)
    first = pos == 0
    last = pos == seq - 1
    cw = cw_ref[...]
    cb = cb_ref[...]
    for c in range(D_MODEL // CONV_COLS):
        cols = slice(c * CONV_COLS, (c + 1) * CONV_COLS)
        part = lambda j: _dot(
            h, win_ref[:, j * D_MODEL + c * CONV_COLS:j * D_MODEL + (c + 1) * CONV_COLS].astype(BF16))
        z = part(1) * part(2)
        zl = jnp.where(first, 0.0, pltpu.roll(z, 1, 0))
        zr = jnp.where(last, 0.0, pltpu.roll(z, tm - 1, 0))
        conv = cb[:, cols] + zl * cw[0:1, cols] + z * cw[1:2, cols] + zr * cw[2:3, cols]
        a_ref[:, cols] = (part(0) * conv).astype(BF16)
    y = _rms(_dot(a_ref[...], wout_ref[...].astype(BF16))) * g_ref[3:4, :]
    o_ref[...] = x_ref[...] + m_ref[5:6, :] * y


def _conv_layer(x, mods, norm_g, w_in, cw, cb, w_out, li, j):
    tm = CONV_TM
    return pl.pallas_call(
        _conv_kernel,
        out_shape=jax.ShapeDtypeStruct((N_TOK, D_MODEL), F32),
        grid=(N_TOK // tm,),
        in_specs=[
            _tok_spec(tm, D_MODEL), _mods_spec(tm), _norm_spec(li), _resident(w_in, j),
            pl.BlockSpec((None, 3, D_MODEL), lambda i: (j, 0, 0)),
            pl.BlockSpec((None, 1, D_MODEL), lambda i: (j, 0, 0)),
            _resident(w_out, j),
        ],
        out_specs=_tok_spec(tm, D_MODEL),
        scratch_shapes=[pltpu.VMEM((tm, D_MODEL), BF16)],
        compiler_params=_params(("arbitrary",)),
        name="conv_layer",
    )(x, mods, norm_g, w_in, cw, cb.reshape(-1, 1, D_MODEL), w_out)


def _block_row(x, w, off):
    c = x.shape[0]
    blk = 2 * w
    row = lambda r: jnp.broadcast_to(x[r:r + 1, :], (max(blk, SUBLANES), LANES))
    if blk >= SUBLANES:
        return jnp.concatenate([row(j * blk + off) for j in range(c // blk)], axis=0)
    if blk == 4:
        upper = lax.broadcasted_iota(jnp.int32, (SUBLANES, LANES), 0) < 4
        return jnp.concatenate(
            [jnp.where(upper, row(SUBLANES * v + off), row(SUBLANES * v + 4 + off))
             for v in range(c // SUBLANES)], axis=0)
    odd = (lax.broadcasted_iota(jnp.int32, (c, LANES), 0) & 1) != 0
    if off == 0:
        return jnp.where(odd, pltpu.roll(x, 1, 0), x)
    return jnp.where(odd, x, pltpu.roll(x, c - 1, 0))


def _split3(x):
    hi = x.astype(BF16)
    r = x - hi.astype(F32)
    mid = r.astype(BF16)
    lo = (r - mid.astype(F32)).astype(BF16)
    return jnp.concatenate([hi, mid, lo], axis=0)


def _hgrn_kernel(*refs, seq, li, hp, has_init, emit_state):
    lb_ref, lv_ref, zq_ref, zi_ref, zf_ref, zb_ref, zg_ref, ng_ref = refs[:8]
    refs = refs[8:]
    s0_ref = None
    if has_init:
        s0_ref, refs = refs[0], refs[1:]
    o_ref, refs = refs[0], refs[1:]
    sout_ref = None
    if emit_state:
        sout_ref, refs = refs[0], refs[1:]
    acc_ref, qf_ref, qb_ref, kv_ref, dec_ref, stf_ref, stb_ref = refs
    c = GLA_CHUNK
    nc = seq // c

    def lower_bound(x):
        e = jnp.exp(x - jnp.max(x, axis=0, keepdims=True))
        pr = e / jnp.sum(e, axis=0, keepdims=True)
        return jnp.sum(pr[1:li + 1], axis=0, keepdims=True)

    lbf, lbb = lower_bound(lb_ref[0]), lower_bound(lb_ref[1])
    ri = lax.broadcasted_iota(jnp.int32, (c, c), 0)
    ci = lax.broadcasted_iota(jnp.int32, (c, c), 1)
    tril = jnp.where(ci <= ri, 1.0, 0.0).astype(BF16)
    triu = jnp.where(ci >= ri, 1.0, 0.0).astype(BF16)
    tril3 = jnp.concatenate([tril, tril, tril], axis=1)
    triu3 = jnp.concatenate([triu, triu, triu], axis=1)
    lv = lv_ref[...]
    rowi = lax.broadcasted_iota(jnp.int32, (c, LANES), 0)

    def chunk_local(n, hd):
        rows = pl.ds(pl.multiple_of(n * c, c), c)
        cols = slice(hd * LANES, (hd + 1) * LANES)
        q = _silu(zq_ref[rows, cols]) * (HGRN_DK ** -0.5)
        zi = zi_ref[rows, cols]
        v = zi.astype(BF16)
        ff = lbf[:, cols] + (1.0 - lbf[:, cols]) * jax.nn.sigmoid(zf_ref[rows, cols])
        fb = lbb[:, cols] + (1.0 - lbb[:, cols]) * jax.nn.sigmoid(zb_ref[rows, cols])
        kf = 1.0 - ff
        kb = 1.0 - fb
        pre = _dot(tril3, _split3(jnp.log2(ff)))
        suf = _dot(triu3, _split3(jnp.log2(fb)))
        attn = jnp.where(lv == 0, _dot_nt(q.astype(BF16), (kf + kb).astype(BF16)), 0.0)
        odd = (rowi & 1) != 0
        qh = (q * jnp.where(odd, ff, fb)).astype(BF16)
        kh = jnp.where(odd, kb, kf).astype(BF16)
        attn = jnp.where(lv == 1, _dot_nt(qh, kh), attn)
        w, level = 2, 2
        while w < c:
            d_f = pre - _block_row(pre, w, w - 1)
            d_b = suf - _block_row(suf, w, w)
            xq = jnp.minimum(d_f, d_b)
            xk = xq - (d_f + d_b)
            second = (rowi & w) != 0
            qh = (q * jnp.exp2(xq)).astype(BF16)
            kh = (jnp.where(second, kb, kf) * jnp.exp2(xk)).astype(BF16)
            attn = jnp.where(lv == level, _dot_nt(qh, kh), attn)
            w *= 2
            level += 1
        acc_ref[hd, rows, :] = _dot(attn.astype(BF16), v)
        tot_f = pre[c - 1:c, :]
        tot_b = suf[0:1, :]
        qf_ref[hd, rows, :] = (q * jnp.exp2(pre)).astype(BF16)
        qb_ref[hd, rows, :] = (q * jnp.exp2(suf)).astype(BF16)
        kt = jnp.concatenate([(kf * jnp.exp2(tot_f - pre)).astype(BF16),
                              (kb * jnp.exp2(tot_b - suf)).astype(BF16)], axis=1)
        kv_ref[hd, n] = _dot(zi.T.astype(BF16), kt)
        dec_ref[hd, n] = jnp.exp2(jnp.concatenate([tot_f, tot_b], axis=1))

    def local_body(n, carry):
        for hd in range(hp):
            chunk_local(n, hd)
        return carry

    lax.fori_loop(0, nc, local_body, 0, unroll=min(nc, GLA_ROWS_IN_FLIGHT // (c * hp)))

    for hd in range(hp):
        if has_init:
            stf_ref[hd] = s0_ref[0, hd].T
            stb_ref[hd] = s0_ref[1, hd].T
        else:
            stf_ref[hd] = jnp.zeros((HGRN_DV, HGRN_DK), F32)
            stb_ref[hd] = jnp.zeros((HGRN_DV, HGRN_DK), F32)

    def carry_state(i, carry):
        n, m = i, nc - 1 - i
        rn = pl.ds(pl.multiple_of(n * c, c), c)
        rm = pl.ds(pl.multiple_of(m * c, c), c)
        for hd in range(hp):
            stf = stf_ref[hd]
            stb = stb_ref[hd]
            acc_ref[hd, rn, :] += _dot_nt(qf_ref[hd, rn, :], stf.astype(BF16))
            acc_ref[hd, rm, :] += _dot_nt(qb_ref[hd, rm, :], stb.astype(BF16))
            stf_ref[hd] = stf * dec_ref[hd, n][:, :HGRN_DK] + kv_ref[hd, n][:, :HGRN_DK]
            stb_ref[hd] = stb * dec_ref[hd, m][:, HGRN_DK:] + kv_ref[hd, m][:, HGRN_DK:]
        return carry

    lax.fori_loop(0, nc, carry_state, 0, unroll=True)
    for hd in range(hp):
        cols = slice(hd * LANES, (hd + 1) * LANES)
        o_ref[:, cols] = (_rms(acc_ref[hd]) * ng_ref[...] * _silu(zg_ref[:, cols])).astype(o_ref.dtype)
        if emit_state:
            sout_ref[0, hd] = stf_ref[hd].T
            sout_ref[1, hd] = stb_ref[hd].T


def _pair_levels():
    t = np.arange(GLA_CHUNK)
    x = t[:, None] ^ t[None, :]
    return jnp.asarray(np.where(x > 0, np.floor(np.log2(np.maximum(x, 1))) + 1, 0).astype(np.int32))


def _hgrn_core(z, lb, ng, s0, *, j, batch, seq, row_off, li, emit_state):
    has_init = s0 is not None
    blk0 = row_off // seq
    fd = HGRN_FD // LANES
    dm = D_MODEL // LANES
    c = GLA_CHUNK
    hp = max(1, GLA_ROWS_IN_FLIGHT // seq)
    width = hp * LANES

    def col(c0):
        return pl.BlockSpec((seq, width), lambda b, h: (blk0 + b, c0 // hp + h))

    in_specs = [
        pl.BlockSpec((2, DEPTH, width), lambda b, h: (0, 0, h)),
        pl.BlockSpec((c, c), lambda b, h: (0, 0)),
        col(0), col(fd), col(fd + dm), col(2 * fd + dm), col(3 * fd + dm),
        pl.BlockSpec((None, 1, HGRN_DV), lambda b, h: (j, 0, 0)),
    ]
    args = [lb, _pair_levels(), z, z, z, z, z, ng.reshape(-1, 1, HGRN_DV)]
    if has_init:
        in_specs.append(pl.BlockSpec((None, None, 2, hp, HGRN_DK, HGRN_DV),
                                     lambda b, h: (b, j, 0, h, 0, 0)))
        args.append(s0)
    out_shape = [jax.ShapeDtypeStruct((batch * seq, D_MODEL), BF16)]
    out_specs = [pl.BlockSpec((seq, width), lambda b, h: (b, h))]
    if emit_state:
        out_shape.append(jax.ShapeDtypeStruct((batch, 2, HGRN_HEADS, HGRN_DK, HGRN_DV), F32))
        out_specs.append(pl.BlockSpec((None, 2, hp, HGRN_DK, HGRN_DV), lambda b, h: (b, 0, h, 0, 0)))
    return pl.pallas_call(
        functools.partial(_hgrn_kernel, seq=seq, li=li, hp=hp, has_init=has_init, emit_state=emit_state),
        out_shape=tuple(out_shape),
        grid=(batch, HGRN_HEADS // hp),
        in_specs=in_specs,
        out_specs=tuple(out_specs),
        scratch_shapes=[
            pltpu.VMEM((hp, seq, HGRN_DV), F32),
            pltpu.VMEM((hp, seq, HGRN_DK), BF16),
            pltpu.VMEM((hp, seq, HGRN_DK), BF16),
            pltpu.VMEM((hp, seq // c, HGRN_DV, 2 * HGRN_DK), F32),
            pltpu.VMEM((hp, seq // c, 1, 2 * HGRN_DK), F32),
            pltpu.VMEM((hp, HGRN_DV, HGRN_DK), F32),
            pltpu.VMEM((hp, HGRN_DV, HGRN_DK), F32),
        ],
        compiler_params=_params(("arbitrary", "arbitrary")),
        name=f"hgrn_core_{seq}",
    )(*args)


def kernel(x_prompt, x_sample, cache_k, cache_v, state_hgrn, c, c_ctx, w_ada, b_ada, norm_g, w_ffn_in, w_ffn_out, w_attn_qkv, w_attn_o, attn_sink, w_conv_in, conv_w, conv_b, w_conv_out, w_hgrn_in, hgrn_lb, hgrn_norm_g, w_hgrn_out):
    cond = jnp.concatenate([c, c_ctx[None, :],
                            jnp.zeros((N_COND - DEC_BATCH - 1, D_MODEL), F32)], axis=0)
    b_ada = b_ada.reshape(DEPTH, 1, ADA_COLS)
    mods = _ada_table(cond, w_ada, b_ada, 0)
    cos, sin = _rope_tables()
    ck = cache_k.reshape(DEC_BATCH, -1, PAST_LEN, KV_DIM)
    cv = cache_v.reshape(DEC_BATCH, -1, PAST_LEN, KV_DIM)
    bf = lambda w: w.astype(BF16)
    xs = (x_sample.reshape(N_SAMPLE_TOK, D_MODEL), x_prompt.reshape(N_PROMPT_TOK, D_MODEL))
    ffn_w = (bf(w_ffn_in[0, 0]), bf(w_ffn_out[0, 0]))
    new_k, new_v, new_s = [], [], []
    for li in range(DEPTH):
        kind, j = li % N_MIXERS, li // N_MIXERS
        last = li == DEPTH - 1
        mixer = None
        ada = None if last else (cond, w_ada, b_ada, li + 1)
        (x,), ffn_w, mods_next = _ffn(xs, mods, norm_g, ffn_w, (w_ffn_in, w_ffn_out, li, 1), li, 0, ada=ada)
        if kind == 0:
            q, k, v, kf, vf = _qkv(x, mods, norm_g, w_attn_qkv, li, j, cos, sin)
            o_s = _attn_latent(attn_sink[j], q, k, v, ck, cv, j)
            o_p = _attn_context(attn_sink[j], q, k, v)
            mixer = (o_s, o_p, w_attn_o, j)
            new_k.append(kf.reshape(BATCH, SEQ, ATTN_KV_HEADS, HEAD_DIM))
            new_v.append(vf.reshape(BATCH, SEQ, ATTN_KV_HEADS, HEAD_DIM))
        elif kind == 1:
            x = _conv_layer(x, mods, norm_g, w_conv_in, conv_w, conv_b, w_conv_out, li, j)
        else:
            z = _proj(x, mods, norm_g, w_hgrn_in, li, j, F32)
            o_s = _hgrn_core(z, hgrn_lb, hgrn_norm_g, state_hgrn, j=j, batch=DEC_BATCH,
                             seq=DEC_SEQ, row_off=0, li=li, emit_state=False)[0]
            o_p, st = _hgrn_core(z, hgrn_lb, hgrn_norm_g, None, j=j, batch=BATCH, seq=SEQ,
                                 row_off=N_SAMPLE_TOK, li=li, emit_state=True)
            mixer = (o_s, o_p, w_hgrn_out, j)
            new_s.append(st)
        nxt = None if last else (w_ffn_in, w_ffn_out, li + 1, 0)
        xs, ffn_w, _ = _ffn((x,), mods, norm_g, ffn_w, nxt, li, 1, split_out=last, mixer=mixer)
        mods = mods_next
    y_sample = xs[0].reshape(DEC_BATCH, DEC_SEQ, D_MODEL)
    y_prompt = xs[1].reshape(BATCH, SEQ, D_MODEL)
    return (y_prompt, y_sample, jnp.stack(new_k, axis=1), jnp.stack(new_v, axis=1),
            jnp.stack(new_s, axis=1))
```

```python
import functools
import math

import jax
import jax.numpy as jnp
import numpy as np
from jax import lax
from jax.experimental import pallas as pl
from jax.experimental.pallas import tpu as pltpu

D_MODEL = 1024
BATCH = 16
SEQ = 256
DEPTH = 4
DEC_BATCH = 8
DEC_SEQ = 1024
PAST_LEN = 256
GRID_W = 64
N_MIXERS = 3
N_MOD = 9
N_NORMS = 6
D_FF = 2816
FFN_RES = 0.5
ATTN_HEADS = 16
ATTN_KV_HEADS = 4
HEAD_DIM = 64
ATTN_DIM = ATTN_HEADS * HEAD_DIM
KV_DIM = ATTN_KV_HEADS * HEAD_DIM
BLOCK = 128
WINDOW = 128
ATTN_SCALE = HEAD_DIM ** -0.5
LOG2E = math.log2(math.e)
ROPE_BASE = 10000.0
ROPE_PAIRS_PER_AXIS = HEAD_DIM // 4
HGRN_HEADS = 8
HGRN_DK = 128
HGRN_DV = D_MODEL // HGRN_HEADS
HGRN_FD = HGRN_HEADS * HGRN_DK
EPS = 1e-6

N_SAMPLE_TOK = DEC_BATCH * DEC_SEQ
N_PROMPT_TOK = BATCH * SEQ
N_TOK = N_SAMPLE_TOK + N_PROMPT_TOK
N_COND = 16
CTX_COND = DEC_BATCH

LANES = 128
SUBLANES = 8
TM = 512
CONV_TM = 1024
FFN_SPLITS = (0, 1024, 2048, D_FF)
GLA_CHUNK = 128
GLA_ROWS_IN_FLIGHT = 2048
NEG_BIG = -1e30
VMEM_LIMIT = 56 * 1024 * 1024

F32 = jnp.float32
BF16 = jnp.bfloat16


def _rms(x):
    return x * lax.rsqrt(jnp.mean(x * x, axis=-1, keepdims=True) + EPS)


def _silu(x):
    return x * jax.nn.sigmoid(x)


def _dot(a, b):
    return jnp.dot(a, b, preferred_element_type=F32)


def _dot_nt(a, b):
    return lax.dot_general(a, b, (((1,), (1,)), ((), ())), preferred_element_type=F32)


def _params(sem):
    return pltpu.CompilerParams(dimension_semantics=sem, vmem_limit_bytes=VMEM_LIMIT)


def _resident(arr, *lead):
    shape = (None,) * len(lead) + arr.shape[len(lead):]
    idx = tuple(lead) + (0,) * (arr.ndim - len(lead))
    return pl.BlockSpec(shape, lambda *_: idx, pipeline_mode=pl.Buffered(1))


def _mods_spec(tm):
    per = DEC_SEQ // tm
    return pl.BlockSpec((None, N_MOD, D_MODEL), lambda i: (jnp.minimum(i // per, CTX_COND), 0, 0))


def _norm_spec(li):
    return pl.BlockSpec((None, N_NORMS, D_MODEL), lambda i: (li, 0, 0))


def _tok_spec(tm, n):
    return pl.BlockSpec((tm, n), lambda i: (i, 0))


def _split_specs(tm, n):
    ns = N_SAMPLE_TOK // tm
    return [pl.BlockSpec((tm, n), lambda i: (jnp.minimum(i, ns - 1), 0)),
            pl.BlockSpec((tm, n), lambda i: (jnp.maximum(i - ns, 0), 0))]


def _load_split(s_ref, p_ref, tm):
    return jnp.where(pl.program_id(0) < N_SAMPLE_TOK // tm, s_ref[...], p_ref[...])


def _store_split(s_ref, p_ref, val, tm):
    ns = N_SAMPLE_TOK // tm
    i = pl.program_id(0)

    @pl.when(i < ns)
    def _():
        s_ref[...] = val

    @pl.when(i >= ns)
    def _():
        p_ref[...] = val


def _ada_kernel(c_ref, w_ref, b_ref, o_ref):
    s = _silu(c_ref[...]).astype(BF16)
    o_ref[...] = _dot(s, w_ref[...].astype(BF16)) + b_ref[...]


ADA_COLS = N_MOD * D_MODEL
ADA_SIDE_COLS = ADA_COLS // (N_TOK // TM)


def _ada_table(cond, w_ada, b_ada, li):
    tn = 3 * D_MODEL
    out = pl.pallas_call(
        _ada_kernel,
        out_shape=jax.ShapeDtypeStruct((N_COND, ADA_COLS), F32),
        grid=(ADA_COLS // tn,),
        in_specs=[
            pl.BlockSpec((N_COND, D_MODEL), lambda j: (0, 0)),
            pl.BlockSpec((None, D_MODEL, tn), lambda j: (li, 0, j)),
            pl.BlockSpec((None, 1, tn), lambda j: (li, 0, j)),
        ],
        out_specs=pl.BlockSpec((N_COND, tn), lambda j: (0, j)),
        compiler_params=_params(("arbitrary",)),
        name="ada_table",
    )(cond, w_ada, b_ada)
    return out.reshape(N_COND, N_MOD, D_MODEL)


NEXT_IN_ROWS = 64
NEXT_OUT_ROWS = 128


def _ffn_kernel(*refs, k0, g0, split_in, split_out, has_mixer, has_next, has_ada):
    n_in = 2 if split_in else 1
    x_refs, refs = refs[:n_in], refs[n_in:]
    mix_refs = ()
    if has_mixer:
        mix_refs, refs = refs[:3], refs[3:]
    m_ref, g_ref, win_ref, wout_ref = refs[:4]
    refs = refs[4:]
    n_side = (2 if has_next else 0) + (3 if has_ada else 0)
    side_in, outs = refs[:n_side], refs[n_side:]
    n_x = 2 if split_out else 1
    o_refs, side_out = outs[:n_x], outs[n_x:]
    if has_next:
        side_out[0][...] = side_in[0][...].astype(BF16)
        side_out[1][...] = side_in[1][...].astype(BF16)
        side_in, side_out = side_in[2:], side_out[2:]
    if has_ada:
        c_ref, wa_ref, ba_ref = side_in
        side_out[0][...] = _dot(_silu(c_ref[...]).astype(BF16), wa_ref[...].astype(BF16)) + ba_ref[...]
    x = _load_split(*x_refs, TM) if split_in else x_refs[0][...]
    m = m_ref[...]
    g = g_ref[...]
    if has_mixer:
        as_ref, ap_ref, wo_ref = mix_refs
        mixed = _dot(_load_split(as_ref, ap_ref, TM), wo_ref[...].astype(BF16))
        x = x + m[5:6] * (_rms(mixed) * g[3:4])
    h = _rms(x) * g[g0:g0 + 1]
    h = (h * (1.0 + m[k0 + 1:k0 + 2]) + m[k0:k0 + 1]).astype(BF16)
    acc = None
    for lo, hi in zip(FFN_SPLITS[:-1], FFN_SPLITS[1:]):
        a = _dot(h, win_ref[:, lo:hi])
        b = _dot(h, win_ref[:, D_FF + lo:D_FF + hi])
        u = (_silu(a) * b).astype(BF16)
        part = _dot(u, wout_ref[lo:hi, :])
        acc = part if acc is None else acc + part
    y = _rms(acc) * g[g0 + 1:g0 + 2]
    res = x + FFN_RES * m[k0 + 2:k0 + 3] * y
    if split_out:
        _store_split(*o_refs, res, TM)
    else:
        o_refs[0][...] = res


def _ffn(xs, mods, norm_g, w_bf, nxt, li, which, split_out=False, mixer=None, ada=None):
    k0, g0 = (0, 0) if which == 0 else (6, 4)
    split_in = len(xs) == 2
    steps = N_TOK // TM
    x_specs = _split_specs(TM, D_MODEL) if split_in else [_tok_spec(TM, D_MODEL)]
    mix_specs, mix_args = [], []
    if mixer is not None:
        a_s, a_p, w_o, j = mixer
        mix_specs = _split_specs(TM, D_MODEL) + [_resident(w_o, j)]
        mix_args = [a_s, a_p, w_o]
    if split_out:
        out_shape = [jax.ShapeDtypeStruct((N_SAMPLE_TOK, D_MODEL), F32),
                     jax.ShapeDtypeStruct((N_PROMPT_TOK, D_MODEL), F32)]
        out_specs = _split_specs(TM, D_MODEL)
    else:
        out_shape = [jax.ShapeDtypeStruct((N_TOK, D_MODEL), F32)]
        out_specs = [_tok_spec(TM, D_MODEL)]
    nxt_specs, nxt_args = [], []
    if nxt is not None:
        w_in_all, w_out_all, li2, which2 = nxt
        last_in = D_MODEL // NEXT_IN_ROWS - 1
        last_out = D_FF // NEXT_OUT_ROWS - 1
        assert last_in < steps and last_out < steps
        nxt_specs = [
            pl.BlockSpec((None, None, NEXT_IN_ROWS, 2 * D_FF),
                         lambda i: (li2, which2, jnp.minimum(i, last_in), 0)),
            pl.BlockSpec((None, None, NEXT_OUT_ROWS, D_MODEL),
                         lambda i: (li2, which2, jnp.minimum(i, last_out), 0)),
        ]
        nxt_args = [w_in_all, w_out_all]
        out_shape += [jax.ShapeDtypeStruct((D_MODEL, 2 * D_FF), BF16),
                      jax.ShapeDtypeStruct((D_FF, D_MODEL), BF16)]
        out_specs += [pl.BlockSpec((NEXT_IN_ROWS, 2 * D_FF), lambda i: (jnp.minimum(i, last_in), 0)),
                      pl.BlockSpec((NEXT_OUT_ROWS, D_MODEL), lambda i: (jnp.minimum(i, last_out), 0))]
    ada_specs, ada_args = [], []
    if ada is not None:
        cond, w_ada, b_ada, li_ada = ada
        ada_specs = [
            pl.BlockSpec((N_COND, D_MODEL), lambda i: (0, 0)),
            pl.BlockSpec((None, D_MODEL, ADA_SIDE_COLS), lambda i: (li_ada, 0, i)),
            pl.BlockSpec((None, 1, ADA_SIDE_COLS), lambda i: (li_ada, 0, i)),
        ]
        ada_args = [cond, w_ada, b_ada]
        out_shape += [jax.ShapeDtypeStruct((N_COND, ADA_COLS), F32)]
        out_specs += [pl.BlockSpec((N_COND, ADA_SIDE_COLS), lambda i: (0, i))]
    outs = pl.pallas_call(
        functools.partial(_ffn_kernel, k0=k0, g0=g0, split_in=split_in, split_out=split_out,
                          has_mixer=mixer is not None, has_next=nxt is not None, has_ada=ada is not None),
        out_shape=tuple(out_shape),
        grid=(steps,),
        in_specs=x_specs + mix_specs + [
            _mods_spec(TM), _norm_spec(li), _resident(w_bf[0]), _resident(w_bf[1])] + nxt_specs + ada_specs,
        out_specs=tuple(out_specs),
        compiler_params=_params(("arbitrary",)),
        name=f"ffn{which}",
    )(*xs, *mix_args, mods, norm_g, *w_bf, *nxt_args, *ada_args)
    n_x = 2 if split_out else 1
    n_w = 2 if nxt is not None else 0
    mods_next = outs[n_x + n_w].reshape(N_COND, N_MOD, D_MODEL) if ada is not None else None
    return tuple(outs[:n_x]), tuple(outs[n_x:n_x + n_w]), mods_next


def _mixer_in(x_ref, m_ref, g_ref):
    m = m_ref[...]
    h = _rms(x_ref[...]) * g_ref[2:3, :]
    return (h * (1.0 + m[4:5]) + m[3:4]).astype(BF16)


PROJ_COLS = 512


def _proj_kernel(x_ref, m_ref, g_ref, w_ref, o_ref):
    h = _mixer_in(x_ref, m_ref, g_ref)
    for c0 in range(0, o_ref.shape[1], PROJ_COLS):
        o_ref[:, c0:c0 + PROJ_COLS] = _dot(h, w_ref[:, c0:c0 + PROJ_COLS].astype(BF16)).astype(o_ref.dtype)


def _proj(x, mods, norm_g, w, li, j, out_dtype):
    n = w.shape[-1]
    return pl.pallas_call(
        _proj_kernel,
        out_shape=jax.ShapeDtypeStruct((N_TOK, n), out_dtype),
        grid=(N_TOK // TM,),
        in_specs=[_tok_spec(TM, D_MODEL), _mods_spec(TM), _norm_spec(li), _resident(w, j)],
        out_specs=_tok_spec(TM, n),
        compiler_params=_params(("arbitrary",)),
        name="mixer_proj",
    )(x, mods, norm_g, w)


def _rope_cols(t, cos, sin):
    lane = lax.broadcasted_iota(jnp.int32, (t.shape[0], LANES), 1)
    low = (lane % HEAD_DIM) < (HEAD_DIM // 2)
    outs = []
    for c in range(t.shape[1] // LANES):
        tc = t[:, c * LANES:(c + 1) * LANES]
        partner = jnp.where(low, pltpu.roll(tc, LANES - HEAD_DIM // 2, 1),
                            pltpu.roll(tc, HEAD_DIM // 2, 1))
        outs.append(tc * cos + partner * sin)
    return jnp.concatenate(outs, axis=1)


def _qkv_kernel(x_ref, m_ref, g_ref, w_ref, cos_ref, sin_ref,
                q_ref, k_ref, v_ref, kf_ref, vf_ref):
    h = _mixer_in(x_ref, m_ref, g_ref)
    cos = cos_ref[...]
    sin = sin_ref[...]
    for c0 in range(0, ATTN_DIM, KV_DIM):
        q = _dot(h, w_ref[:, c0:c0 + KV_DIM].astype(BF16))
        q_ref[:, c0:c0 + KV_DIM] = (_rope_cols(q, cos, sin) * (ATTN_SCALE * LOG2E)).astype(BF16)
    k = _dot(h, w_ref[:, ATTN_DIM:ATTN_DIM + KV_DIM].astype(BF16))
    v = _dot(h, w_ref[:, ATTN_DIM + KV_DIM:].astype(BF16))

    @pl.when(pl.program_id(0) >= N_SAMPLE_TOK // TM)
    def _():
        for s in range(TM // SEQ):
            kf_ref[s] = k[s * SEQ:(s + 1) * SEQ, :].T
            vf_ref[s] = v[s * SEQ:(s + 1) * SEQ, :].T

    k_ref[...] = _rope_cols(k, cos, sin).astype(BF16)
    v_ref[...] = v.astype(BF16)


def _rope_tables():
    pos = np.arange(DEC_SEQ)
    row = (pos // GRID_W).astype(np.float32)
    col = (pos % GRID_W).astype(np.float32)
    inv = (ROPE_BASE ** (-np.arange(ROPE_PAIRS_PER_AXIS, dtype=np.float32) / ROPE_PAIRS_PER_AXIS)).astype(np.float32)
    ang = jnp.asarray(np.concatenate([row[:, None] * inv, col[:, None] * inv], axis=-1).astype(np.float32))
    cos, sin = jnp.cos(ang), jnp.sin(ang)
    cos = jnp.concatenate([cos, cos, cos, cos], axis=-1)
    sin = jnp.concatenate([-sin, sin, -sin, sin], axis=-1)
    cos = jnp.concatenate([cos, jnp.ones((TM, LANES), F32)], axis=0)
    sin = jnp.concatenate([sin, jnp.zeros((TM, LANES), F32)], axis=0)
    return cos, sin


def _qkv(x, mods, norm_g, w, li, j, cos, sin):
    per = DEC_SEQ // TM
    ns = N_SAMPLE_TOK // TM
    tab = pl.BlockSpec((TM, LANES), lambda i: (jnp.where(i < ns, i % per, per), 0))
    cache = pl.BlockSpec((TM // SEQ, KV_DIM, SEQ), lambda i: (jnp.maximum(i - ns, 0), 0, 0))
    return pl.pallas_call(
        _qkv_kernel,
        out_shape=(
            jax.ShapeDtypeStruct((N_TOK, ATTN_DIM), BF16),
            jax.ShapeDtypeStruct((N_TOK, KV_DIM), BF16),
            jax.ShapeDtypeStruct((N_TOK, KV_DIM), BF16),
            jax.ShapeDtypeStruct((BATCH, KV_DIM, SEQ), F32),
            jax.ShapeDtypeStruct((BATCH, KV_DIM, SEQ), F32),
        ),
        grid=(N_TOK // TM,),
        in_specs=[_tok_spec(TM, D_MODEL), _mods_spec(TM), _norm_spec(li), _resident(w, j), tab, tab],
        out_specs=(_tok_spec(TM, ATTN_DIM), _tok_spec(TM, KV_DIM), _tok_spec(TM, KV_DIM), cache, cache),
        compiler_params=_params(("arbitrary",)),
        name="attn_qkv",
    )(x, mods, norm_g, w, cos, sin)


N_HEAD_SLOTS = 2 * ATTN_KV_HEADS


def _fill_head_slots(x, k_dst, v_dst, row0, is_value):
    dst = v_dst if is_value else k_dst
    pad = 1.0 if is_value else 0.0
    rows = x.shape[0]
    low = lax.broadcasted_iota(jnp.int32, (rows, LANES), 1) < HEAD_DIM
    for g in range(ATTN_KV_HEADS):
        cg, pg = g // 2, g % 2
        xb = x[:, cg * LANES:(cg + 1) * LANES]
        xr = pltpu.roll(xb, HEAD_DIM, 1)
        x_lo, x_hi = (xb, xr) if pg == 0 else (xr, xb)
        dst[2 * g, pl.ds(row0, rows), :] = jnp.where(low, x_lo, pad).astype(BF16)
        dst[2 * g + 1, pl.ds(row0, rows), :] = jnp.where(low, pad, x_hi).astype(BF16)


def _attend_rows(q, key_blocks, sink_ref, o_ref, orow0, s_scr, p_scr):
    rows = q.shape[0]
    rid = lax.broadcasted_iota(jnp.int32, (2 * rows, 1), 0)
    low = lax.broadcasted_iota(jnp.int32, (2 * rows, LANES), 1) < HEAD_DIM
    sinks, maxes = [], []
    for slot in range(N_HEAD_SLOTS):
        g, side = slot // 2, slot % 2
        c0 = 2 * g * LANES
        qs = jnp.concatenate([q[:, c0:c0 + LANES], q[:, c0 + LANES:c0 + 2 * LANES]], axis=0)
        sk = jnp.where(rid < rows, sink_ref[4 * g + side], sink_ref[4 * g + 2 + side]) * LOG2E
        mx = sk
        col = 0
        for k_of, _, bias, n_keys in key_blocks:
            s = _dot_nt(qs, k_of(slot))
            if bias is not None:
                s = s + bias
            s_scr[slot, :, col:col + n_keys] = s
            mx = jnp.maximum(mx, jnp.max(s, axis=-1, keepdims=True))
            col += n_keys
        sinks.append(sk)
        maxes.append(mx)
    for slot in range(N_HEAD_SLOTS):
        p_scr[slot] = jnp.exp2(s_scr[slot] - maxes[slot]).astype(BF16)
    for g in range(ATTN_KV_HEADS):
        res = []
        for side in range(2):
            slot = 2 * g + side
            part = None
            col = 0
            for _, v_of, _, n_keys in key_blocks:
                pv = _dot(p_scr[slot, :, col:col + n_keys], v_of(slot))
                part = pv if part is None else part + pv
                col += n_keys
            den = pltpu.roll(part, HEAD_DIM, 1) + jnp.exp2(sinks[slot] - maxes[slot])
            res.append(part / den)
        out = jnp.where(low, res[0], res[1]).astype(o_ref.dtype)
        c0 = 2 * g * LANES
        o_ref[pl.ds(orow0, rows), c0:c0 + LANES] = out[:rows]
        o_ref[pl.ds(orow0, rows), c0 + LANES:c0 + 2 * LANES] = out[rows:]


def _attn_lat_kernel(sink_ref, q_ref, k_ref, v_ref, ck_ref, cv_ref, o_ref,
                     kl_ref, vl_ref, kc_ref, vc_ref, s_scr, p_scr):
    nb = DEC_SEQ // BLOCK
    zeros = jnp.zeros((BLOCK, LANES), BF16)
    for slot in range(N_HEAD_SLOTS):
        for ref in (kl_ref, vl_ref):
            ref[slot, pl.ds(0, BLOCK), :] = zeros
            ref[slot, pl.ds(DEC_SEQ + BLOCK, BLOCK), :] = zeros
    _fill_head_slots(k_ref[...].astype(F32), kl_ref, vl_ref, BLOCK, False)
    _fill_head_slots(v_ref[...].astype(F32), kl_ref, vl_ref, BLOCK, True)
    _fill_head_slots(ck_ref[...].T, kc_ref, vc_ref, 0, False)
    _fill_head_slots(cv_ref[...].T, kc_ref, vc_ref, 0, True)
    n_band = 3 * BLOCK
    qi = lax.broadcasted_iota(jnp.int32, (2 * BLOCK, n_band), 0) % BLOCK
    u = lax.broadcasted_iota(jnp.int32, (2 * BLOCK, n_band), 1)
    in_window = jnp.abs(qi + BLOCK - u) <= WINDOW

    def body(n, carry):
        r0 = pl.multiple_of(n * BLOCK, BLOCK)
        kpos = (n - 1) * BLOCK + u
        ok = in_window & (kpos >= 0) & (kpos < nb * BLOCK)
        bias = jnp.where(ok, 0.0, NEG_BIG)
        band = (lambda s: kl_ref[s, pl.ds(r0, n_band), :], lambda s: vl_ref[s, pl.ds(r0, n_band), :],
                bias, n_band)
        ctx = (lambda s: kc_ref[s], lambda s: vc_ref[s], None, PAST_LEN)
        _attend_rows(q_ref[pl.ds(r0, BLOCK), :], [band, ctx], sink_ref, o_ref, r0, s_scr, p_scr)
        return carry

    lax.fori_loop(0, nb, body, 0)


def _attn_latent(sink, q, k, v, ck, cv, j):
    seq = lambda n: pl.BlockSpec((DEC_SEQ, n), lambda b: (b, 0))
    ctx = pl.BlockSpec((None, None, KV_DIM, PAST_LEN), lambda b: (b, j, 0, 0))
    return pl.pallas_call(
        _attn_lat_kernel,
        out_shape=jax.ShapeDtypeStruct((N_SAMPLE_TOK, ATTN_DIM), BF16),
        grid=(DEC_BATCH,),
        in_specs=[pl.BlockSpec(memory_space=pltpu.SMEM), seq(ATTN_DIM), seq(KV_DIM), seq(KV_DIM), ctx, ctx],
        out_specs=seq(ATTN_DIM),
        scratch_shapes=[
            pltpu.VMEM((N_HEAD_SLOTS, DEC_SEQ + 2 * BLOCK, LANES), BF16),
            pltpu.VMEM((N_HEAD_SLOTS, DEC_SEQ + 2 * BLOCK, LANES), BF16),
            pltpu.VMEM((N_HEAD_SLOTS, PAST_LEN, LANES), BF16),
            pltpu.VMEM((N_HEAD_SLOTS, PAST_LEN, LANES), BF16),
            pltpu.VMEM((N_HEAD_SLOTS, 2 * BLOCK, 3 * BLOCK + PAST_LEN), F32),
            pltpu.VMEM((N_HEAD_SLOTS, 2 * BLOCK, 3 * BLOCK + PAST_LEN), BF16),
        ],
        compiler_params=_params(("arbitrary",)),
        name="attn_latent",
    )(sink, q, k, v, ck, cv)


def _attn_ctx_kernel(sink_ref, q_ref, k_ref, v_ref, o_ref, ks_ref, vs_ref, s_scr, p_scr):
    _fill_head_slots(k_ref[...].astype(F32), ks_ref, vs_ref, 0, False)
    _fill_head_slots(v_ref[...].astype(F32), ks_ref, vs_ref, 0, True)
    blk = (lambda s: ks_ref[s], lambda s: vs_ref[s], None, SEQ)
    _attend_rows(q_ref[...], [blk], sink_ref, o_ref, 0, s_scr, p_scr)


def _attn_context(sink, q, k, v):
    off = N_SAMPLE_TOK // SEQ
    row = lambda n: pl.BlockSpec((SEQ, n), lambda b: (off + b, 0))
    return pl.pallas_call(
        _attn_ctx_kernel,
        out_shape=jax.ShapeDtypeStruct((N_PROMPT_TOK, ATTN_DIM), BF16),
        grid=(BATCH,),
        in_specs=[pl.BlockSpec(memory_space=pltpu.SMEM), row(ATTN_DIM), row(KV_DIM), row(KV_DIM)],
        out_specs=pl.BlockSpec((SEQ, ATTN_DIM), lambda b: (b, 0)),
        scratch_shapes=[pltpu.VMEM((N_HEAD_SLOTS, SEQ, LANES), BF16),
                        pltpu.VMEM((N_HEAD_SLOTS, SEQ, LANES), BF16),
                        pltpu.VMEM((N_HEAD_SLOTS, 2 * SEQ, SEQ), F32),
                        pltpu.VMEM((N_HEAD_SLOTS, 2 * SEQ, SEQ), BF16)],
        compiler_params=_params(("arbitrary",)),
        name="attn_context",
    )(sink, q, k, v)


CONV_COLS = 256


def _conv_kernel(x_ref, m_ref, g_ref, win_ref, cw_ref, cb_ref, wout_ref, o_ref, a_ref):
    i = pl.program_id(0)
    h = _mixer_in(x_ref, m_ref, g_ref)
    tm = h.shape[0]
    seq = jnp.where(i < N_SAMPLE_TOK // tm, DEC_SEQ, SEQ)
    pos = lax.broadcasted_iota(jnp.int32, (tm, CONV_COLS), 0) & (seq - 1)
    first = pos == 0
    last = pos == seq - 1
    cw = cw_ref[...]
    cb = cb_ref[...]
    for c in range(D_MODEL // CONV_COLS):
        cols = slice(c * CONV_COLS, (c + 1) * CONV_COLS)
        part = lambda j: _dot(
            h, win_ref[:, j * D_MODEL + c * CONV_COLS:j * D_MODEL + (c + 1) * CONV_COLS].astype(BF16))
        z = part(1) * part(2)
        zl = jnp.where(first, 0.0, pltpu.roll(z, 1, 0))
        zr = jnp.where(last, 0.0, pltpu.roll(z, tm - 1, 0))
        conv = cb[:, cols] + zl * cw[0:1, cols] + z * cw[1:2, cols] + zr * cw[2:3, cols]
        a_ref[:, cols] = (part(0) * conv).astype(BF16)
    y = _rms(_dot(a_ref[...], wout_ref[...].astype(BF16))) * g_ref[3:4, :]
    o_ref[...] = x_ref[...] + m_ref[5:6, :] * y


def _conv_layer(x, mods, norm_g, w_in, cw, cb, w_out, li, j):
    tm = CONV_TM
    return pl.pallas_call(
        _conv_kernel,
        out_shape=jax.ShapeDtypeStruct((N_TOK, D_MODEL), F32),
        grid=(N_TOK // tm,),
        in_specs=[
            _tok_spec(tm, D_MODEL), _mods_spec(tm), _norm_spec(li), _resident(w_in, j),
            pl.BlockSpec((None, 3, D_MODEL), lambda i: (j, 0, 0)),
            pl.BlockSpec((None, 1, D_MODEL), lambda i: (j, 0, 0)),
            _resident(w_out, j),
        ],
        out_specs=_tok_spec(tm, D_MODEL),
        scratch_shapes=[pltpu.VMEM((tm, D_MODEL), BF16)],
        compiler_params=_params(("arbitrary",)),
        name="conv_layer",
    )(x, mods, norm_g, w_in, cw, cb.reshape(-1, 1, D_MODEL), w_out)


def _block_row(x, w, off):
    c = x.shape[0]
    blk = 2 * w
    row = lambda r: jnp.broadcast_to(x[r:r + 1, :], (max(blk, SUBLANES), LANES))
    if blk >= SUBLANES:
        return jnp.concatenate([row(j * blk + off) for j in range(c // blk)], axis=0)
    if blk == 4:
        upper = lax.broadcasted_iota(jnp.int32, (SUBLANES, LANES), 0) < 4
        return jnp.concatenate(
            [jnp.where(upper, row(SUBLANES * v + off), row(SUBLANES * v + 4 + off))
             for v in range(c // SUBLANES)], axis=0)
    odd = (lax.broadcasted_iota(jnp.int32, (c, LANES), 0) & 1) != 0
    if off == 0:
        return jnp.where(odd, pltpu.roll(x, 1, 0), x)
    return jnp.where(odd, x, pltpu.roll(x, c - 1, 0))


def _split3(x):
    hi = x.astype(BF16)
    r = x - hi.astype(F32)
    mid = r.astype(BF16)
    lo = (r - mid.astype(F32)).astype(BF16)
    return jnp.concatenate([hi, mid, lo], axis=0)


def _hgrn_kernel(*refs, seq, li, hp, has_init, emit_state):
    lb_ref, lv_ref, zq_ref, zi_ref, zf_ref, zb_ref, zg_ref, ng_ref = refs[:8]
    refs = refs[8:]
    s0_ref = None
    if has_init:
        s0_ref, refs = refs[0], refs[1:]
    o_ref, refs = refs[0], refs[1:]
    sout_ref = None
    if emit_state:
        sout_ref, refs = refs[0], refs[1:]
    acc_ref, qf_ref, qb_ref, kv_ref, dec_ref, stf_ref, stb_ref = refs
    c = GLA_CHUNK
    nc = seq // c

    def lower_bound(x):
        e = jnp.exp(x - jnp.max(x, axis=0, keepdims=True))
        pr = e / jnp.sum(e, axis=0, keepdims=True)
        return jnp.sum(pr[1:li + 1], axis=0, keepdims=True)

    lbf, lbb = lower_bound(lb_ref[0]), lower_bound(lb_ref[1])
    ri = lax.broadcasted_iota(jnp.int32, (c, c), 0)
    ci = lax.broadcasted_iota(jnp.int32, (c, c), 1)
    tril = jnp.where(ci <= ri, 1.0, 0.0).astype(BF16)
    triu = jnp.where(ci >= ri, 1.0, 0.0).astype(BF16)
    tril3 = jnp.concatenate([tril, tril, tril], axis=1)
    triu3 = jnp.concatenate([triu, triu, triu], axis=1)
    lv = lv_ref[...]
    rowi = lax.broadcasted_iota(jnp.int32, (c, LANES), 0)

    def chunk_local(n, hd):
        rows = pl.ds(pl.multiple_of(n * c, c), c)
        cols = slice(hd * LANES, (hd + 1) * LANES)
        q = _silu(zq_ref[rows, cols]) * (HGRN_DK ** -0.5)
        zi = zi_ref[rows, cols]
        v = zi.astype(BF16)
        ff = lbf[:, cols] + (1.0 - lbf[:, cols]) * jax.nn.sigmoid(zf_ref[rows, cols])
        fb = lbb[:, cols] + (1.0 - lbb[:, cols]) * jax.nn.sigmoid(zb_ref[rows, cols])
        kf = 1.0 - ff
        kb = 1.0 - fb
        pre = _dot(tril3, _split3(jnp.log2(ff)))
        suf = _dot(triu3, _split3(jnp.log2(fb)))
        attn = jnp.where(lv == 0, _dot_nt(q.astype(BF16), (kf + kb).astype(BF16)), 0.0)
        odd = (rowi & 1) != 0
        qh = (q * jnp.where(odd, ff, fb)).astype(BF16)
        kh = jnp.where(odd, kb, kf).astype(BF16)
        attn = jnp.where(lv == 1, _dot_nt(qh, kh), attn)
        w, level = 2, 2
        while w < c:
            d_f = pre - _block_row(pre, w, w - 1)
            d_b = suf - _block_row(suf, w, w)
            xq = jnp.minimum(d_f, d_b)
            xk = xq - (d_f + d_b)
            second = (rowi & w) != 0
            qh = (q * jnp.exp2(xq)).astype(BF16)
            kh = (jnp.where(second, kb, kf) * jnp.exp2(xk)).astype(BF16)
            attn = jnp.where(lv == level, _dot_nt(qh, kh), attn)
            w *= 2
            level += 1
        acc_ref[hd, rows, :] = _dot(attn.astype(BF16), v)
        tot_f = pre[c - 1:c, :]
        tot_b = suf[0:1, :]
        qf_ref[hd, rows, :] = (q * jnp.exp2(pre)).astype(BF16)
        qb_ref[hd, rows, :] = (q * jnp.exp2(suf)).astype(BF16)
        kt = jnp.concatenate([(kf * jnp.exp2(tot_f - pre)).astype(BF16),
                              (kb * jnp.exp2(tot_b - suf)).astype(BF16)], axis=1)
        kv_ref[hd, n] = _dot(zi.T.astype(BF16), kt)
        dec_ref[hd, n] = jnp.exp2(jnp.concatenate([tot_f, tot_b], axis=1))

    def local_body(n, carry):
        for hd in range(hp):
            chunk_local(n, hd)
        return carry

    lax.fori_loop(0, nc, local_body, 0, unroll=min(nc, GLA_ROWS_IN_FLIGHT // (c * hp)))

    for hd in range(hp):
        if has_init:
            stf_ref[hd] = s0_ref[0, hd].T
            stb_ref[hd] = s0_ref[1, hd].T
        else:
            stf_ref[hd] = jnp.zeros((HGRN_DV, HGRN_DK), F32)
            stb_ref[hd] = jnp.zeros((HGRN_DV, HGRN_DK), F32)

    def carry_state(i, carry):
        n, m = i, nc - 1 - i
        rn = pl.ds(pl.multiple_of(n * c, c), c)
        rm = pl.ds(pl.multiple_of(m * c, c), c)
        for hd in range(hp):
            stf = stf_ref[hd]
            stb = stb_ref[hd]
            acc_ref[hd, rn, :] += _dot_nt(qf_ref[hd, rn, :], stf.astype(BF16))
            acc_ref[hd, rm, :] += _dot_nt(qb_ref[hd, rm, :], stb.astype(BF16))
            stf_ref[hd] = stf * dec_ref[hd, n][:, :HGRN_DK] + kv_ref[hd, n][:, :HGRN_DK]
            stb_ref[hd] = stb * dec_ref[hd, m][:, HGRN_DK:] + kv_ref[hd, m][:, HGRN_DK:]
        return carry

    lax.fori_loop(0, nc, carry_state, 0, unroll=True)
    for hd in range(hp):
        cols = slice(hd * LANES, (hd + 1) * LANES)
        o_ref[:, cols] = (_rms(acc_ref[hd]) * ng_ref[...] * _silu(zg_ref[:, cols])).astype(o_ref.dtype)
        if emit_state:
            sout_ref[0, hd] = stf_ref[hd].T
            sout_ref[1, hd] = stb_ref[hd].T


def _pair_levels():
    t = np.arange(GLA_CHUNK)
    x = t[:, None] ^ t[None, :]
    return jnp.asarray(np.where(x > 0, np.floor(np.log2(np.maximum(x, 1))) + 1, 0).astype(np.int32))


def _hgrn_core(z, lb, ng, s0, *, j, batch, seq, row_off, li, emit_state):
    has_init = s0 is not None
    blk0 = row_off // seq
    fd = HGRN_FD // LANES
    dm = D_MODEL // LANES
    c = GLA_CHUNK
    hp = max(1, GLA_ROWS_IN_FLIGHT // seq)
    width = hp * LANES

    def col(c0):
        return pl.BlockSpec((seq, width), lambda b, h: (blk0 + b, c0 // hp + h))

    in_specs = [
        pl.BlockSpec((2, DEPTH, width), lambda b, h: (0, 0, h)),
        pl.BlockSpec((c, c), lambda b, h: (0, 0)),
        col(0), col(fd), col(fd + dm), col(2 * fd + dm), col(3 * fd + dm),
        pl.BlockSpec((None, 1, HGRN_DV), lambda b, h: (j, 0, 0)),
    ]
    args = [lb, _pair_levels(), z, z, z, z, z, ng.reshape(-1, 1, HGRN_DV)]
    if has_init:
        in_specs.append(pl.BlockSpec((None, None, 2, hp, HGRN_DK, HGRN_DV),
                                     lambda b, h: (b, j, 0, h, 0, 0)))
        args.append(s0)
    out_shape = [jax.ShapeDtypeStruct((batch * seq, D_MODEL), BF16)]
    out_specs = [pl.BlockSpec((seq, width), lambda b, h: (b, h))]
    if emit_state:
        out_shape.append(jax.ShapeDtypeStruct((batch, 2, HGRN_HEADS, HGRN_DK, HGRN_DV), F32))
        out_specs.append(pl.BlockSpec((None, 2, hp, HGRN_DK, HGRN_DV), lambda b, h: (b, 0, h, 0, 0)))
    return pl.pallas_call(
        functools.partial(_hgrn_kernel, seq=seq, li=li, hp=hp, has_init=has_init, emit_state=emit_state),
        out_shape=tuple(out_shape),
        grid=(batch, HGRN_HEADS // hp),
        in_specs=in_specs,
        out_specs=tuple(out_specs),
        scratch_shapes=[
            pltpu.VMEM((hp, seq, HGRN_DV), F32),
            pltpu.VMEM((hp, seq, HGRN_DK), BF16),
            pltpu.VMEM((hp, seq, HGRN_DK), BF16),
            pltpu.VMEM((hp, seq // c, HGRN_DV, 2 * HGRN_DK), F32),
            pltpu.VMEM((hp, seq // c, 1, 2 * HGRN_DK), F32),
            pltpu.VMEM((hp, HGRN_DV, HGRN_DK), F32),
            pltpu.VMEM((hp, HGRN_DV, HGRN_DK), F32),
        ],
        compiler_params=_params(("arbitrary", "arbitrary")),
        name=f"hgrn_core_{seq}",
    )(*args)


def kernel(x_prompt, x_sample, cache_k, cache_v, state_hgrn, c, c_ctx, w_ada, b_ada, norm_g, w_ffn_in, w_ffn_out, w_attn_qkv, w_attn_o, attn_sink, w_conv_in, conv_w, conv_b, w_conv_out, w_hgrn_in, hgrn_lb, hgrn_norm_g, w_hgrn_out):
    cond = jnp.concatenate([c, c_ctx[None, :],
                            jnp.zeros((N_COND - DEC_BATCH - 1, D_MODEL), F32)], axis=0)
    b_ada = b_ada.reshape(DEPTH, 1, ADA_COLS)
    mods = _ada_table(cond, w_ada, b_ada, 0)
    cos, sin = _rope_tables()
    ck = cache_k.transpose(0, 1, 3, 4, 2).reshape(DEC_BATCH, -1, KV_DIM, PAST_LEN)
    cv = cache_v.transpose(0, 1, 3, 4, 2).reshape(DEC_BATCH, -1, KV_DIM, PAST_LEN)
    cache_out = lambda parts: jnp.stack(parts, axis=1).reshape(
        BATCH, -1, ATTN_KV_HEADS, HEAD_DIM, SEQ).transpose(0, 1, 4, 2, 3)
    bf = lambda w: w.astype(BF16)
    xs = (x_sample.reshape(N_SAMPLE_TOK, D_MODEL), x_prompt.reshape(N_PROMPT_TOK, D_MODEL))
    ffn_w = (bf(w_ffn_in[0, 0]), bf(w_ffn_out[0, 0]))
    new_k, new_v, new_s = [], [], []
    for li in range(DEPTH):
        kind, j = li % N_MIXERS, li // N_MIXERS
        last = li == DEPTH - 1
        mixer = None
        ada = None if last else (cond, w_ada, b_ada, li + 1)
        (x,), ffn_w, mods_next = _ffn(xs, mods, norm_g, ffn_w, (w_ffn_in, w_ffn_out, li, 1), li, 0, ada=ada)
        if kind == 0:
            q, k, v, kf, vf = _qkv(x, mods, norm_g, w_attn_qkv, li, j, cos, sin)
            o_s = _attn_latent(attn_sink[j], q, k, v, ck, cv, j)
            o_p = _attn_context(attn_sink[j], q, k, v)
            mixer = (o_s, o_p, w_attn_o, j)
            new_k.append(kf)
            new_v.append(vf)
        elif kind == 1:
            x = _conv_layer(x, mods, norm_g, w_conv_in, conv_w, conv_b, w_conv_out, li, j)
        else:
            z = _proj(x, mods, norm_g, w_hgrn_in, li, j, F32)
            o_s = _hgrn_core(z, hgrn_lb, hgrn_norm_g, state_hgrn, j=j, batch=DEC_BATCH,
                             seq=DEC_SEQ, row_off=0, li=li, emit_state=False)[0]
            o_p, st = _hgrn_core(z, hgrn_lb, hgrn_norm_g, None, j=j, batch=BATCH, seq=SEQ,
                                 row_off=N_SAMPLE_TOK, li=li, emit_state=True)
            mixer = (o_s, o_p, w_hgrn_out, j)
            new_s.append(st)
        nxt = None if last else (w_ffn_in, w_ffn_out, li + 1, 0)
        xs, ffn_w, _ = _ffn((x,), mods, norm_g, ffn_w, nxt, li, 1, split_out=last, mixer=mixer)
        mods = mods_next
    y_sample = xs[0].reshape(DEC_BATCH, DEC_SEQ, D_MODEL)
    y_prompt = xs[1].reshape(BATCH, SEQ, D_MODEL)
    return (y_prompt, y_sample, cache_out(new_k), cache_out(new_v), jnp.stack(new_s, axis=1))
```

```python
import functools
import math

import jax
import jax.numpy as jnp
import numpy as np
from jax import lax
from jax.experimental import pallas as pl
from jax.experimental.pallas import tpu as pltpu

D_MODEL = 1024
BATCH = 16
SEQ = 256
DEPTH = 4
DEC_BATCH = 8
DEC_SEQ = 1024
PAST_LEN = 256
GRID_W = 64
N_MIXERS = 3
N_MOD = 9
N_NORMS = 6
D_FF = 2816
FFN_RES = 0.5
ATTN_HEADS = 16
ATTN_KV_HEADS = 4
HEAD_DIM = 64
ATTN_DIM = ATTN_HEADS * HEAD_DIM
KV_DIM = ATTN_KV_HEADS * HEAD_DIM
BLOCK = 128
WINDOW = 128
ATTN_SCALE = HEAD_DIM ** -0.5
LOG2E = math.log2(math.e)
ROPE_BASE = 10000.0
ROPE_PAIRS_PER_AXIS = HEAD_DIM // 4
HGRN_HEADS = 8
HGRN_DK = 128
HGRN_DV = D_MODEL // HGRN_HEADS
HGRN_FD = HGRN_HEADS * HGRN_DK
EPS = 1e-6

N_ATTN_LAYERS = len(range(0, DEPTH, N_MIXERS))
N_SAMPLE_TOK = DEC_BATCH * DEC_SEQ
N_PROMPT_TOK = BATCH * SEQ
N_TOK = N_SAMPLE_TOK + N_PROMPT_TOK
N_COND = 16
CTX_COND = DEC_BATCH

LANES = 128
SUBLANES = 8
TM = 512
CONV_TM = 1024
FFN_SPLITS = (0, 1024, 2048, D_FF)
GLA_CHUNK = 128
GLA_ROWS_IN_FLIGHT = 2048
NEG_BIG = -1e30
VMEM_LIMIT = 56 * 1024 * 1024

F32 = jnp.float32
BF16 = jnp.bfloat16


def _rms(x):
    return x * lax.rsqrt(jnp.mean(x * x, axis=-1, keepdims=True) + EPS)


def _silu(x):
    return x * jax.nn.sigmoid(x)


def _dot(a, b):
    return jnp.dot(a, b, preferred_element_type=F32)


def _dot_nt(a, b):
    return lax.dot_general(a, b, (((1,), (1,)), ((), ())), preferred_element_type=F32)


def _params(sem):
    return pltpu.CompilerParams(dimension_semantics=sem, vmem_limit_bytes=VMEM_LIMIT)


def _resident(arr, *lead):
    shape = (None,) * len(lead) + arr.shape[len(lead):]
    idx = tuple(lead) + (0,) * (arr.ndim - len(lead))
    return pl.BlockSpec(shape, lambda *_: idx, pipeline_mode=pl.Buffered(1))


def _mods_spec(tm):
    per = DEC_SEQ // tm
    return pl.BlockSpec((None, N_MOD, D_MODEL), lambda i: (jnp.minimum(i // per, CTX_COND), 0, 0))


def _norm_spec(li):
    return pl.BlockSpec((None, N_NORMS, D_MODEL), lambda i: (li, 0, 0))


def _tok_spec(tm, n):
    return pl.BlockSpec((tm, n), lambda i: (i, 0))


def _split_specs(tm, n):
    ns = N_SAMPLE_TOK // tm
    return [pl.BlockSpec((tm, n), lambda i: (jnp.minimum(i, ns - 1), 0)),
            pl.BlockSpec((tm, n), lambda i: (jnp.maximum(i - ns, 0), 0))]


def _load_split(s_ref, p_ref, tm):
    return jnp.where(pl.program_id(0) < N_SAMPLE_TOK // tm, s_ref[...], p_ref[...])


def _store_split(s_ref, p_ref, val, tm):
    ns = N_SAMPLE_TOK // tm
    i = pl.program_id(0)

    @pl.when(i < ns)
    def _():
        s_ref[...] = val

    @pl.when(i >= ns)
    def _():
        p_ref[...] = val


def _ada_kernel(c_ref, w_ref, b_ref, o_ref):
    s = _silu(c_ref[...]).astype(BF16)
    o_ref[...] = _dot(s, w_ref[...].astype(BF16)) + b_ref[...]


ADA_COLS = N_MOD * D_MODEL
ADA_SIDE_COLS = ADA_COLS // (N_TOK // TM)


def _ada_table(cond, w_ada, b_ada, li):
    tn = 3 * D_MODEL
    out = pl.pallas_call(
        _ada_kernel,
        out_shape=jax.ShapeDtypeStruct((N_COND, ADA_COLS), F32),
        grid=(ADA_COLS // tn,),
        in_specs=[
            pl.BlockSpec((N_COND, D_MODEL), lambda j: (0, 0)),
            pl.BlockSpec((None, D_MODEL, tn), lambda j: (li, 0, j)),
            pl.BlockSpec((None, 1, tn), lambda j: (li, 0, j)),
        ],
        out_specs=pl.BlockSpec((N_COND, tn), lambda j: (0, j)),
        compiler_params=_params(("arbitrary",)),
        name="ada_table",
    )(cond, w_ada, b_ada)
    return out.reshape(N_COND, N_MOD, D_MODEL)


NEXT_IN_ROWS = 64
NEXT_OUT_ROWS = 128


def _ffn_kernel(*refs, k0, g0, split_in, split_out, has_mixer, has_next, has_ada):
    n_in = 2 if split_in else 1
    x_refs, refs = refs[:n_in], refs[n_in:]
    mix_refs = ()
    if has_mixer:
        mix_refs, refs = refs[:3], refs[3:]
    m_ref, g_ref, win_ref, wout_ref = refs[:4]
    refs = refs[4:]
    n_side = (2 if has_next else 0) + (3 if has_ada else 0)
    side_in, outs = refs[:n_side], refs[n_side:]
    n_x = 2 if split_out else 1
    o_refs, side_out = outs[:n_x], outs[n_x:]
    if has_next:
        side_out[0][...] = side_in[0][...].astype(BF16)
        side_out[1][...] = side_in[1][...].astype(BF16)
        side_in, side_out = side_in[2:], side_out[2:]
    if has_ada:
        c_ref, wa_ref, ba_ref = side_in
        side_out[0][...] = _dot(_silu(c_ref[...]).astype(BF16), wa_ref[...].astype(BF16)) + ba_ref[...]
    x = _load_split(*x_refs, TM) if split_in else x_refs[0][...]
    m = m_ref[...]
    g = g_ref[...]
    if has_mixer:
        as_ref, ap_ref, wo_ref = mix_refs
        mixed = _dot(_load_split(as_ref, ap_ref, TM), wo_ref[...].astype(BF16))
        x = x + m[5:6] * (_rms(mixed) * g[3:4])
    h = _rms(x) * g[g0:g0 + 1]
    h = (h * (1.0 + m[k0 + 1:k0 + 2]) + m[k0:k0 + 1]).astype(BF16)
    acc = None
    for lo, hi in zip(FFN_SPLITS[:-1], FFN_SPLITS[1:]):
        a = _dot(h, win_ref[:, lo:hi])
        b = _dot(h, win_ref[:, D_FF + lo:D_FF + hi])
        u = (_silu(a) * b).astype(BF16)
        part = _dot(u, wout_ref[lo:hi, :])
        acc = part if acc is None else acc + part
    y = _rms(acc) * g[g0 + 1:g0 + 2]
    res = x + FFN_RES * m[k0 + 2:k0 + 3] * y
    if split_out:
        _store_split(*o_refs, res, TM)
    else:
        o_refs[0][...] = res


def _ffn(xs, mods, norm_g, w_bf, nxt, li, which, split_out=False, mixer=None, ada=None):
    k0, g0 = (0, 0) if which == 0 else (6, 4)
    split_in = len(xs) == 2
    steps = N_TOK // TM
    x_specs = _split_specs(TM, D_MODEL) if split_in else [_tok_spec(TM, D_MODEL)]
    mix_specs, mix_args = [], []
    if mixer is not None:
        a_s, a_p, w_o, j = mixer
        mix_specs = _split_specs(TM, D_MODEL) + [_resident(w_o, j)]
        mix_args = [a_s, a_p, w_o]
    if split_out:
        out_shape = [jax.ShapeDtypeStruct((N_SAMPLE_TOK, D_MODEL), F32),
                     jax.ShapeDtypeStruct((N_PROMPT_TOK, D_MODEL), F32)]
        out_specs = _split_specs(TM, D_MODEL)
    else:
        out_shape = [jax.ShapeDtypeStruct((N_TOK, D_MODEL), F32)]
        out_specs = [_tok_spec(TM, D_MODEL)]
    nxt_specs, nxt_args = [], []
    if nxt is not None:
        w_in_all, w_out_all, li2, which2 = nxt
        last_in = D_MODEL // NEXT_IN_ROWS - 1
        last_out = D_FF // NEXT_OUT_ROWS - 1
        assert last_in < steps and last_out < steps
        nxt_specs = [
            pl.BlockSpec((None, None, NEXT_IN_ROWS, 2 * D_FF),
                         lambda i: (li2, which2, jnp.minimum(i, last_in), 0)),
            pl.BlockSpec((None, None, NEXT_OUT_ROWS, D_MODEL),
                         lambda i: (li2, which2, jnp.minimum(i, last_out), 0)),
        ]
        nxt_args = [w_in_all, w_out_all]
        out_shape += [jax.ShapeDtypeStruct((D_MODEL, 2 * D_FF), BF16),
                      jax.ShapeDtypeStruct((D_FF, D_MODEL), BF16)]
        out_specs += [pl.BlockSpec((NEXT_IN_ROWS, 2 * D_FF), lambda i: (jnp.minimum(i, last_in), 0)),
                      pl.BlockSpec((NEXT_OUT_ROWS, D_MODEL), lambda i: (jnp.minimum(i, last_out), 0))]
    ada_specs, ada_args = [], []
    if ada is not None:
        cond, w_ada, b_ada, li_ada = ada
        ada_specs = [
            pl.BlockSpec((N_COND, D_MODEL), lambda i: (0, 0)),
            pl.BlockSpec((None, D_MODEL, ADA_SIDE_COLS), lambda i: (li_ada, 0, i)),
            pl.BlockSpec((None, 1, ADA_SIDE_COLS), lambda i: (li_ada, 0, i)),
        ]
        ada_args = [cond, w_ada, b_ada]
        out_shape += [jax.ShapeDtypeStruct((N_COND, ADA_COLS), F32)]
        out_specs += [pl.BlockSpec((N_COND, ADA_SIDE_COLS), lambda i: (0, i))]
    outs = pl.pallas_call(
        functools.partial(_ffn_kernel, k0=k0, g0=g0, split_in=split_in, split_out=split_out,
                          has_mixer=mixer is not None, has_next=nxt is not None, has_ada=ada is not None),
        out_shape=tuple(out_shape),
        grid=(steps,),
        in_specs=x_specs + mix_specs + [
            _mods_spec(TM), _norm_spec(li), _resident(w_bf[0]), _resident(w_bf[1])] + nxt_specs + ada_specs,
        out_specs=tuple(out_specs),
        compiler_params=_params(("arbitrary",)),
        name=f"ffn{which}",
    )(*xs, *mix_args, mods, norm_g, *w_bf, *nxt_args, *ada_args)
    n_x = 2 if split_out else 1
    n_w = 2 if nxt is not None else 0
    mods_next = outs[n_x + n_w].reshape(N_COND, N_MOD, D_MODEL) if ada is not None else None
    return tuple(outs[:n_x]), tuple(outs[n_x:n_x + n_w]), mods_next


def _mixer_in(x_ref, m_ref, g_ref):
    m = m_ref[...]
    h = _rms(x_ref[...]) * g_ref[2:3, :]
    return (h * (1.0 + m[4:5]) + m[3:4]).astype(BF16)


PROJ_COLS = 512


def _proj_kernel(x_ref, m_ref, g_ref, w_ref, o_ref):
    h = _mixer_in(x_ref, m_ref, g_ref)
    for c0 in range(0, o_ref.shape[1], PROJ_COLS):
        o_ref[:, c0:c0 + PROJ_COLS] = _dot(h, w_ref[:, c0:c0 + PROJ_COLS].astype(BF16)).astype(o_ref.dtype)


def _proj(x, mods, norm_g, w, li, j, out_dtype):
    n = w.shape[-1]
    return pl.pallas_call(
        _proj_kernel,
        out_shape=jax.ShapeDtypeStruct((N_TOK, n), out_dtype),
        grid=(N_TOK // TM,),
        in_specs=[_tok_spec(TM, D_MODEL), _mods_spec(TM), _norm_spec(li), _resident(w, j)],
        out_specs=_tok_spec(TM, n),
        compiler_params=_params(("arbitrary",)),
        name="mixer_proj",
    )(x, mods, norm_g, w)


def _rope_cols(t, cos, sin):
    lane = lax.broadcasted_iota(jnp.int32, (t.shape[0], LANES), 1)
    low = (lane % HEAD_DIM) < (HEAD_DIM // 2)
    outs = []
    for c in range(t.shape[1] // LANES):
        tc = t[:, c * LANES:(c + 1) * LANES]
        partner = jnp.where(low, pltpu.roll(tc, LANES - HEAD_DIM // 2, 1),
                            pltpu.roll(tc, HEAD_DIM // 2, 1))
        outs.append(tc * cos + partner * sin)
    return jnp.concatenate(outs, axis=1)


def _qkv_kernel(*refs):
    x_ref, m_ref, g_ref, w_ref, cos_ref, sin_ref = refs[:6]
    q_ref, k_ref, v_ref, kf_ref, vf_ref = refs[-5:]
    h = _mixer_in(x_ref, m_ref, g_ref)
    cos = cos_ref[...]
    sin = sin_ref[...]
    for c0 in range(0, ATTN_DIM, KV_DIM):
        q = _dot(h, w_ref[:, c0:c0 + KV_DIM].astype(BF16))
        q_ref[:, c0:c0 + KV_DIM] = (_rope_cols(q, cos, sin) * (ATTN_SCALE * LOG2E)).astype(BF16)
    k = _dot(h, w_ref[:, ATTN_DIM:ATTN_DIM + KV_DIM].astype(BF16))
    v = _dot(h, w_ref[:, ATTN_DIM + KV_DIM:].astype(BF16))

    @pl.when(pl.program_id(0) >= N_SAMPLE_TOK // TM)
    def _():
        for s in range(TM // SEQ):
            kf_ref[s] = k[s * SEQ:(s + 1) * SEQ, :].T
            vf_ref[s] = v[s * SEQ:(s + 1) * SEQ, :].T

    k_ref[...] = _rope_cols(k, cos, sin).astype(BF16)
    v_ref[...] = v.astype(BF16)


def _rope_tables():
    pos = np.arange(DEC_SEQ)
    row = (pos // GRID_W).astype(np.float32)
    col = (pos % GRID_W).astype(np.float32)
    inv = (ROPE_BASE ** (-np.arange(ROPE_PAIRS_PER_AXIS, dtype=np.float32) / ROPE_PAIRS_PER_AXIS)).astype(np.float32)
    ang = jnp.asarray(np.concatenate([row[:, None] * inv, col[:, None] * inv], axis=-1).astype(np.float32))
    cos, sin = jnp.cos(ang), jnp.sin(ang)
    cos = jnp.concatenate([cos, cos, cos, cos], axis=-1)
    sin = jnp.concatenate([-sin, sin, -sin, sin], axis=-1)
    cos = jnp.concatenate([cos, jnp.ones((TM, LANES), F32)], axis=0)
    sin = jnp.concatenate([sin, jnp.zeros((TM, LANES), F32)], axis=0)
    return cos, sin


def _qkv(x, mods, norm_g, w, li, j, cos, sin, new_cache):
    per = DEC_SEQ // TM
    ns = N_SAMPLE_TOK // TM
    tab = pl.BlockSpec((TM, LANES), lambda i: (jnp.where(i < ns, i % per, per), 0))
    cache = pl.BlockSpec((TM // SEQ, None, KV_DIM, SEQ), lambda i: (jnp.maximum(i - ns, 0), j, 0, 0))
    n_in = 6
    return pl.pallas_call(
        _qkv_kernel,
        out_shape=(
            jax.ShapeDtypeStruct((N_TOK, ATTN_DIM), BF16),
            jax.ShapeDtypeStruct((N_TOK, KV_DIM), BF16),
            jax.ShapeDtypeStruct((N_TOK, KV_DIM), BF16),
            jax.ShapeDtypeStruct((BATCH, N_ATTN_LAYERS, KV_DIM, SEQ), F32),
            jax.ShapeDtypeStruct((BATCH, N_ATTN_LAYERS, KV_DIM, SEQ), F32),
        ),
        grid=(N_TOK // TM,),
        in_specs=[_tok_spec(TM, D_MODEL), _mods_spec(TM), _norm_spec(li), _resident(w, j), tab, tab]
                 + [pl.BlockSpec(memory_space=pl.ANY)] * len(new_cache),
        out_specs=(_tok_spec(TM, ATTN_DIM), _tok_spec(TM, KV_DIM), _tok_spec(TM, KV_DIM), cache, cache),
        input_output_aliases={n_in + t: 3 + t for t in range(len(new_cache))},
        compiler_params=_params(("arbitrary",)),
        name="attn_qkv",
    )(x, mods, norm_g, w, cos, sin, *new_cache)


N_HEAD_SLOTS = 2 * ATTN_KV_HEADS


def _fill_head_slots(x, k_dst, v_dst, row0, is_value):
    dst = v_dst if is_value else k_dst
    pad = 1.0 if is_value else 0.0
    rows = x.shape[0]
    low = lax.broadcasted_iota(jnp.int32, (rows, LANES), 1) < HEAD_DIM
    for g in range(ATTN_KV_HEADS):
        cg, pg = g // 2, g % 2
        xb = x[:, cg * LANES:(cg + 1) * LANES]
        xr = pltpu.roll(xb, HEAD_DIM, 1)
        x_lo, x_hi = (xb, xr) if pg == 0 else (xr, xb)
        dst[2 * g, pl.ds(row0, rows), :] = jnp.where(low, x_lo, pad).astype(BF16)
        dst[2 * g + 1, pl.ds(row0, rows), :] = jnp.where(low, pad, x_hi).astype(BF16)


def _attend_rows(q, key_blocks, sink_ref, o_ref, orow0, s_scr, p_scr):
    rows = q.shape[0]
    rid = lax.broadcasted_iota(jnp.int32, (2 * rows, 1), 0)
    low = lax.broadcasted_iota(jnp.int32, (2 * rows, LANES), 1) < HEAD_DIM
    sinks, maxes = [], []
    for slot in range(N_HEAD_SLOTS):
        g, side = slot // 2, slot % 2
        c0 = 2 * g * LANES
        qs = jnp.concatenate([q[:, c0:c0 + LANES], q[:, c0 + LANES:c0 + 2 * LANES]], axis=0)
        sk = jnp.where(rid < rows, sink_ref[4 * g + side], sink_ref[4 * g + 2 + side]) * LOG2E
        mx = sk
        col = 0
        for k_of, _, bias, n_keys in key_blocks:
            s = _dot_nt(qs, k_of(slot))
            if bias is not None:
                s = s + bias
            s_scr[slot, :, col:col + n_keys] = s
            mx = jnp.maximum(mx, jnp.max(s, axis=-1, keepdims=True))
            col += n_keys
        sinks.append(sk)
        maxes.append(mx)
    for slot in range(N_HEAD_SLOTS):
        p_scr[slot] = jnp.exp2(s_scr[slot] - maxes[slot]).astype(BF16)
    for g in range(ATTN_KV_HEADS):
        res = []
        for side in range(2):
            slot = 2 * g + side
            part = None
            col = 0
            for _, v_of, _, n_keys in key_blocks:
                pv = _dot(p_scr[slot, :, col:col + n_keys], v_of(slot))
                part = pv if part is None else part + pv
                col += n_keys
            den = pltpu.roll(part, HEAD_DIM, 1) + jnp.exp2(sinks[slot] - maxes[slot])
            res.append(part / den)
        out = jnp.where(low, res[0], res[1]).astype(o_ref.dtype)
        c0 = 2 * g * LANES
        o_ref[pl.ds(orow0, rows), c0:c0 + LANES] = out[:rows]
        o_ref[pl.ds(orow0, rows), c0 + LANES:c0 + 2 * LANES] = out[rows:]


def _attn_lat_kernel(sink_ref, q_ref, k_ref, v_ref, ck_ref, cv_ref, o_ref,
                     kl_ref, vl_ref, kc_ref, vc_ref, s_scr, p_scr):
    nb = DEC_SEQ // BLOCK
    zeros = jnp.zeros((BLOCK, LANES), BF16)
    for slot in range(N_HEAD_SLOTS):
        for ref in (kl_ref, vl_ref):
            ref[slot, pl.ds(0, BLOCK), :] = zeros
            ref[slot, pl.ds(DEC_SEQ + BLOCK, BLOCK), :] = zeros
    _fill_head_slots(k_ref[...].astype(F32), kl_ref, vl_ref, BLOCK, False)
    _fill_head_slots(v_ref[...].astype(F32), kl_ref, vl_ref, BLOCK, True)
    _fill_head_slots(ck_ref[...].T, kc_ref, vc_ref, 0, False)
    _fill_head_slots(cv_ref[...].T, kc_ref, vc_ref, 0, True)
    n_band = 3 * BLOCK
    qi = lax.broadcasted_iota(jnp.int32, (2 * BLOCK, n_band), 0) % BLOCK
    u = lax.broadcasted_iota(jnp.int32, (2 * BLOCK, n_band), 1)
    in_window = jnp.abs(qi + BLOCK - u) <= WINDOW

    def body(n, carry):
        r0 = pl.multiple_of(n * BLOCK, BLOCK)
        kpos = (n - 1) * BLOCK + u
        ok = in_window & (kpos >= 0) & (kpos < nb * BLOCK)
        bias = jnp.where(ok, 0.0, NEG_BIG)
        band = (lambda s: kl_ref[s, pl.ds(r0, n_band), :], lambda s: vl_ref[s, pl.ds(r0, n_band), :],
                bias, n_band)
        ctx = (lambda s: kc_ref[s], lambda s: vc_ref[s], None, PAST_LEN)
        _attend_rows(q_ref[pl.ds(r0, BLOCK), :], [band, ctx], sink_ref, o_ref, r0, s_scr, p_scr)
        return carry

    lax.fori_loop(0, nb, body, 0)


def _attn_latent(sink, q, k, v, ck, cv, j):
    seq = lambda n: pl.BlockSpec((DEC_SEQ, n), lambda b: (b, 0))
    ctx = pl.BlockSpec((None, None, KV_DIM, PAST_LEN), lambda b: (b, j, 0, 0))
    return pl.pallas_call(
        _attn_lat_kernel,
        out_shape=jax.ShapeDtypeStruct((N_SAMPLE_TOK, ATTN_DIM), BF16),
        grid=(DEC_BATCH,),
        in_specs=[pl.BlockSpec(memory_space=pltpu.SMEM), seq(ATTN_DIM), seq(KV_DIM), seq(KV_DIM), ctx, ctx],
        out_specs=seq(ATTN_DIM),
        scratch_shapes=[
            pltpu.VMEM((N_HEAD_SLOTS, DEC_SEQ + 2 * BLOCK, LANES), BF16),
            pltpu.VMEM((N_HEAD_SLOTS, DEC_SEQ + 2 * BLOCK, LANES), BF16),
            pltpu.VMEM((N_HEAD_SLOTS, PAST_LEN, LANES), BF16),
            pltpu.VMEM((N_HEAD_SLOTS, PAST_LEN, LANES), BF16),
            pltpu.VMEM((N_HEAD_SLOTS, 2 * BLOCK, 3 * BLOCK + PAST_LEN), F32),
            pltpu.VMEM((N_HEAD_SLOTS, 2 * BLOCK, 3 * BLOCK + PAST_LEN), BF16),
        ],
        compiler_params=_params(("arbitrary",)),
        name="attn_latent",
    )(sink, q, k, v, ck, cv)


def _attn_ctx_kernel(sink_ref, q_ref, k_ref, v_ref, o_ref, ks_ref, vs_ref, s_scr, p_scr):
    _fill_head_slots(k_ref[...].astype(F32), ks_ref, vs_ref, 0, False)
    _fill_head_slots(v_ref[...].astype(F32), ks_ref, vs_ref, 0, True)
    blk = (lambda s: ks_ref[s], lambda s: vs_ref[s], None, SEQ)
    _attend_rows(q_ref[...], [blk], sink_ref, o_ref, 0, s_scr, p_scr)


def _attn_context(sink, q, k, v):
    off = N_SAMPLE_TOK // SEQ
    row = lambda n: pl.BlockSpec((SEQ, n), lambda b: (off + b, 0))
    return pl.pallas_call(
        _attn_ctx_kernel,
        out_shape=jax.ShapeDtypeStruct((N_PROMPT_TOK, ATTN_DIM), BF16),
        grid=(BATCH,),
        in_specs=[pl.BlockSpec(memory_space=pltpu.SMEM), row(ATTN_DIM), row(KV_DIM), row(KV_DIM)],
        out_specs=pl.BlockSpec((SEQ, ATTN_DIM), lambda b: (b, 0)),
        scratch_shapes=[pltpu.VMEM((N_HEAD_SLOTS, SEQ, LANES), BF16),
                        pltpu.VMEM((N_HEAD_SLOTS, SEQ, LANES), BF16),
                        pltpu.VMEM((N_HEAD_SLOTS, 2 * SEQ, SEQ), F32),
                        pltpu.VMEM((N_HEAD_SLOTS, 2 * SEQ, SEQ), BF16)],
        compiler_params=_params(("arbitrary",)),
        name="attn_context",
    )(sink, q, k, v)


CONV_COLS = 256


def _conv_kernel(x_ref, m_ref, g_ref, win_ref, cw_ref, cb_ref, wout_ref, o_ref, a_ref):
    i = pl.program_id(0)
    h = _mixer_in(x_ref, m_ref, g_ref)
    tm = h.shape[0]
    seq = jnp.where(i < N_SAMPLE_TOK // tm, DEC_SEQ, SEQ)
    pos = lax.broadcasted_iota(jnp.int32, (tm, CONV_COLS), 0) & (seq - 1)
    first = pos == 0
    last = pos == seq - 1
    cw = cw_ref[...]
    cb = cb_ref[...]
    for c in range(D_MODEL // CONV_COLS):
        cols = slice(c * CONV_COLS, (c + 1) * CONV_COLS)
        part = lambda j: _dot(
            h, win_ref[:, j * D_MODEL + c * CONV_COLS:j * D_MODEL + (c + 1) * CONV_COLS].astype(BF16))
        z = part(1) * part(2)
        zl = jnp.where(first, 0.0, pltpu.roll(z, 1, 0))
        zr = jnp.where(last, 0.0, pltpu.roll(z, tm - 1, 0))
        conv = cb[:, cols] + zl * cw[0:1, cols] + z * cw[1:2, cols] + zr * cw[2:3, cols]
        a_ref[:, cols] = (part(0) * conv).astype(BF16)
    y = _rms(_dot(a_ref[...], wout_ref[...].astype(BF16))) * g_ref[3:4, :]
    o_ref[...] = x_ref[...] + m_ref[5:6, :] * y


def _conv_layer(x, mods, norm_g, w_in, cw, cb, w_out, li, j):
    tm = CONV_TM
    return pl.pallas_call(
        _conv_kernel,
        out_shape=jax.ShapeDtypeStruct((N_TOK, D_MODEL), F32),
        grid=(N_TOK // tm,),
        in_specs=[
            _tok_spec(tm, D_MODEL), _mods_spec(tm), _norm_spec(li), _resident(w_in, j),
            pl.BlockSpec((None, 3, D_MODEL), lambda i: (j, 0, 0)),
            pl.BlockSpec((None, 1, D_MODEL), lambda i: (j, 0, 0)),
            _resident(w_out, j),
        ],
        out_specs=_tok_spec(tm, D_MODEL),
        scratch_shapes=[pltpu.VMEM((tm, D_MODEL), BF16)],
        compiler_params=_params(("arbitrary",)),
        name="conv_layer",
    )(x, mods, norm_g, w_in, cw, cb.reshape(-1, 1, D_MODEL), w_out)


def _block_row(x, w, off):
    c = x.shape[0]
    blk = 2 * w
    row = lambda r: jnp.broadcast_to(x[r:r + 1, :], (max(blk, SUBLANES), LANES))
    if blk >= SUBLANES:
        return jnp.concatenate([row(j * blk + off) for j in range(c // blk)], axis=0)
    if blk == 4:
        upper = lax.broadcasted_iota(jnp.int32, (SUBLANES, LANES), 0) < 4
        return jnp.concatenate(
            [jnp.where(upper, row(SUBLANES * v + off), row(SUBLANES * v + 4 + off))
             for v in range(c // SUBLANES)], axis=0)
    odd = (lax.broadcasted_iota(jnp.int32, (c, LANES), 0) & 1) != 0
    if off == 0:
        return jnp.where(odd, pltpu.roll(x, 1, 0), x)
    return jnp.where(odd, x, pltpu.roll(x, c - 1, 0))


def _split3(x):
    hi = x.astype(BF16)
    r = x - hi.astype(F32)
    mid = r.astype(BF16)
    lo = (r - mid.astype(F32)).astype(BF16)
    return jnp.concatenate([hi, mid, lo], axis=0)


def _hgrn_kernel(*refs, seq, li, hp, has_init, emit_state):
    lb_ref, lv_ref, zq_ref, zi_ref, zf_ref, zb_ref, zg_ref, ng_ref = refs[:8]
    refs = refs[8:]
    s0_ref = None
    if has_init:
        s0_ref, refs = refs[0], refs[1:]
    o_ref, refs = refs[0], refs[1:]
    sout_ref = None
    if emit_state:
        sout_ref, refs = refs[0], refs[1:]
    acc_ref, qf_ref, qb_ref, kv_ref, dec_ref, stf_ref, stb_ref = refs
    c = GLA_CHUNK
    nc = seq // c

    def lower_bound(x):
        e = jnp.exp(x - jnp.max(x, axis=0, keepdims=True))
        pr = e / jnp.sum(e, axis=0, keepdims=True)
        return jnp.sum(pr[1:li + 1], axis=0, keepdims=True)

    lbf, lbb = lower_bound(lb_ref[0]), lower_bound(lb_ref[1])
    ri = lax.broadcasted_iota(jnp.int32, (c, c), 0)
    ci = lax.broadcasted_iota(jnp.int32, (c, c), 1)
    tril = jnp.where(ci <= ri, 1.0, 0.0).astype(BF16)
    triu = jnp.where(ci >= ri, 1.0, 0.0).astype(BF16)
    tril3 = jnp.concatenate([tril, tril, tril], axis=1)
    triu3 = jnp.concatenate([triu, triu, triu], axis=1)
    lv = lv_ref[...]
    rowi = lax.broadcasted_iota(jnp.int32, (c, LANES), 0)

    def chunk_local(n, hd):
        rows = pl.ds(pl.multiple_of(n * c, c), c)
        cols = slice(hd * LANES, (hd + 1) * LANES)
        q = _silu(zq_ref[rows, cols]) * (HGRN_DK ** -0.5)
        zi = zi_ref[rows, cols]
        v = zi.astype(BF16)
        ff = lbf[:, cols] + (1.0 - lbf[:, cols]) * jax.nn.sigmoid(zf_ref[rows, cols])
        fb = lbb[:, cols] + (1.0 - lbb[:, cols]) * jax.nn.sigmoid(zb_ref[rows, cols])
        kf = 1.0 - ff
        kb = 1.0 - fb
        pre = _dot(tril3, _split3(jnp.log2(ff)))
        suf = _dot(triu3, _split3(jnp.log2(fb)))
        attn = jnp.where(lv == 0, _dot_nt(q.astype(BF16), (kf + kb).astype(BF16)), 0.0)
        odd = (rowi & 1) != 0
        qh = (q * jnp.where(odd, ff, fb)).astype(BF16)
        kh = jnp.where(odd, kb, kf).astype(BF16)
        attn = jnp.where(lv == 1, _dot_nt(qh, kh), attn)
        w, level = 2, 2
        while w < c:
            d_f = pre - _block_row(pre, w, w - 1)
            d_b = suf - _block_row(suf, w, w)
            xq = jnp.minimum(d_f, d_b)
            xk = xq - (d_f + d_b)
            second = (rowi & w) != 0
            qh = (q * jnp.exp2(xq)).astype(BF16)
            kh = (jnp.where(second, kb, kf) * jnp.exp2(xk)).astype(BF16)
            attn = jnp.where(lv == level, _dot_nt(qh, kh), attn)
            w *= 2
            level += 1
        acc_ref[hd, rows, :] = _dot(attn.astype(BF16), v)
        tot_f = pre[c - 1:c, :]
        tot_b = suf[0:1, :]
        qf_ref[hd, rows, :] = (q * jnp.exp2(pre)).astype(BF16)
        qb_ref[hd, rows, :] = (q * jnp.exp2(suf)).astype(BF16)
        kt = jnp.concatenate([(kf * jnp.exp2(tot_f - pre)).astype(BF16),
                              (kb * jnp.exp2(tot_b - suf)).astype(BF16)], axis=1)
        kv_ref[hd, n] = _dot(zi.T.astype(BF16), kt)
        dec_ref[hd, n] = jnp.exp2(jnp.concatenate([tot_f, tot_b], axis=1))

    def local_body(n, carry):
        for hd in range(hp):
            chunk_local(n, hd)
        return carry

    lax.fori_loop(0, nc, local_body, 0, unroll=min(nc, GLA_ROWS_IN_FLIGHT // (c * hp)))

    for hd in range(hp):
        if has_init:
            stf_ref[hd] = s0_ref[0, hd].T
            stb_ref[hd] = s0_ref[1, hd].T
        else:
            stf_ref[hd] = jnp.zeros((HGRN_DV, HGRN_DK), F32)
            stb_ref[hd] = jnp.zeros((HGRN_DV, HGRN_DK), F32)

    def carry_state(i, carry):
        n, m = i, nc - 1 - i
        rn = pl.ds(pl.multiple_of(n * c, c), c)
        rm = pl.ds(pl.multiple_of(m * c, c), c)
        for hd in range(hp):
            stf = stf_ref[hd]
            stb = stb_ref[hd]
            acc_ref[hd, rn, :] += _dot_nt(qf_ref[hd, rn, :], stf.astype(BF16))
            acc_ref[hd, rm, :] += _dot_nt(qb_ref[hd, rm, :], stb.astype(BF16))
            stf_ref[hd] = stf * dec_ref[hd, n][:, :HGRN_DK] + kv_ref[hd, n][:, :HGRN_DK]
            stb_ref[hd] = stb * dec_ref[hd, m][:, HGRN_DK:] + kv_ref[hd, m][:, HGRN_DK:]
        return carry

    lax.fori_loop(0, nc, carry_state, 0, unroll=True)
    for hd in range(hp):
        cols = slice(hd * LANES, (hd + 1) * LANES)
        o_ref[:, cols] = (_rms(acc_ref[hd]) * ng_ref[...] * _silu(zg_ref[:, cols])).astype(o_ref.dtype)
        if emit_state:
            sout_ref[0, hd] = stf_ref[hd].T
            sout_ref[1, hd] = stb_ref[hd].T


def _pair_levels():
    t = np.arange(GLA_CHUNK)
    x = t[:, None] ^ t[None, :]
    return jnp.asarray(np.where(x > 0, np.floor(np.log2(np.maximum(x, 1))) + 1, 0).astype(np.int32))


def _hgrn_core(z, lb, ng, s0, *, j, batch, seq, row_off, li, emit_state):
    has_init = s0 is not None
    blk0 = row_off // seq
    fd = HGRN_FD // LANES
    dm = D_MODEL // LANES
    c = GLA_CHUNK
    hp = max(1, GLA_ROWS_IN_FLIGHT // seq)
    width = hp * LANES

    def col(c0):
        return pl.BlockSpec((seq, width), lambda b, h: (blk0 + b, c0 // hp + h))

    in_specs = [
        pl.BlockSpec((2, DEPTH, width), lambda b, h: (0, 0, h)),
        pl.BlockSpec((c, c), lambda b, h: (0, 0)),
        col(0), col(fd), col(fd + dm), col(2 * fd + dm), col(3 * fd + dm),
        pl.BlockSpec((None, 1, HGRN_DV), lambda b, h: (j, 0, 0)),
    ]
    args = [lb, _pair_levels(), z, z, z, z, z, ng.reshape(-1, 1, HGRN_DV)]
    if has_init:
        in_specs.append(pl.BlockSpec((None, None, 2, hp, HGRN_DK, HGRN_DV),
                                     lambda b, h: (b, j, 0, h, 0, 0)))
        args.append(s0)
    out_shape = [jax.ShapeDtypeStruct((batch * seq, D_MODEL), BF16)]
    out_specs = [pl.BlockSpec((seq, width), lambda b, h: (b, h))]
    if emit_state:
        out_shape.append(jax.ShapeDtypeStruct((batch, 2, HGRN_HEADS, HGRN_DK, HGRN_DV), F32))
        out_specs.append(pl.BlockSpec((None, 2, hp, HGRN_DK, HGRN_DV), lambda b, h: (b, 0, h, 0, 0)))
    return pl.pallas_call(
        functools.partial(_hgrn_kernel, seq=seq, li=li, hp=hp, has_init=has_init, emit_state=emit_state),
        out_shape=tuple(out_shape),
        grid=(batch, HGRN_HEADS // hp),
        in_specs=in_specs,
        out_specs=tuple(out_specs),
        scratch_shapes=[
            pltpu.VMEM((hp, seq, HGRN_DV), F32),
            pltpu.VMEM((hp, seq, HGRN_DK), BF16),
            pltpu.VMEM((hp, seq, HGRN_DK), BF16),
            pltpu.VMEM((hp, seq // c, HGRN_DV, 2 * HGRN_DK), F32),
            pltpu.VMEM((hp, seq // c, 1, 2 * HGRN_DK), F32),
            pltpu.VMEM((hp, HGRN_DV, HGRN_DK), F32),
            pltpu.VMEM((hp, HGRN_DV, HGRN_DK), F32),
        ],
        compiler_params=_params(("arbitrary", "arbitrary")),
        name=f"hgrn_core_{seq}",
    )(*args)


def kernel(x_prompt, x_sample, cache_k, cache_v, state_hgrn, c, c_ctx, w_ada, b_ada, norm_g, w_ffn_in, w_ffn_out, w_attn_qkv, w_attn_o, attn_sink, w_conv_in, conv_w, conv_b, w_conv_out, w_hgrn_in, hgrn_lb, hgrn_norm_g, w_hgrn_out):
    cond = jnp.concatenate([c, c_ctx[None, :],
                            jnp.zeros((N_COND - DEC_BATCH - 1, D_MODEL), F32)], axis=0)
    b_ada = b_ada.reshape(DEPTH, 1, ADA_COLS)
    mods = _ada_table(cond, w_ada, b_ada, 0)
    cos, sin = _rope_tables()
    ck = cache_k.transpose(0, 1, 3, 4, 2).reshape(DEC_BATCH, -1, KV_DIM, PAST_LEN)
    cv = cache_v.transpose(0, 1, 3, 4, 2).reshape(DEC_BATCH, -1, KV_DIM, PAST_LEN)
    cache_out = lambda t: t.reshape(BATCH, -1, ATTN_KV_HEADS, HEAD_DIM, SEQ).transpose(0, 1, 4, 2, 3)
    bf = lambda w: w.astype(BF16)
    xs = (x_sample.reshape(N_SAMPLE_TOK, D_MODEL), x_prompt.reshape(N_PROMPT_TOK, D_MODEL))
    ffn_w = (bf(w_ffn_in[0, 0]), bf(w_ffn_out[0, 0]))
    new_cache, new_s = (), []
    for li in range(DEPTH):
        kind, j = li % N_MIXERS, li // N_MIXERS
        last = li == DEPTH - 1
        mixer = None
        ada = None if last else (cond, w_ada, b_ada, li + 1)
        (x,), ffn_w, mods_next = _ffn(xs, mods, norm_g, ffn_w, (w_ffn_in, w_ffn_out, li, 1), li, 0, ada=ada)
        if kind == 0:
            q, k, v, *new_cache = _qkv(x, mods, norm_g, w_attn_qkv, li, j, cos, sin, new_cache)
            o_s = _attn_latent(attn_sink[j], q, k, v, ck, cv, j)
            o_p = _attn_context(attn_sink[j], q, k, v)
            mixer = (o_s, o_p, w_attn_o, j)
        elif kind == 1:
            x = _conv_layer(x, mods, norm_g, w_conv_in, conv_w, conv_b, w_conv_out, li, j)
        else:
            z = _proj(x, mods, norm_g, w_hgrn_in, li, j, F32)
            o_s = _hgrn_core(z, hgrn_lb, hgrn_norm_g, state_hgrn, j=j, batch=DEC_BATCH,
                             seq=DEC_SEQ, row_off=0, li=li, emit_state=False)[0]
            o_p, st = _hgrn_core(z, hgrn_lb, hgrn_norm_g, None, j=j, batch=BATCH, seq=SEQ,
                                 row_off=N_SAMPLE_TOK, li=li, emit_state=True)
            mixer = (o_s, o_p, w_hgrn_out, j)
            new_s.append(st)
        nxt = None if last else (w_ffn_in, w_ffn_out, li + 1, 0)
        xs, ffn_w, _ = _ffn((x,), mods, norm_g, ffn_w, nxt, li, 1, split_out=last, mixer=mixer)
        mods = mods_next
    y_sample = xs[0].reshape(DEC_BATCH, DEC_SEQ, D_MODEL)
    y_prompt = xs[1].reshape(BATCH, SEQ, D_MODEL)
    return (y_prompt, y_sample, cache_out(new_cache[0]), cache_out(new_cache[1]), jnp.stack(new_s, axis=1))
```

```python
import functools
import math

import jax
import jax.numpy as jnp
import numpy as np
from jax import lax
from jax.experimental import pallas as pl
from jax.experimental.pallas import tpu as pltpu

D_MODEL = 1024
BATCH = 16
SEQ = 256
DEPTH = 4
DEC_BATCH = 8
DEC_SEQ = 1024
PAST_LEN = 256
GRID_W = 64
N_MIXERS = 3
N_MOD = 9
N_NORMS = 6
D_FF = 2816
FFN_RES = 0.5
ATTN_HEADS = 16
ATTN_KV_HEADS = 4
HEAD_DIM = 64
ATTN_DIM = ATTN_HEADS * HEAD_DIM
KV_DIM = ATTN_KV_HEADS * HEAD_DIM
BLOCK = 128
WINDOW = 128
ATTN_SCALE = HEAD_DIM ** -0.5
LOG2E = math.log2(math.e)
ROPE_BASE = 10000.0
ROPE_PAIRS_PER_AXIS = HEAD_DIM // 4
HGRN_HEADS = 8
HGRN_DK = 128
HGRN_DV = D_MODEL // HGRN_HEADS
HGRN_FD = HGRN_HEADS * HGRN_DK
EPS = 1e-6

N_ATTN_LAYERS = len(range(0, DEPTH, N_MIXERS))
N_SAMPLE_TOK = DEC_BATCH * DEC_SEQ
N_PROMPT_TOK = BATCH * SEQ
N_TOK = N_SAMPLE_TOK + N_PROMPT_TOK
N_COND = 16
CTX_COND = DEC_BATCH

LANES = 128
SUBLANES = 8
TM = 512
CONV_TM = 1024
QKV_TM = 1024
FFN_SPLITS = (0, 1024, 2048, D_FF)
GLA_CHUNK = 128
GLA_ROWS_IN_FLIGHT = 2048
NEG_BIG = -1e30
VMEM_LIMIT = 56 * 1024 * 1024

F32 = jnp.float32
BF16 = jnp.bfloat16


def _rms(x):
    return x * lax.rsqrt(jnp.mean(x * x, axis=-1, keepdims=True) + EPS)


def _silu(x):
    return x * jax.nn.sigmoid(x)


def _dot(a, b):
    return jnp.dot(a, b, preferred_element_type=F32)


def _dot_nt(a, b):
    return lax.dot_general(a, b, (((1,), (1,)), ((), ())), preferred_element_type=F32)


def _params(sem):
    return pltpu.CompilerParams(dimension_semantics=sem, vmem_limit_bytes=VMEM_LIMIT)


def _resident(arr, *lead):
    shape = (None,) * len(lead) + arr.shape[len(lead):]
    idx = tuple(lead) + (0,) * (arr.ndim - len(lead))
    return pl.BlockSpec(shape, lambda *_: idx, pipeline_mode=pl.Buffered(1))


def _mods_spec(tm):
    per = DEC_SEQ // tm
    return pl.BlockSpec((None, N_MOD, D_MODEL), lambda i: (jnp.minimum(i // per, CTX_COND), 0, 0))


def _norm_spec(li):
    return pl.BlockSpec((None, N_NORMS, D_MODEL), lambda i: (li, 0, 0))


def _tok_spec(tm, n):
    return pl.BlockSpec((tm, n), lambda i: (i, 0))


def _split_specs(tm, n):
    ns = N_SAMPLE_TOK // tm
    return [pl.BlockSpec((tm, n), lambda i: (jnp.minimum(i, ns - 1), 0)),
            pl.BlockSpec((tm, n), lambda i: (jnp.maximum(i - ns, 0), 0))]


def _load_split(s_ref, p_ref, tm):
    return jnp.where(pl.program_id(0) < N_SAMPLE_TOK // tm, s_ref[...], p_ref[...])


def _store_split(s_ref, p_ref, val, tm):
    ns = N_SAMPLE_TOK // tm
    i = pl.program_id(0)

    @pl.when(i < ns)
    def _():
        s_ref[...] = val

    @pl.when(i >= ns)
    def _():
        p_ref[...] = val


def _ada_kernel(c_ref, w_ref, b_ref, o_ref):
    s = _silu(c_ref[...]).astype(BF16)
    o_ref[...] = _dot(s, w_ref[...].astype(BF16)) + b_ref[...]


ADA_COLS = N_MOD * D_MODEL
ADA_SIDE_COLS = ADA_COLS // (N_TOK // TM)


def _ada_table(cond, w_ada, b_ada, li):
    tn = 3 * D_MODEL
    out = pl.pallas_call(
        _ada_kernel,
        out_shape=jax.ShapeDtypeStruct((N_COND, ADA_COLS), F32),
        grid=(ADA_COLS // tn,),
        in_specs=[
            pl.BlockSpec((N_COND, D_MODEL), lambda j: (0, 0)),
            pl.BlockSpec((None, D_MODEL, tn), lambda j: (li, 0, j)),
            pl.BlockSpec((None, 1, tn), lambda j: (li, 0, j)),
        ],
        out_specs=pl.BlockSpec((N_COND, tn), lambda j: (0, j)),
        compiler_params=_params(("arbitrary",)),
        name="ada_table",
    )(cond, w_ada, b_ada)
    return out.reshape(N_COND, N_MOD, D_MODEL)


NEXT_IN_ROWS = 64
NEXT_OUT_ROWS = 128


def _ffn_kernel(*refs, k0, g0, split_in, split_out, has_mixer, has_next, has_ada):
    n_in = 2 if split_in else 1
    x_refs, refs = refs[:n_in], refs[n_in:]
    mix_refs = ()
    if has_mixer:
        mix_refs, refs = refs[:3], refs[3:]
    m_ref, g_ref, win_ref, wout_ref = refs[:4]
    refs = refs[4:]
    n_side = (2 if has_next else 0) + (3 if has_ada else 0)
    side_in, outs = refs[:n_side], refs[n_side:]
    n_x = 2 if split_out else 1
    o_refs, side_out = outs[:n_x], outs[n_x:]
    if has_next:
        side_out[0][...] = side_in[0][...].astype(BF16)
        side_out[1][...] = side_in[1][...].astype(BF16)
        side_in, side_out = side_in[2:], side_out[2:]
    if has_ada:
        c_ref, wa_ref, ba_ref = side_in
        side_out[0][...] = _dot(_silu(c_ref[...]).astype(BF16), wa_ref[...].astype(BF16)) + ba_ref[...]
    x = _load_split(*x_refs, TM) if split_in else x_refs[0][...]
    m = m_ref[...]
    g = g_ref[...]
    if has_mixer:
        as_ref, ap_ref, wo_ref = mix_refs
        mixed = _dot(_load_split(as_ref, ap_ref, TM), wo_ref[...].astype(BF16))
        x = x + m[5:6] * (_rms(mixed) * g[3:4])
    h = _rms(x) * g[g0:g0 + 1]
    h = (h * (1.0 + m[k0 + 1:k0 + 2]) + m[k0:k0 + 1]).astype(BF16)
    acc = None
    for lo, hi in zip(FFN_SPLITS[:-1], FFN_SPLITS[1:]):
        a = _dot(h, win_ref[:, lo:hi])
        b = _dot(h, win_ref[:, D_FF + lo:D_FF + hi])
        u = (_silu(a) * b).astype(BF16)
        part = _dot(u, wout_ref[lo:hi, :])
        acc = part if acc is None else acc + part
    y = _rms(acc) * g[g0 + 1:g0 + 2]
    res = x + FFN_RES * m[k0 + 2:k0 + 3] * y
    if split_out:
        _store_split(*o_refs, res, TM)
    else:
        o_refs[0][...] = res


def _ffn(xs, mods, norm_g, w_bf, nxt, li, which, split_out=False, mixer=None, ada=None):
    k0, g0 = (0, 0) if which == 0 else (6, 4)
    split_in = len(xs) == 2
    steps = N_TOK // TM
    x_specs = _split_specs(TM, D_MODEL) if split_in else [_tok_spec(TM, D_MODEL)]
    mix_specs, mix_args = [], []
    if mixer is not None:
        a_s, a_p, w_o, j = mixer
        mix_specs = _split_specs(TM, D_MODEL) + [_resident(w_o, j)]
        mix_args = [a_s, a_p, w_o]
    if split_out:
        out_shape = [jax.ShapeDtypeStruct((N_SAMPLE_TOK, D_MODEL), F32),
                     jax.ShapeDtypeStruct((N_PROMPT_TOK, D_MODEL), F32)]
        out_specs = _split_specs(TM, D_MODEL)
    else:
        out_shape = [jax.ShapeDtypeStruct((N_TOK, D_MODEL), F32)]
        out_specs = [_tok_spec(TM, D_MODEL)]
    nxt_specs, nxt_args = [], []
    if nxt is not None:
        w_in_all, w_out_all, li2, which2 = nxt
        last_in = D_MODEL // NEXT_IN_ROWS - 1
        last_out = D_FF // NEXT_OUT_ROWS - 1
        assert last_in < steps and last_out < steps
        nxt_specs = [
            pl.BlockSpec((None, None, NEXT_IN_ROWS, 2 * D_FF),
                         lambda i: (li2, which2, jnp.minimum(i, last_in), 0)),
            pl.BlockSpec((None, None, NEXT_OUT_ROWS, D_MODEL),
                         lambda i: (li2, which2, jnp.minimum(i, last_out), 0)),
        ]
        nxt_args = [w_in_all, w_out_all]
        out_shape += [jax.ShapeDtypeStruct((D_MODEL, 2 * D_FF), BF16),
                      jax.ShapeDtypeStruct((D_FF, D_MODEL), BF16)]
        out_specs += [pl.BlockSpec((NEXT_IN_ROWS, 2 * D_FF), lambda i: (jnp.minimum(i, last_in), 0)),
                      pl.BlockSpec((NEXT_OUT_ROWS, D_MODEL), lambda i: (jnp.minimum(i, last_out), 0))]
    ada_specs, ada_args = [], []
    if ada is not None:
        cond, w_ada, b_ada, li_ada = ada
        ada_specs = [
            pl.BlockSpec((N_COND, D_MODEL), lambda i: (0, 0)),
            pl.BlockSpec((None, D_MODEL, ADA_SIDE_COLS), lambda i: (li_ada, 0, i)),
            pl.BlockSpec((None, 1, ADA_SIDE_COLS), lambda i: (li_ada, 0, i)),
        ]
        ada_args = [cond, w_ada, b_ada]
        out_shape += [jax.ShapeDtypeStruct((N_COND, ADA_COLS), F32)]
        out_specs += [pl.BlockSpec((N_COND, ADA_SIDE_COLS), lambda i: (0, i))]
    outs = pl.pallas_call(
        functools.partial(_ffn_kernel, k0=k0, g0=g0, split_in=split_in, split_out=split_out,
                          has_mixer=mixer is not None, has_next=nxt is not None, has_ada=ada is not None),
        out_shape=tuple(out_shape),
        grid=(steps,),
        in_specs=x_specs + mix_specs + [
            _mods_spec(TM), _norm_spec(li), _resident(w_bf[0]), _resident(w_bf[1])] + nxt_specs + ada_specs,
        out_specs=tuple(out_specs),
        compiler_params=_params(("arbitrary",)),
        name=f"ffn{which}",
    )(*xs, *mix_args, mods, norm_g, *w_bf, *nxt_args, *ada_args)
    n_x = 2 if split_out else 1
    n_w = 2 if nxt is not None else 0
    mods_next = outs[n_x + n_w].reshape(N_COND, N_MOD, D_MODEL) if ada is not None else None
    return tuple(outs[:n_x]), tuple(outs[n_x:n_x + n_w]), mods_next


def _mixer_in(x_ref, m_ref, g_ref):
    m = m_ref[...]
    h = _rms(x_ref[...]) * g_ref[2:3, :]
    return (h * (1.0 + m[4:5]) + m[3:4]).astype(BF16)


PROJ_COLS = 512


def _proj_kernel(x_ref, m_ref, g_ref, w_ref, o_ref):
    h = _mixer_in(x_ref, m_ref, g_ref)
    for c0 in range(0, o_ref.shape[1], PROJ_COLS):
        o_ref[:, c0:c0 + PROJ_COLS] = _dot(h, w_ref[:, c0:c0 + PROJ_COLS].astype(BF16)).astype(o_ref.dtype)


def _proj(x, mods, norm_g, w, li, j, out_dtype):
    n = w.shape[-1]
    return pl.pallas_call(
        _proj_kernel,
        out_shape=jax.ShapeDtypeStruct((N_TOK, n), out_dtype),
        grid=(N_TOK // TM,),
        in_specs=[_tok_spec(TM, D_MODEL), _mods_spec(TM), _norm_spec(li), _resident(w, j)],
        out_specs=_tok_spec(TM, n),
        compiler_params=_params(("arbitrary",)),
        name="mixer_proj",
    )(x, mods, norm_g, w)


def _rope_cols(t, cos, sin):
    lane = lax.broadcasted_iota(jnp.int32, (t.shape[0], LANES), 1)
    low = (lane % HEAD_DIM) < (HEAD_DIM // 2)
    outs = []
    for c in range(t.shape[1] // LANES):
        tc = t[:, c * LANES:(c + 1) * LANES]
        partner = jnp.where(low, pltpu.roll(tc, LANES - HEAD_DIM // 2, 1),
                            pltpu.roll(tc, HEAD_DIM // 2, 1))
        outs.append(tc * cos + partner * sin)
    return jnp.concatenate(outs, axis=1)


def _qkv_kernel(*refs):
    TM = QKV_TM
    x_ref, m_ref, g_ref, w_ref, cos_ref, sin_ref = refs[:6]
    q_ref, k_ref, v_ref, kf_ref, vf_ref = refs[-5:]
    h = _mixer_in(x_ref, m_ref, g_ref)
    cos = cos_ref[...]
    sin = sin_ref[...]
    for c0 in range(0, ATTN_DIM, KV_DIM):
        q = _dot(h, w_ref[:, c0:c0 + KV_DIM].astype(BF16))
        q_ref[:, c0:c0 + KV_DIM] = (_rope_cols(q, cos, sin) * (ATTN_SCALE * LOG2E)).astype(BF16)
    k = _dot(h, w_ref[:, ATTN_DIM:ATTN_DIM + KV_DIM].astype(BF16))
    v = _dot(h, w_ref[:, ATTN_DIM + KV_DIM:].astype(BF16))

    @pl.when(pl.program_id(0) >= N_SAMPLE_TOK // TM)
    def _():
        for s in range(TM // SEQ):
            kf_ref[s] = k[s * SEQ:(s + 1) * SEQ, :].T
            vf_ref[s] = v[s * SEQ:(s + 1) * SEQ, :].T

    k_ref[...] = _rope_cols(k, cos, sin).astype(BF16)
    v_ref[...] = v.astype(BF16)


def _rope_tables():
    TM = QKV_TM
    pos = np.arange(DEC_SEQ)
    row = (pos // GRID_W).astype(np.float32)
    col = (pos % GRID_W).astype(np.float32)
    inv = (ROPE_BASE ** (-np.arange(ROPE_PAIRS_PER_AXIS, dtype=np.float32) / ROPE_PAIRS_PER_AXIS)).astype(np.float32)
    ang = jnp.asarray(np.concatenate([row[:, None] * inv, col[:, None] * inv], axis=-1).astype(np.float32))
    cos, sin = jnp.cos(ang), jnp.sin(ang)
    cos = jnp.concatenate([cos, cos, cos, cos], axis=-1)
    sin = jnp.concatenate([-sin, sin, -sin, sin], axis=-1)
    cos = jnp.concatenate([cos, jnp.ones((TM, LANES), F32)], axis=0)
    sin = jnp.concatenate([sin, jnp.zeros((TM, LANES), F32)], axis=0)
    return cos, sin


def _qkv(x, mods, norm_g, w, li, j, cos, sin, new_cache):
    TM = QKV_TM
    per = DEC_SEQ // TM
    ns = N_SAMPLE_TOK // TM
    tab = pl.BlockSpec((TM, LANES), lambda i: (jnp.where(i < ns, i % per, per), 0))
    cache = pl.BlockSpec((TM // SEQ, None, KV_DIM, SEQ), lambda i: (jnp.maximum(i - ns, 0), j, 0, 0))
    n_in = 6
    return pl.pallas_call(
        _qkv_kernel,
        out_shape=(
            jax.ShapeDtypeStruct((N_TOK, ATTN_DIM), BF16),
            jax.ShapeDtypeStruct((N_TOK, KV_DIM), BF16),
            jax.ShapeDtypeStruct((N_TOK, KV_DIM), BF16),
            jax.ShapeDtypeStruct((BATCH, N_ATTN_LAYERS, KV_DIM, SEQ), F32),
            jax.ShapeDtypeStruct((BATCH, N_ATTN_LAYERS, KV_DIM, SEQ), F32),
        ),
        grid=(N_TOK // TM,),
        in_specs=[_tok_spec(TM, D_MODEL), _mods_spec(TM), _norm_spec(li), _resident(w, j), tab, tab]
                 + [pl.BlockSpec(memory_space=pl.ANY)] * len(new_cache),
        out_specs=(_tok_spec(TM, ATTN_DIM), _tok_spec(TM, KV_DIM), _tok_spec(TM, KV_DIM), cache, cache),
        input_output_aliases={n_in + t: 3 + t for t in range(len(new_cache))},
        compiler_params=_params(("arbitrary",)),
        name="attn_qkv",
    )(x, mods, norm_g, w, cos, sin, *new_cache)


N_HEAD_SLOTS = 2 * ATTN_KV_HEADS


def _fill_head_slots(x, k_dst, v_dst, row0, is_value):
    dst = v_dst if is_value else k_dst
    pad = 1.0 if is_value else 0.0
    rows = x.shape[0]
    low = lax.broadcasted_iota(jnp.int32, (rows, LANES), 1) < HEAD_DIM
    for g in range(ATTN_KV_HEADS):
        cg, pg = g // 2, g % 2
        xb = x[:, cg * LANES:(cg + 1) * LANES]
        xr = pltpu.roll(xb, HEAD_DIM, 1)
        x_lo, x_hi = (xb, xr) if pg == 0 else (xr, xb)
        dst[2 * g, pl.ds(row0, rows), :] = jnp.where(low, x_lo, pad).astype(BF16)
        dst[2 * g + 1, pl.ds(row0, rows), :] = jnp.where(low, pad, x_hi).astype(BF16)


def _attend_rows(q, key_blocks, sink_ref, o_ref, orow0, s_scr, p_scr):
    rows = q.shape[0]
    rid = lax.broadcasted_iota(jnp.int32, (2 * rows, 1), 0)
    low = lax.broadcasted_iota(jnp.int32, (2 * rows, LANES), 1) < HEAD_DIM
    sinks, maxes = [], []
    for slot in range(N_HEAD_SLOTS):
        g, side = slot // 2, slot % 2
        c0 = 2 * g * LANES
        qs = jnp.concatenate([q[:, c0:c0 + LANES], q[:, c0 + LANES:c0 + 2 * LANES]], axis=0)
        sk = jnp.where(rid < rows, sink_ref[4 * g + side], sink_ref[4 * g + 2 + side]) * LOG2E
        mx = sk
        col = 0
        for k_of, _, bias, n_keys in key_blocks:
            s = _dot_nt(qs, k_of(slot))
            if bias is not None:
                s = s + bias
            s_scr[slot, :, col:col + n_keys] = s
            mx = jnp.maximum(mx, jnp.max(s, axis=-1, keepdims=True))
            col += n_keys
        sinks.append(sk)
        maxes.append(mx)
    for slot in range(N_HEAD_SLOTS):
        p_scr[slot] = jnp.exp2(s_scr[slot] - maxes[slot]).astype(BF16)
    for g in range(ATTN_KV_HEADS):
        res = []
        for side in range(2):
            slot = 2 * g + side
            part = None
            col = 0
            for _, v_of, _, n_keys in key_blocks:
                pv = _dot(p_scr[slot, :, col:col + n_keys], v_of(slot))
                part = pv if part is None else part + pv
                col += n_keys
            den = pltpu.roll(part, HEAD_DIM, 1) + jnp.exp2(sinks[slot] - maxes[slot])
            res.append(part / den)
        out = jnp.where(low, res[0], res[1]).astype(o_ref.dtype)
        c0 = 2 * g * LANES
        o_ref[pl.ds(orow0, rows), c0:c0 + LANES] = out[:rows]
        o_ref[pl.ds(orow0, rows), c0 + LANES:c0 + 2 * LANES] = out[rows:]


def _attn_lat_kernel(sink_ref, q_ref, k_ref, v_ref, ck_ref, cv_ref, o_ref,
                     kl_ref, vl_ref, kc_ref, vc_ref, s_scr, p_scr):
    nb = DEC_SEQ // BLOCK
    zeros = jnp.zeros((BLOCK, LANES), BF16)
    for slot in range(N_HEAD_SLOTS):
        for ref in (kl_ref, vl_ref):
            ref[slot, pl.ds(0, BLOCK), :] = zeros
            ref[slot, pl.ds(DEC_SEQ + BLOCK, BLOCK), :] = zeros
    _fill_head_slots(k_ref[...].astype(F32), kl_ref, vl_ref, BLOCK, False)
    _fill_head_slots(v_ref[...].astype(F32), kl_ref, vl_ref, BLOCK, True)
    _fill_head_slots(ck_ref[...].T, kc_ref, vc_ref, 0, False)
    _fill_head_slots(cv_ref[...].T, kc_ref, vc_ref, 0, True)
    n_band = 3 * BLOCK
    qi = lax.broadcasted_iota(jnp.int32, (2 * BLOCK, n_band), 0) % BLOCK
    u = lax.broadcasted_iota(jnp.int32, (2 * BLOCK, n_band), 1)
    in_window = jnp.abs(qi + BLOCK - u) <= WINDOW

    def body(n, carry):
        r0 = pl.multiple_of(n * BLOCK, BLOCK)
        kpos = (n - 1) * BLOCK + u
        ok = in_window & (kpos >= 0) & (kpos < nb * BLOCK)
        bias = jnp.where(ok, 0.0, NEG_BIG)
        band = (lambda s: kl_ref[s, pl.ds(r0, n_band), :], lambda s: vl_ref[s, pl.ds(r0, n_band), :],
                bias, n_band)
        ctx = (lambda s: kc_ref[s], lambda s: vc_ref[s], None, PAST_LEN)
        _attend_rows(q_ref[pl.ds(r0, BLOCK), :], [band, ctx], sink_ref, o_ref, r0, s_scr, p_scr)
        return carry

    lax.fori_loop(0, nb, body, 0)


def _attn_latent(sink, q, k, v, ck, cv, j):
    seq = lambda n: pl.BlockSpec((DEC_SEQ, n), lambda b: (b, 0))
    ctx = pl.BlockSpec((None, None, KV_DIM, PAST_LEN), lambda b: (b, j, 0, 0))
    return pl.pallas_call(
        _attn_lat_kernel,
        out_shape=jax.ShapeDtypeStruct((N_SAMPLE_TOK, ATTN_DIM), BF16),
        grid=(DEC_BATCH,),
        in_specs=[pl.BlockSpec(memory_space=pltpu.SMEM), seq(ATTN_DIM), seq(KV_DIM), seq(KV_DIM), ctx, ctx],
        out_specs=seq(ATTN_DIM),
        scratch_shapes=[
            pltpu.VMEM((N_HEAD_SLOTS, DEC_SEQ + 2 * BLOCK, LANES), BF16),
            pltpu.VMEM((N_HEAD_SLOTS, DEC_SEQ + 2 * BLOCK, LANES), BF16),
            pltpu.VMEM((N_HEAD_SLOTS, PAST_LEN, LANES), BF16),
            pltpu.VMEM((N_HEAD_SLOTS, PAST_LEN, LANES), BF16),
            pltpu.VMEM((N_HEAD_SLOTS, 2 * BLOCK, 3 * BLOCK + PAST_LEN), F32),
            pltpu.VMEM((N_HEAD_SLOTS, 2 * BLOCK, 3 * BLOCK + PAST_LEN), BF16),
        ],
        compiler_params=_params(("arbitrary",)),
        name="attn_latent",
    )(sink, q, k, v, ck, cv)


def _attn_ctx_kernel(sink_ref, q_ref, k_ref, v_ref, o_ref, ks_ref, vs_ref, s_scr, p_scr):
    _fill_head_slots(k_ref[...].astype(F32), ks_ref, vs_ref, 0, False)
    _fill_head_slots(v_ref[...].astype(F32), ks_ref, vs_ref, 0, True)
    blk = (lambda s: ks_ref[s], lambda s: vs_ref[s], None, SEQ)
    _attend_rows(q_ref[...], [blk], sink_ref, o_ref, 0, s_scr, p_scr)


def _attn_context(sink, q, k, v):
    off = N_SAMPLE_TOK // SEQ
    row = lambda n: pl.BlockSpec((SEQ, n), lambda b: (off + b, 0))
    return pl.pallas_call(
        _attn_ctx_kernel,
        out_shape=jax.ShapeDtypeStruct((N_PROMPT_TOK, ATTN_DIM), BF16),
        grid=(BATCH,),
        in_specs=[pl.BlockSpec(memory_space=pltpu.SMEM), row(ATTN_DIM), row(KV_DIM), row(KV_DIM)],
        out_specs=pl.BlockSpec((SEQ, ATTN_DIM), lambda b: (b, 0)),
        scratch_shapes=[pltpu.VMEM((N_HEAD_SLOTS, SEQ, LANES), BF16),
                        pltpu.VMEM((N_HEAD_SLOTS, SEQ, LANES), BF16),
                        pltpu.VMEM((N_HEAD_SLOTS, 2 * SEQ, SEQ), F32),
                        pltpu.VMEM((N_HEAD_SLOTS, 2 * SEQ, SEQ), BF16)],
        compiler_params=_params(("arbitrary",)),
        name="attn_context",
    )(sink, q, k, v)


CONV_COLS = 256


def _conv_kernel(x_ref, m_ref, g_ref, win_ref, cw_ref, cb_ref, wout_ref, o_ref, a_ref):
    i = pl.program_id(0)
    h = _mixer_in(x_ref, m_ref, g_ref)
    tm = h.shape[0]
    seq = jnp.where(i < N_SAMPLE_TOK // tm, DEC_SEQ, SEQ)
    pos = lax.broadcasted_iota(jnp.int32, (tm, CONV_COLS), 0) & (seq - 1)
    first = pos == 0
    last = pos == seq - 1
    cw = cw_ref[...]
    cb = cb_ref[...]
    for c in range(D_MODEL // CONV_COLS):
        cols = slice(c * CONV_COLS, (c + 1) * CONV_COLS)
        part = lambda j: _dot(
            h, win_ref[:, j * D_MODEL + c * CONV_COLS:j * D_MODEL + (c + 1) * CONV_COLS].astype(BF16))
        z = part(1) * part(2)
        zl = jnp.where(first, 0.0, pltpu.roll(z, 1, 0))
        zr = jnp.where(last, 0.0, pltpu.roll(z, tm - 1, 0))
        conv = cb[:, cols] + zl * cw[0:1, cols] + z * cw[1:2, cols] + zr * cw[2:3, cols]
        a_ref[:, cols] = (part(0) * conv).astype(BF16)
    y = _rms(_dot(a_ref[...], wout_ref[...].astype(BF16))) * g_ref[3:4, :]
    o_ref[...] = x_ref[...] + m_ref[5:6, :] * y


def _conv_layer(x, mods, norm_g, w_in, cw, cb, w_out, li, j):
    tm = CONV_TM
    return pl.pallas_call(
        _conv_kernel,
        out_shape=jax.ShapeDtypeStruct((N_TOK, D_MODEL), F32),
        grid=(N_TOK // tm,),
        in_specs=[
            _tok_spec(tm, D_MODEL), _mods_spec(tm), _norm_spec(li), _resident(w_in, j),
            pl.BlockSpec((None, 3, D_MODEL), lambda i: (j, 0, 0)),
            pl.BlockSpec((None, 1, D_MODEL), lambda i: (j, 0, 0)),
            _resident(w_out, j),
        ],
        out_specs=_tok_spec(tm, D_MODEL),
        scratch_shapes=[pltpu.VMEM((tm, D_MODEL), BF16)],
        compiler_params=_params(("arbitrary",)),
        name="conv_layer",
    )(x, mods, norm_g, w_in, cw, cb.reshape(-1, 1, D_MODEL), w_out)


def _block_row(x, w, off):
    c = x.shape[0]
    blk = 2 * w
    row = lambda r: jnp.broadcast_to(x[r:r + 1, :], (max(blk, SUBLANES), LANES))
    if blk >= SUBLANES:
        return jnp.concatenate([row(j * blk + off) for j in range(c // blk)], axis=0)
    if blk == 4:
        upper = lax.broadcasted_iota(jnp.int32, (SUBLANES, LANES), 0) < 4
        return jnp.concatenate(
            [jnp.where(upper, row(SUBLANES * v + off), row(SUBLANES * v + 4 + off))
             for v in range(c // SUBLANES)], axis=0)
    odd = (lax.broadcasted_iota(jnp.int32, (c, LANES), 0) & 1) != 0
    if off == 0:
        return jnp.where(odd, pltpu.roll(x, 1, 0), x)
    return jnp.where(odd, x, pltpu.roll(x, c - 1, 0))


def _split3(x):
    hi = x.astype(BF16)
    r = x - hi.astype(F32)
    mid = r.astype(BF16)
    lo = (r - mid.astype(F32)).astype(BF16)
    return jnp.concatenate([hi, mid, lo], axis=0)


def _hgrn_kernel(*refs, seq, li, hp, has_init, emit_state):
    lb_ref, lv_ref, zq_ref, zi_ref, zf_ref, zb_ref, zg_ref, ng_ref = refs[:8]
    refs = refs[8:]
    s0_ref = None
    if has_init:
        s0_ref, refs = refs[0], refs[1:]
    o_ref, refs = refs[0], refs[1:]
    sout_ref = None
    if emit_state:
        sout_ref, refs = refs[0], refs[1:]
    acc_ref, qf_ref, qb_ref, kv_ref, dec_ref, stf_ref, stb_ref = refs
    c = GLA_CHUNK
    nc = seq // c

    def lower_bound(x):
        e = jnp.exp(x - jnp.max(x, axis=0, keepdims=True))
        pr = e / jnp.sum(e, axis=0, keepdims=True)
        return jnp.sum(pr[1:li + 1], axis=0, keepdims=True)

    lbf, lbb = lower_bound(lb_ref[0]), lower_bound(lb_ref[1])
    ri = lax.broadcasted_iota(jnp.int32, (c, c), 0)
    ci = lax.broadcasted_iota(jnp.int32, (c, c), 1)
    tril = jnp.where(ci <= ri, 1.0, 0.0).astype(BF16)
    triu = jnp.where(ci >= ri, 1.0, 0.0).astype(BF16)
    tril3 = jnp.concatenate([tril, tril, tril], axis=1)
    triu3 = jnp.concatenate([triu, triu, triu], axis=1)
    lv = lv_ref[...]
    rowi = lax.broadcasted_iota(jnp.int32, (c, LANES), 0)

    def chunk_local(n, hd):
        rows = pl.ds(pl.multiple_of(n * c, c), c)
        cols = slice(hd * LANES, (hd + 1) * LANES)
        q = _silu(zq_ref[rows, cols]) * (HGRN_DK ** -0.5)
        zi = zi_ref[rows, cols]
        v = zi.astype(BF16)
        ff = lbf[:, cols] + (1.0 - lbf[:, cols]) * jax.nn.sigmoid(zf_ref[rows, cols])
        fb = lbb[:, cols] + (1.0 - lbb[:, cols]) * jax.nn.sigmoid(zb_ref[rows, cols])
        kf = 1.0 - ff
        kb = 1.0 - fb
        pre = _dot(tril3, _split3(jnp.log2(ff)))
        suf = _dot(triu3, _split3(jnp.log2(fb)))
        attn = jnp.where(lv == 0, _dot_nt(q.astype(BF16), (kf + kb).astype(BF16)), 0.0)
        odd = (rowi & 1) != 0
        qh = (q * jnp.where(odd, ff, fb)).astype(BF16)
        kh = jnp.where(odd, kb, kf).astype(BF16)
        attn = jnp.where(lv == 1, _dot_nt(qh, kh), attn)
        w, level = 2, 2
        while w < c:
            d_f = pre - _block_row(pre, w, w - 1)
            d_b = suf - _block_row(suf, w, w)
            xq = jnp.minimum(d_f, d_b)
            xk = xq - (d_f + d_b)
            second = (rowi & w) != 0
            qh = (q * jnp.exp2(xq)).astype(BF16)
            kh = (jnp.where(second, kb, kf) * jnp.exp2(xk)).astype(BF16)
            attn = jnp.where(lv == level, _dot_nt(qh, kh), attn)
            w *= 2
            level += 1
        acc_ref[hd, rows, :] = _dot(attn.astype(BF16), v)
        tot_f = pre[c - 1:c, :]
        tot_b = suf[0:1, :]
        qf_ref[hd, rows, :] = (q * jnp.exp2(pre)).astype(BF16)
        qb_ref[hd, rows, :] = (q * jnp.exp2(suf)).astype(BF16)
        kt = jnp.concatenate([(kf * jnp.exp2(tot_f - pre)).astype(BF16),
                              (kb * jnp.exp2(tot_b - suf)).astype(BF16)], axis=1)
        kv_ref[hd, n] = _dot(zi.T.astype(BF16), kt)
        dec_ref[hd, n] = jnp.exp2(jnp.concatenate([tot_f, tot_b], axis=1))

    def local_body(n, carry):
        for hd in range(hp):
            chunk_local(n, hd)
        return carry

    lax.fori_loop(0, nc, local_body, 0, unroll=min(nc, GLA_ROWS_IN_FLIGHT // (c * hp)))

    for hd in range(hp):
        if has_init:
            stf_ref[hd] = s0_ref[0, hd].T
            stb_ref[hd] = s0_ref[1, hd].T
        else:
            stf_ref[hd] = jnp.zeros((HGRN_DV, HGRN_DK), F32)
            stb_ref[hd] = jnp.zeros((HGRN_DV, HGRN_DK), F32)

    def carry_state(i, carry):
        n, m = i, nc - 1 - i
        rn = pl.ds(pl.multiple_of(n * c, c), c)
        rm = pl.ds(pl.multiple_of(m * c, c), c)
        for hd in range(hp):
            stf = stf_ref[hd]
            stb = stb_ref[hd]
            acc_ref[hd, rn, :] += _dot_nt(qf_ref[hd, rn, :], stf.astype(BF16))
            acc_ref[hd, rm, :] += _dot_nt(qb_ref[hd, rm, :], stb.astype(BF16))
            stf_ref[hd] = stf * dec_ref[hd, n][:, :HGRN_DK] + kv_ref[hd, n][:, :HGRN_DK]
            stb_ref[hd] = stb * dec_ref[hd, m][:, HGRN_DK:] + kv_ref[hd, m][:, HGRN_DK:]
        return carry

    lax.fori_loop(0, nc, carry_state, 0, unroll=True)
    for hd in range(hp):
        cols = slice(hd * LANES, (hd + 1) * LANES)
        o_ref[:, cols] = (_rms(acc_ref[hd]) * ng_ref[...] * _silu(zg_ref[:, cols])).astype(o_ref.dtype)
        if emit_state:
            sout_ref[0, hd] = stf_ref[hd].T
            sout_ref[1, hd] = stb_ref[hd].T


def _pair_levels():
    t = np.arange(GLA_CHUNK)
    x = t[:, None] ^ t[None, :]
    return jnp.asarray(np.where(x > 0, np.floor(np.log2(np.maximum(x, 1))) + 1, 0).astype(np.int32))


def _hgrn_core(z, lb, ng, s0, *, j, batch, seq, row_off, li, emit_state):
    has_init = s0 is not None
    blk0 = row_off // seq
    fd = HGRN_FD // LANES
    dm = D_MODEL // LANES
    c = GLA_CHUNK
    hp = max(1, GLA_ROWS_IN_FLIGHT // seq)
    width = hp * LANES

    def col(c0):
        return pl.BlockSpec((seq, width), lambda b, h: (blk0 + b, c0 // hp + h))

    in_specs = [
        pl.BlockSpec((2, DEPTH, width), lambda b, h: (0, 0, h)),
        pl.BlockSpec((c, c), lambda b, h: (0, 0)),
        col(0), col(fd), col(fd + dm), col(2 * fd + dm), col(3 * fd + dm),
        pl.BlockSpec((None, 1, HGRN_DV), lambda b, h: (j, 0, 0)),
    ]
    args = [lb, _pair_levels(), z, z, z, z, z, ng.reshape(-1, 1, HGRN_DV)]
    if has_init:
        in_specs.append(pl.BlockSpec((None, None, 2, hp, HGRN_DK, HGRN_DV),
                                     lambda b, h: (b, j, 0, h, 0, 0)))
        args.append(s0)
    out_shape = [jax.ShapeDtypeStruct((batch * seq, D_MODEL), BF16)]
    out_specs = [pl.BlockSpec((seq, width), lambda b, h: (b, h))]
    if emit_state:
        out_shape.append(jax.ShapeDtypeStruct((batch, 2, HGRN_HEADS, HGRN_DK, HGRN_DV), F32))
        out_specs.append(pl.BlockSpec((None, 2, hp, HGRN_DK, HGRN_DV), lambda b, h: (b, 0, h, 0, 0)))
    return pl.pallas_call(
        functools.partial(_hgrn_kernel, seq=seq, li=li, hp=hp, has_init=has_init, emit_state=emit_state),
        out_shape=tuple(out_shape),
        grid=(batch, HGRN_HEADS // hp),
        in_specs=in_specs,
        out_specs=tuple(out_specs),
        scratch_shapes=[
            pltpu.VMEM((hp, seq, HGRN_DV), F32),
            pltpu.VMEM((hp, seq, HGRN_DK), BF16),
            pltpu.VMEM((hp, seq, HGRN_DK), BF16),
            pltpu.VMEM((hp, seq // c, HGRN_DV, 2 * HGRN_DK), F32),
            pltpu.VMEM((hp, seq // c, 1, 2 * HGRN_DK), F32),
            pltpu.VMEM((hp, HGRN_DV, HGRN_DK), F32),
            pltpu.VMEM((hp, HGRN_DV, HGRN_DK), F32),
        ],
        compiler_params=_params(("arbitrary", "arbitrary")),
        name=f"hgrn_core_{seq}",
    )(*args)


def kernel(x_prompt, x_sample, cache_k, cache_v, state_hgrn, c, c_ctx, w_ada, b_ada, norm_g, w_ffn_in, w_ffn_out, w_attn_qkv, w_attn_o, attn_sink, w_conv_in, conv_w, conv_b, w_conv_out, w_hgrn_in, hgrn_lb, hgrn_norm_g, w_hgrn_out):
    cond = jnp.concatenate([c, c_ctx[None, :],
                            jnp.zeros((N_COND - DEC_BATCH - 1, D_MODEL), F32)], axis=0)
    b_ada = b_ada.reshape(DEPTH, 1, ADA_COLS)
    mods = _ada_table(cond, w_ada, b_ada, 0)
    cos, sin = _rope_tables()
    ck = cache_k.transpose(0, 1, 3, 4, 2).reshape(DEC_BATCH, -1, KV_DIM, PAST_LEN)
    cv = cache_v.transpose(0, 1, 3, 4, 2).reshape(DEC_BATCH, -1, KV_DIM, PAST_LEN)
    cache_out = lambda t: t.reshape(BATCH, -1, ATTN_KV_HEADS, HEAD_DIM, SEQ).transpose(0, 1, 4, 2, 3)
    bf = lambda w: w.astype(BF16)
    xs = (x_sample.reshape(N_SAMPLE_TOK, D_MODEL), x_prompt.reshape(N_PROMPT_TOK, D_MODEL))
    ffn_w = (bf(w_ffn_in[0, 0]), bf(w_ffn_out[0, 0]))
    new_cache, new_s = (), []
    for li in range(DEPTH):
        kind, j = li % N_MIXERS, li // N_MIXERS
        last = li == DEPTH - 1
        mixer = None
        ada = None if last else (cond, w_ada, b_ada, li + 1)
        (x,), ffn_w, mods_next = _ffn(xs, mods, norm_g, ffn_w, (w_ffn_in, w_ffn_out, li, 1), li, 0, ada=ada)
        if kind == 0:
            q, k, v, *new_cache = _qkv(x, mods, norm_g, w_attn_qkv, li, j, cos, sin, new_cache)
            o_s = _attn_latent(attn_sink[j], q, k, v, ck, cv, j)
            o_p = _attn_context(attn_sink[j], q, k, v)
            mixer = (o_s, o_p, w_attn_o, j)
        elif kind == 1:
            x = _conv_layer(x, mods, norm_g, w_conv_in, conv_w, conv_b, w_conv_out, li, j)
        else:
            z = _proj(x, mods, norm_g, w_hgrn_in, li, j, F32)
            o_s = _hgrn_core(z, hgrn_lb, hgrn_norm_g, state_hgrn, j=j, batch=DEC_BATCH,
                             seq=DEC_SEQ, row_off=0, li=li, emit_state=False)[0]
            o_p, st = _hgrn_core(z, hgrn_lb, hgrn_norm_g, None, j=j, batch=BATCH, seq=SEQ,
                                 row_off=N_SAMPLE_TOK, li=li, emit_state=True)
            mixer = (o_s, o_p, w_hgrn_out, j)
            new_s.append(st)
        nxt = None if last else (w_ffn_in, w_ffn_out, li + 1, 0)
        xs, ffn_w, _ = _ffn((x,), mods, norm_g, ffn_w, nxt, li, 1, split_out=last, mixer=mixer)
        mods = mods_next
    y_sample = xs[0].reshape(DEC_BATCH, DEC_SEQ, D_MODEL)
    y_prompt = xs[1].reshape(BATCH, SEQ, D_MODEL)
    return (y_prompt, y_sample, cache_out(new_cache[0]), cache_out(new_cache[1]), jnp.stack(new_s, axis=1))
```
